```python
import jax
import jax.numpy as jnp
from jax import lax
import numpy as np

D_MODEL = 1024
BATCH = 32
SEQ = 256
DEPTH = 2
DEC_BATCH = 2
DEC_SEQ = 2048
PAST_LEN = 512

GRID_W = 64
N_BRANCH = 4
BRANCH_W = 512
H_A = 8
N_A = 64
LORA_W = 64
LORA_A = 64
RWKV_GN_EPS = 64e-5
H_B = 8
NOPE = 64
ROPE = 32
VDIM = 64
Q_LORA = 256
KV_LORA = 128
ROPE_BASE = 10000.0
Q_BLOCK = 128
G_C = 4
CHUNK = 128
G_D = 4
NORM_EPS = 1e-6
IN_W = 5728
SHIFT_W = 1728
IN_SPLITS = (512, 1024, 1536, 1600, 1664, 1728, 2240, 2496, 2624, 2656, 3168, 3680, 4192, 4704, 5216)

kernel_name = 'hybrid_rwkv_mla_gmlp_fnet_diffusion_step'


def rms_norm(x, g):
    xf = x.astype(jnp.float32)
    y = xf * lax.rsqrt(jnp.mean(xf * xf, axis=-1, keepdims=True) + NORM_EPS)
    return (y * g.astype(jnp.float32)).astype(x.dtype)


def layer_norm(x, g, b):
    xf = x.astype(jnp.float32)
    mu = jnp.mean(xf, axis=-1, keepdims=True)
    var = jnp.mean(jnp.square(xf - mu), axis=-1, keepdims=True)
    y = (xf - mu) * lax.rsqrt(var + 1e-5) * g.astype(jnp.float32) + b.astype(jnp.float32)
    return y.astype(x.dtype)


def head_group_norm(o, g, b):
    B, T, H, N = o.shape
    mu = jnp.mean(o, axis=-1, keepdims=True)
    var = jnp.mean(jnp.square(o - mu), axis=-1, keepdims=True)
    y = ((o - mu) * lax.rsqrt(var + RWKV_GN_EPS)).reshape(B, T, H * N)
    return y * g.astype(jnp.float32) + b.astype(jnp.float32)


def centred_shift(z):
    prev = jnp.pad(z[:, :-1], ((0, 0), (1, 0), (0, 0)))
    nxt = jnp.pad(z[:, 1:], ((0, 0), (0, 1), (0, 0)))
    return 0.5 * (prev + nxt)


def axial_rope(n_tokens):
    rows = n_tokens // GRID_W
    row = jnp.repeat(jnp.arange(rows, dtype=jnp.float32), GRID_W)
    col = jnp.tile(jnp.arange(GRID_W, dtype=jnp.float32), rows)
    n_freq = ROPE // 4
    inv = ROPE_BASE ** (-jnp.arange(n_freq, dtype=jnp.float32) / n_freq)
    ang = jnp.concatenate([row[:, None] * inv, col[:, None] * inv], axis=-1)
    return jnp.cos(ang)[:, None, :], jnp.sin(ang)[:, None, :]


def apply_rope(x, cos, sin):
    xf = x.astype(jnp.float32).reshape(*x.shape[:-1], ROPE // 2, 2)
    x0, x1 = xf[..., 0], xf[..., 1]
    out = jnp.stack([x0 * cos - x1 * sin, x0 * sin + x1 * cos], axis=-1)
    return out.reshape(x.shape).astype(x.dtype)


def block_attention(q, k, v):
    B, S, H, Dq = q.shape
    scale = (NOPE + ROPE) ** -0.5
    qb = jnp.moveaxis(q.reshape(B, S // Q_BLOCK, Q_BLOCK, H, Dq), 1, 0)

    def one_block(qi):
        s = jnp.einsum('bqhd,bkhd->bhqk', qi, k).astype(jnp.float32) * scale
        pr = jax.nn.softmax(s, axis=-1).astype(v.dtype)
        return jnp.einsum('bhqk,bkhd->bqhd', pr, v)

    o = lax.map(one_block, qb)
    return jnp.moveaxis(o, 0, 1).reshape(B, S, H, v.shape[-1])


def wkv_scan(r, w, k, v, kk, a, s0, reverse):
    xs = tuple(jnp.moveaxis(t, 1, 0) for t in (r, w, k, v, kk, a))

    def step(S, inp):
        r_t, w_t, k_t, v_t, kk_t, a_t = inp
        sa = jnp.einsum('bhvk,bhk->bhv', S, -kk_t)
        S = (S * w_t[:, :, None, :] + sa[..., None] * (kk_t * a_t)[:, :, None, :]
             + v_t[..., None] * k_t[:, :, None, :])
        return S, jnp.einsum('bhvk,bhk->bhv', S, r_t)

    s_fin, ys = lax.scan(step, s0, xs, reverse=reverse)
    return jnp.moveaxis(ys, 0, 1), s_fin


def rwkv_branch(r, k, v, wdf, wdb, ad, g, p, s0_f, s0_b):
    B, T, _ = r.shape
    f32 = jnp.float32

    def heads(t):
        return t.astype(f32).reshape(B, T, H_A, N_A)

    rh, kh, vh = heads(r), heads(k), heads(v)
    kk = heads(k * p['rwkv_k_k'])
    kk = kk / jnp.maximum(jnp.linalg.norm(kk, axis=-1, keepdims=True), 1e-12)
    k_a = p['rwkv_k_a'].astype(f32).reshape(H_A, N_A)
    r_k = p['rwkv_r_k'].astype(f32)
    outs, bonuses, finals = [], [], []
    for d, (wd, s0) in enumerate(((wdf, s0_f), (wdb, s0_b))):
        pre = (p['rwkv_w0'][d] + jnp.tanh(wd) @ p['rwkv_w_up'][d]).astype(f32)
        decay = jnp.exp(-jnp.exp(-jax.nn.softplus(-pre) - 0.5))
        ah = heads(jax.nn.sigmoid(p['rwkv_a0'][d] + ad @ p['rwkv_a_up'][d]))
        kt = kh * (1.0 + (ah - 1.0) * k_a)
        od, sf = wkv_scan(rh, heads(decay), kt, vh, kk, ah, s0.astype(f32), reverse=(d == 1))
        outs.append(od)
        bonuses.append(jnp.sum(rh * kt * r_k, axis=-1, keepdims=True) * vh)
        finals.append(sf)
    y = head_group_norm(outs[0] + outs[1], p['rwkv_ln_g'], p['rwkv_ln_b'])
    y = y + (bonuses[0] + bonuses[1]).reshape(B, T, BRANCH_W)
    return y.astype(r.dtype) * jax.nn.silu(g), finals[0], finals[1]


def mla_queries(qd, kvd, kr, p, rope):
    B, T, _ = qd.shape
    q = (rms_norm(qd, p['mla_q_norm']) @ p['mla_w_q_up']).reshape(B, T, H_B, NOPE + ROPE)
    ckv = rms_norm(kvd, p['mla_kv_norm'])
    kr = kr.reshape(B, T, 1, ROPE)
    q_nope, q_rope = q[..., :NOPE], q[..., NOPE:]
    if rope is not None:
        q_rope = apply_rope(q_rope, rope[0], rope[1])
        kr = apply_rope(kr, rope[0], rope[1])
    return jnp.concatenate([q_nope, q_rope], axis=-1), ckv, kr.reshape(B, T, ROPE)


def mla_expand(ckv, kr, p):
    B, T, _ = ckv.shape
    kv = (ckv @ p['mla_w_kv_up']).reshape(B, T, H_B, NOPE + VDIM)
    k_rope = jnp.broadcast_to(kr.reshape(B, T, 1, ROPE), (B, T, H_B, ROPE)).astype(kv.dtype)
    return jnp.concatenate([kv[..., :NOPE], k_rope], axis=-1), kv[..., NOPE:]


def gmlp_branch(u, vc, g, p):
    B, T, _ = vc.shape
    u = jax.nn.gelu(u)
    vc = layer_norm(jax.nn.gelu(vc), p['gmlp_ln_g'], p['gmlp_ln_b'])
    vg = vc.reshape(B, T // CHUNK, CHUNK, G_C, BRANCH_W // G_C)
    mixed = jnp.einsum('gpq,bcqgd->bcpgd', p['gmlp_w_s'], vg) + p['gmlp_b_s'].T[:, :, None]
    return u * mixed.reshape(B, T, BRANCH_W) * jax.nn.silu(g)


def fourier_branch(f, g):
    B, T, _ = f.shape
    fg = f.astype(jnp.float32).reshape(B, T, G_D, BRANCH_W // G_D)
    y = jnp.fft.fft2(fg, axes=(1, 3), norm='ortho').real
    return y.reshape(B, T, BRANCH_W).astype(f.dtype) * jax.nn.silu(g)


def trunk_layer(x, cond, p, rope, ctx):
    B, T, _ = x.shape
    mod = jax.nn.silu(cond) @ p['w_ada'] + p['b_ada']
    shift, scale, gate = jnp.split(mod[..., None, :], 3, axis=-1)
    h = rms_norm(x, p['norm_g']) * (1.0 + scale) + shift
    z = h @ p['w_in']
    zs = z[..., :SHIFT_W]
    zs = zs + (centred_shift(zs) - zs) * p['shift_mu']
    z = jnp.concatenate([zs, z[..., SHIFT_W:]], axis=-1)
    (r, k, v, wdf, wdb, ad, g_a, qd, kvd, kr, g_b, u, vc, g_c, f, g_d) = jnp.split(z, IN_SPLITS, axis=-1)
    if ctx is None:
        s0_f = jnp.zeros((B, H_A, N_A, N_A), jnp.float32)
        s0_b = s0_f
    else:
        s0_f, s0_b, ckv_c, kr_c = ctx
    o_a, s_f, s_b = rwkv_branch(r, k, v, wdf, wdb, ad, g_a, p, s0_f, s0_b)
    q, ckv, kr_t = mla_queries(qd, kvd, kr, p, rope)
    k_all, v_all = mla_expand(ckv, kr_t, p)
    if ctx is not None:
        k_c, v_c = mla_expand(ckv_c.astype(ckv.dtype), kr_c, p)
        k_all = jnp.concatenate([k_all, k_c], axis=1)
        v_all = jnp.concatenate([v_all, v_c], axis=1)
    o_b = block_attention(q, k_all, v_all).reshape(B, T, H_B * VDIM) * jax.nn.silu(g_b)
    o_c = gmlp_branch(u, vc, g_c, p)
    o_d = fourier_branch(f, g_d)
    proj = jnp.einsum('nbtw,nwd->nbtd', jnp.stack([o_a, o_b, o_c, o_d]), p['w_branch'])
    gates = jax.nn.sigmoid(h @ p['w_merge'] + p['b_merge']).reshape(B, T, N_BRANCH, D_MODEL)
    merged = jnp.einsum('nbtd,btnd->btd', proj, gates)
    x = x + gate * (merged @ p['w_out'])
    return x, s_f, s_b, ckv, kr_t


def setup_inputs(seed: int = 0) -> dict:
    key = jax.random.key(seed)
    keys = list(jax.random.split(key, 40))
    f32 = jnp.float32

    def nrm(i, shape, s):
        return jax.random.normal(keys[i], shape, f32) * s

    def unif(i, shape, lo, hi):
        return jax.random.uniform(keys[i], shape, f32, lo, hi)

    D = D_MODEL
    return {
        'x_prompt': nrm(0, (BATCH, SEQ, D), 1.0),
        'x_sample': nrm(1, (DEC_BATCH, DEC_SEQ, D), 1.0),
        'state_rwkv_fwd': nrm(2, (DEC_BATCH, DEPTH, H_A, N_A, N_A), 0.3),
        'state_rwkv_bwd': nrm(3, (DEC_BATCH, DEPTH, H_A, N_A, N_A), 0.3),
        'cache_mla_ckv': nrm(4, (DEC_BATCH, DEPTH, PAST_LEN, KV_LORA), 1.0),
        'cache_mla_krope': nrm(5, (DEC_BATCH, DEPTH, PAST_LEN, ROPE), 1.0),
        'c': nrm(6, (DEC_BATCH, D), 1.0),
        'c_ctx': nrm(7, (D,), 1.0),
        'norm_g': 1.0 + nrm(8, (DEPTH, D), 0.02),
        'w_ada': nrm(9, (DEPTH, D, 3 * D), 0.5 * D ** -0.5),
        'b_ada': nrm(10, (DEPTH, 3 * D), 0.02),
        'w_in': nrm(11, (DEPTH, D, IN_W), D ** -0.5),
        'shift_mu': unif(12, (DEPTH, SHIFT_W), 0.0, 1.0),
        'rwkv_w0': unif(13, (DEPTH, 2, BRANCH_W), -6.0, 1.0),
        'rwkv_w_up': nrm(14, (DEPTH, 2, LORA_W, BRANCH_W), 0.5 * LORA_W ** -0.5),
        'rwkv_a0': nrm(15, (DEPTH, 2, BRANCH_W), 0.5),
        'rwkv_a_up': nrm(16, (DEPTH, 2, LORA_A, BRANCH_W), 0.3 * LORA_A ** -0.5),
        'rwkv_k_k': 0.85 + nrm(17, (DEPTH, BRANCH_W), 0.02),
        'rwkv_k_a': 1.0 + nrm(18, (DEPTH, BRANCH_W), 0.02),
        'rwkv_r_k': nrm(19, (DEPTH, H_A, N_A), 0.1),
        'rwkv_ln_g': 1.0 + nrm(20, (DEPTH, BRANCH_W), 0.02),
        'rwkv_ln_b': nrm(21, (DEPTH, BRANCH_W), 0.02),
        'mla_q_norm': 1.0 + nrm(22, (DEPTH, Q_LORA), 0.02),
        'mla_w_q_up': nrm(23, (DEPTH, Q_LORA, H_B * (NOPE + ROPE)), Q_LORA ** -0.5),
        'mla_kv_norm': 1.0 + nrm(24, (DEPTH, KV_LORA), 0.02),
        'mla_w_kv_up': nrm(25, (DEPTH, KV_LORA, H_B * (NOPE + VDIM)), KV_LORA ** -0.5),
        'gmlp_ln_g': 1.0 + nrm(26, (DEPTH, BRANCH_W), 0.02),
        'gmlp_ln_b': nrm(27, (DEPTH, BRANCH_W), 0.02),
        'gmlp_w_s': nrm(28, (DEPTH, G_C, CHUNK, CHUNK), CHUNK ** -0.5),
        'gmlp_b_s': 1.0 + nrm(29, (DEPTH, G_C, CHUNK), 0.02),
        'w_branch': nrm(30, (DEPTH, N_BRANCH, BRANCH_W, D), BRANCH_W ** -0.5),
        'w_merge': nrm(31, (DEPTH, D, N_BRANCH * D), D ** -0.5),
        'b_merge': nrm(32, (DEPTH, N_BRANCH * D), 0.02),
        'w_out': nrm(33, (DEPTH, D, D), D ** -0.5),
        'final_norm_g': 1.0 + nrm(34, (D,), 0.02),
    }


def reference(x_prompt, x_sample, state_rwkv_fwd, state_rwkv_bwd, cache_mla_ckv, cache_mla_krope,
              c, c_ctx, norm_g, w_ada, b_ada, w_in, shift_mu, rwkv_w0, rwkv_w_up, rwkv_a0, rwkv_a_up,
              rwkv_k_k, rwkv_k_a, rwkv_r_k, rwkv_ln_g, rwkv_ln_b, mla_q_norm, mla_w_q_up, mla_kv_norm,
              mla_w_kv_up, gmlp_ln_g, gmlp_ln_b, gmlp_w_s, gmlp_b_s, w_branch, w_merge, b_merge, w_out,
              final_norm_g):
    rope = axial_rope(x_sample.shape[1])
    xc = x_prompt
    xl = x_sample
    sf_list, sb_list, ckv_list, kr_list = [], [], [], []
    for l in range(DEPTH):
        p = {
            'norm_g': norm_g[l], 'w_ada': w_ada[l], 'b_ada': b_ada[l], 'w_in': w_in[l],
            'shift_mu': shift_mu[l], 'rwkv_w0': rwkv_w0[l], 'rwkv_w_up': rwkv_w_up[l],
            'rwkv_a0': rwkv_a0[l], 'rwkv_a_up': rwkv_a_up[l], 'rwkv_k_k': rwkv_k_k[l],
            'rwkv_k_a': rwkv_k_a[l], 'rwkv_r_k': rwkv_r_k[l], 'rwkv_ln_g': rwkv_ln_g[l],
            'rwkv_ln_b': rwkv_ln_b[l], 'mla_q_norm': mla_q_norm[l], 'mla_w_q_up': mla_w_q_up[l],
            'mla_kv_norm': mla_kv_norm[l], 'mla_w_kv_up': mla_w_kv_up[l], 'gmlp_ln_g': gmlp_ln_g[l],
            'gmlp_ln_b': gmlp_ln_b[l], 'gmlp_w_s': gmlp_w_s[l], 'gmlp_b_s': gmlp_b_s[l],
            'w_branch': w_branch[l], 'w_merge': w_merge[l], 'b_merge': b_merge[l], 'w_out': w_out[l],
        }
        xc, s_f, s_b, ckv, kr = trunk_layer(xc, c_ctx, p, None, None)
        sf_list.append(s_f)
        sb_list.append(s_b)
        ckv_list.append(ckv)
        kr_list.append(kr)
        ctx = (state_rwkv_fwd[:, l], state_rwkv_bwd[:, l], cache_mla_ckv[:, l], cache_mla_krope[:, l])
        xl = trunk_layer(xl, c, p, rope, ctx)[0]
    y_prompt = rms_norm(xc, final_norm_g)
    y_sample = rms_norm(xl, final_norm_g)
    new_state_rwkv_fwd = jnp.stack(sf_list, axis=1).astype(x_prompt.dtype)
    new_state_rwkv_bwd = jnp.stack(sb_list, axis=1).astype(x_prompt.dtype)
    new_cache_mla_ckv = jnp.stack(ckv_list, axis=1)
    new_cache_mla_krope = jnp.stack(kr_list, axis=1)
    return (y_prompt, y_sample, new_state_rwkv_fwd, new_state_rwkv_bwd, new_cache_mla_ckv, new_cache_mla_krope)
```

```python
import functools
import math

import jax
import jax.numpy as jnp
import numpy as np
from jax import lax
from jax.experimental import pallas as pl
from jax.experimental.pallas import tpu as pltpu

F32 = jnp.float32
BF16 = jnp.bfloat16

D_MODEL = 1024
DEPTH = 2
GRID_W = 64
BRANCH_W = 512
H_A = 8
N_A = 64
LORA = 64
RWKV_GN_EPS = 64e-5
H_B = 8
NOPE = 64
ROPE = 32
VDIM = 64
Q_LORA = 256
KV_LORA = 128
ROPE_BASE = 10000.0
G_C = 4
CHUNK = 128
G_D = 4
NORM_EPS = 1e-6
SHIFT_W = 1728

LANES = 128
VMEM_LIMIT = 52 * 1024 * 1024

ZW = 6144
Z_SHIFT_PAD = 2048
COL_R, COL_K, COL_V = 0, 1, 2
COL_LORA = 6
COL_GA, COL_ZB, COL_GB, COL_U, COL_VC, COL_GC, COL_F, COL_GD = 4, 5, 6, 7, 8, 9, 10, 11
FRONT_TM = 2048
FRONT_TN = 256
N_SHIFT_TILES = 1792 // FRONT_TN
SCAN_C = 64


def _mm(a, b):
    return jnp.dot(a.astype(BF16), b.astype(BF16), preferred_element_type=F32)


def _mm_nt(a, b):
    return lax.dot_general(a.astype(BF16), b.astype(BF16), (((1,), (1,)), ((), ())),
                           preferred_element_type=F32)


def _split3(x):
    hi = x.astype(BF16)
    r1 = x - hi.astype(F32)
    mid = r1.astype(BF16)
    lo = (r1 - mid.astype(F32)).astype(BF16)
    return hi, mid, lo


def _mm_exact_rhs01(x, m01):
    hi, mid, lo = _split3(x)
    d = functools.partial(jnp.dot, preferred_element_type=F32)
    return d(hi, m01) + d(mid, m01) + d(lo, m01)


def _mm_exact_lhs01(m01, x):
    hi, mid, lo = _split3(x)
    d = functools.partial(jnp.dot, preferred_element_type=F32)
    return d(m01, hi) + d(m01, mid) + d(m01, lo)


def _silu(x):
    return x * jax.nn.sigmoid(x)


def _rms(x, g):
    return x * lax.rsqrt(jnp.mean(x * x, axis=-1, keepdims=True) + NORM_EPS) * g


def _cparams(sem, vmem=VMEM_LIMIT):
    return pltpu.CompilerParams(dimension_semantics=sem, vmem_limit_bytes=vmem)


def _mod_kernel(c_ref, w_ref, b_ref, o_ref):
    o_ref[0] = _mm(_silu(c_ref[...]), w_ref[0]) + b_ref[0]


def _modulation(cond8, w_ada, b_ada):
    return pl.pallas_call(
        _mod_kernel,
        grid=(DEPTH, 3),
        in_specs=[
            pl.BlockSpec((8, D_MODEL), lambda l, j: (0, 0)),
            pl.BlockSpec((1, D_MODEL, D_MODEL), lambda l, j: (l, 0, j)),
            pl.BlockSpec((1, 1, D_MODEL), lambda l, j: (l, 0, j)),
        ],
        out_specs=pl.BlockSpec((1, 8, D_MODEL), lambda l, j: (l, 0, j)),
        out_shape=jax.ShapeDtypeStruct((DEPTH, 8, 3 * D_MODEL), F32),
        compiler_params=_cparams(("arbitrary", "arbitrary")),
        name="modulation",
    )(cond8, w_ada, b_ada.reshape(DEPTH, 1, 3 * D_MODEL))


def _front_kernel(x_ref, ss_ref, g_ref, w_ref, mu_ref, z_ref, h_scr, *, seq):
    j = pl.program_id(1)

    @pl.when(j == 0)
    def _():
        h = _rms(x_ref[...], g_ref[...]) * (1.0 + ss_ref[0, 1:2, :]) + ss_ref[0, 0:1, :]
        h_scr[...] = h.astype(BF16)

    z = jnp.dot(h_scr[...], w_ref[...], preferred_element_type=F32)

    @pl.when(j < N_SHIFT_TILES)
    def _():
        t = lax.broadcasted_iota(jnp.int32, z.shape, 0) & (seq - 1)
        prev = jnp.where(t == 0, 0.0, pltpu.roll(z, 1, axis=0))
        nxt = jnp.where(t == seq - 1, 0.0, pltpu.roll(z, FRONT_TM - 1, axis=0))
        z_ref[...] = z + (0.5 * (prev + nxt) - z) * mu_ref[...]

    @pl.when(j >= N_SHIFT_TILES)
    def _():
        z_ref[...] = z


def _front(x2d, ss, norm_g, w_in_p, mu_p, seq):
    n = x2d.shape[0]
    return pl.pallas_call(
        functools.partial(_front_kernel, seq=seq),
        grid=(n // FRONT_TM, ZW // FRONT_TN),
        in_specs=[
            pl.BlockSpec((FRONT_TM, D_MODEL), lambda i, j: (i, 0)),
            pl.BlockSpec((1, 2, D_MODEL), lambda i, j: (i, 0, 0)),
            pl.BlockSpec((1, D_MODEL), lambda i, j: (0, 0)),
            pl.BlockSpec((D_MODEL, FRONT_TN), lambda i, j: (0, j)),
            pl.BlockSpec((1, FRONT_TN), lambda i, j: (0, jnp.minimum(j, N_SHIFT_TILES - 1))),
        ],
        out_specs=pl.BlockSpec((FRONT_TM, FRONT_TN), lambda i, j: (i, j)),
        out_shape=jax.ShapeDtypeStruct((n, ZW), F32),
        scratch_shapes=[pltpu.VMEM((FRONT_TM, D_MODEL), BF16)],
        compiler_params=_cparams(("arbitrary", "arbitrary")),
        name="front",
    )(x2d, ss, norm_g, w_in_p, mu_p)


def _rwkv_prep_kernel(r_ref, k_ref, v_ref, lora_ref, wlt_ref, wla_ref, w0_ref, a0_ref,
                      kk_ref_, ka_ref, rk_ref, bd_ref,
                      kk_o, lwf_o, lwb_o, ktf_o, ktb_o, bf_o, bb_o, bonus_o):
    lora = lora_ref[...]
    pre = w0_ref[...] + _mm(jnp.tanh(lora), wlt_ref[...])
    lw = -jnp.exp(-jax.nn.softplus(-pre) - 0.5)
    a = jax.nn.sigmoid(a0_ref[...] + _mm(lora, wla_ref[...]))
    k = k_ref[...]
    bd = bd_ref[...]
    kkraw = k * kk_ref_[...]
    nrm = jnp.sqrt(_mm_exact_rhs01(kkraw * kkraw, bd))
    kk = kkraw / jnp.maximum(nrm, 1e-12)
    af = a[:, :BRANCH_W]
    ab = a[:, BRANCH_W:]
    ka = ka_ref[...]
    ktf = k * (1.0 + (af - 1.0) * ka)
    ktb = k * (1.0 + (ab - 1.0) * ka)
    rr = r_ref[...] * rk_ref[...]
    bonus = (_mm_exact_rhs01(rr * ktf, bd) + _mm_exact_rhs01(rr * ktb, bd)) * v_ref[...]
    kk_o[...] = kk
    lwf_o[...] = lw[:, :BRANCH_W]
    lwb_o[...] = lw[:, BRANCH_W:]
    ktf_o[...] = ktf
    ktb_o[...] = ktb
    bf_o[...] = kk * af
    bb_o[...] = kk * ab
    bonus_o[...] = bonus


def _rwkv_prep(z, wp, tm=256):
    n = z.shape[0]
    zb = lambda c: pl.BlockSpec((tm, BRANCH_W), lambda i, c=c: (i, c))
    full = lambda a: pl.BlockSpec(a.shape, lambda i: (0,) * a.ndim)
    params = (wp["wlt"], wp["wla"], wp["w0"], wp["a0"], wp["k_k"], wp["k_a"], wp["r_k"], wp["bd"])
    ob = pl.BlockSpec((tm, BRANCH_W), lambda i: (i, 0))
    return pl.pallas_call(
        _rwkv_prep_kernel,
        grid=(n // tm,),
        in_specs=[zb(COL_R), zb(COL_K), zb(COL_V),
                  pl.BlockSpec((tm, 256), lambda i: (i, COL_LORA))] + [full(p) for p in params],
        out_specs=[ob] * 8,
        out_shape=[jax.ShapeDtypeStruct((n, BRANCH_W), F32)] * 8,
        compiler_params=_cparams(("arbitrary",)),
        name="rwkv_prep",
    )(z, z, z, z, *params)


def _unit_lower_inverse(lab, same16, eye):
    d = jnp.where(same16, lab, 0.0)
    e = lab - d
    p = eye + d
    dk = d
    for _ in range(3):
        dk = _mm(dk, dk)
        p = p + _mm(p, dk)
    f = _mm(p, e)
    f2 = _mm(f, f)
    q = eye + f + f2 + _mm(f, f2)
    return _mm(q, p)


def _scan_chunk(r, kk, v, lw, kt, bv, s_ref, d, rev):
    c = SCAN_C
    ri = lax.broadcasted_iota(jnp.int32, (c, c), 0)
    ci = lax.broadcasted_iota(jnp.int32, (c, c), 1)
    tri = ((ri <= ci) if rev else (ri >= ci)).astype(BF16)
    cs = _mm_exact_lhs01(tri, lw)
    tot = cs[0:1] if rev else cs[c - 1:c]
    g_tot = jnp.exp(tot)
    a_t = -kk * jnp.exp(cs - lw)
    r_t = r * jnp.exp(cs)
    g_inv = jnp.exp(-cs)
    b_t = bv * g_inv
    k_t = kt * g_inv
    g_rem = jnp.exp(tot - cs)
    b_h = bv * g_rem
    k_h = kt * g_rem

    n2 = 2 * c
    row = lax.broadcasted_iota(jnp.int32, (n2, n2), 0)
    col = lax.broadcasted_iota(jnp.int32, (n2, n2), 1)
    same64 = (row >> 6) == (col >> 6)
    same16 = (row >> 4) == (col >> 4)
    tl = row & (c - 1)
    il = col & (c - 1)
    strict = same64 & ((il > tl) if rev else (il < tl))
    incl = same64 & ((il >= tl) if rev else (il <= tl))
    eye = (row == col).astype(F32)
    lane_a = lax.broadcasted_iota(jnp.int32, (c, LANES), 1) < N_A

    def stack2(x):
        return jnp.concatenate([jnp.where(lane_a, x, 0.0), jnp.where(lane_a, 0.0, x)], axis=0)

    def dup2(x):
        return jnp.concatenate([x, x], axis=0)

    ys = []
    for p in range(H_A // 2):
        sl = slice(p * LANES, (p + 1) * LANES)
        a2 = stack2(a_t[:, sl])
        r2 = stack2(r_t[:, sl])
        v2 = stack2(v[:, sl])
        lhs = jnp.concatenate([a2, r2], axis=0)
        rhs = jnp.concatenate([dup2(b_t[:, sl]), dup2(k_t[:, sl])], axis=0)
        big = _mm_nt(lhs, rhs)
        lab = jnp.where(strict, big[:n2, :n2], 0.0)
        lak = jnp.where(strict, big[:n2, n2:], 0.0)
        mrb = jnp.where(incl, big[n2:, :n2], 0.0)
        mrk = jnp.where(incl, big[n2:, n2:], 0.0)
        tinv = _unit_lower_inverse(lab, same16, eye)
        s = s_ref[d, p]
        as_ = _mm_nt(lhs, s)
        u2 = _mm(tinv, as_[:n2] + _mm(lak, v2))
        uv = jnp.concatenate([u2, v2], axis=0)
        y2 = as_[n2:] + _mm(jnp.concatenate([mrb, mrk], axis=1), uv)
        bk = jnp.concatenate([stack2(b_h[:, sl]), stack2(k_h[:, sl])], axis=0)
        s_ref[d, p] = s * g_tot[:, sl] + _mm(uv.T, bk)
        ys.append(y2[:c] + y2[c:])
    return jnp.concatenate(ys, axis=1)


def _scan_kernel(*refs, has_init):
    if has_init:
        s0_ref, refs = refs[0], refs[1:]
    fwd, bwd = refs[0:6], refs[6:12]
    yf_ref, yb_ref, sfin_ref, s_scr = refs[12:16]
    c = pl.program_id(1)

    @pl.when(c == 0)
    def _():
        if has_init:
            s_scr[...] = s0_ref[0]
        else:
            s_scr[...] = jnp.zeros_like(s_scr)

    yf_ref[...] = _scan_chunk(*[x[...] for x in fwd], s_scr, 0, False)
    yb_ref[...] = _scan_chunk(*[x[...] for x in bwd], s_scr, 1, True)

    @pl.when(c == pl.num_programs(1) - 1)
    def _():
        sfin_ref[0] = s_scr[...]


def _rwkv_scan(z, prep, s0_bd, nb, seq):
    n = z.shape[0]
    nc = seq // SCAN_C
    kk, lwf, lwb, ktf, ktb, bf, bb, _ = prep
    fi = lambda b, c: b * nc + c
    bi = lambda b, c: b * nc + (nc - 1 - c)
    spec = lambda idx, col=0: pl.BlockSpec((SCAN_C, BRANCH_W), lambda b, c: (idx(b, c), col))
    in_specs = [spec(fi, COL_R), spec(fi), spec(fi, COL_V), spec(fi), spec(fi), spec(fi),
                spec(bi, COL_R), spec(bi), spec(bi, COL_V), spec(bi), spec(bi), spec(bi)]
    args = [z, kk, z, lwf, ktf, bf, z, kk, z, lwb, ktb, bb]
    sspec = pl.BlockSpec((1, 2, H_A // 2, LANES, LANES), lambda b, c: (b, 0, 0, 0, 0))
    has_init = s0_bd is not None
    if has_init:
        in_specs = [sspec] + in_specs
        args = [s0_bd] + args
    return pl.pallas_call(
        functools.partial(_scan_kernel, has_init=has_init),
        grid=(nb, nc),
        in_specs=in_specs,
        out_specs=[spec(fi), spec(bi), sspec],
        out_shape=[jax.ShapeDtypeStruct((n, BRANCH_W), F32),
                   jax.ShapeDtypeStruct((n, BRANCH_W), F32),
                   jax.ShapeDtypeStruct((nb, 2, H_A // 2, LANES, LANES), F32)],
        scratch_shapes=[pltpu.VMEM((2, H_A // 2, LANES, LANES), F32)],
        compiler_params=_cparams(("arbitrary", "arbitrary")),
        name="rwkv_scan",
    )(*args)


def _rwkv_post_kernel(yf_ref, yb_ref, bonus_ref, g_ref, lng_ref, lnb_ref, bd_ref, o_ref):
    bd = bd_ref[...]
    o = yf_ref[...] + yb_ref[...]
    mu = _mm_exact_rhs01(o, bd) * (1.0 / N_A)
    dlt = o - mu
    var = _mm_exact_rhs01(dlt * dlt, bd) * (1.0 / N_A)
    y = dlt * lax.rsqrt(var + RWKV_GN_EPS) * lng_ref[...] + lnb_ref[...] + bonus_ref[...]
    o_ref[...] = y * _silu(g_ref[...])


def _rwkv_post(z, yf, yb, bonus, wp, tm=1024):
    n = z.shape[0]
    rb = pl.BlockSpec((tm, BRANCH_W), lambda i: (i, 0))
    full = lambda a: pl.BlockSpec(a.shape, lambda i: (0,) * a.ndim)
    return pl.pallas_call(
        _rwkv_post_kernel,
        grid=(n // tm,),
        in_specs=[rb, rb, rb, pl.BlockSpec((tm, BRANCH_W), lambda i: (i, COL_GA)),
                  full(wp["ln_g"]), full(wp["ln_b"]), full(wp["bd"])],
        out_specs=rb,
        out_shape=jax.ShapeDtypeStruct((n, BRANCH_W), F32),
        compiler_params=_cparams(("arbitrary",)),
        name="rwkv_post",
    )(yf, yb, bonus, z, wp["ln_g"], wp["ln_b"], wp["bd"])


def _swap_pairs(x):
    w = x.shape[1]
    even = (lax.broadcasted_iota(jnp.int32, x.shape, 1) & 1) == 0
    return jnp.where(even, pltpu.roll(x, w - 1, axis=1), pltpu.roll(x, 1, axis=1))


def _mla_prep_kernel(*refs, rope):
    zb_ref, gq_ref, gkv_ref, wq_ref, wk_ref, wv_ref, pk_ref = refs[:7]
    if rope:
        cq_ref, sq_ref, ck_ref, sk_ref = refs[7:11]
        refs = refs[11:]
    else:
        refs = refs[7:]
    q_o, k_o, v_o, ckv_o, kr_o = refs
    zb = zb_ref[...]
    q = _mm(_rms(zb[:, :Q_LORA], gq_ref[...]), wq_ref[...])
    ckv = _rms(zb[:, Q_LORA:Q_LORA + KV_LORA], gkv_ref[...])
    kr = zb[:, Q_LORA + KV_LORA:]
    if rope:
        cq = jnp.concatenate([cq_ref[...]] * H_B, axis=1)
        sq = jnp.concatenate([sq_ref[...]] * H_B, axis=1)
        q = q * cq + _swap_pairs(q) * sq
        kr = kr * ck_ref[...] + _swap_pairs(kr) * sk_ref[...]
    q_o[...] = q.astype(BF16)
    k_o[...] = (_mm(ckv, wk_ref[...]) + _mm(kr, pk_ref[...])).astype(BF16)
    v_o[...] = _mm(ckv, wv_ref[...]).astype(BF16)
    ckv_o[...] = ckv
    kr_o[...] = kr


def _mla_prep(z, mp, rope_tabs, seq, tm=1024):
    n = z.shape[0]
    full = lambda a: pl.BlockSpec(a.shape, lambda i: (0,) * a.ndim)
    params = (mp["q_norm"], mp["kv_norm"], mp["wq"], mp["wk"], mp["wv"], mp["pk"])
    in_specs = [pl.BlockSpec((tm, BRANCH_W), lambda i: (i, COL_ZB))] + [full(p) for p in params]
    args = [z, *params]
    rope = rope_tabs is not None
    if rope:
        per = seq // tm
        in_specs += [pl.BlockSpec((tm, LANES), lambda i: (i % per, 0))] * 4
        args += list(rope_tabs)
    rb = lambda w: pl.BlockSpec((tm, w), lambda i: (i, 0))
    return pl.pallas_call(
        functools.partial(_mla_prep_kernel, rope=rope),
        grid=(n // tm,),
        in_specs=in_specs,
        out_specs=[rb(H_B * LANES), rb(H_B * LANES), rb(H_B * VDIM), rb(KV_LORA), rb(LANES)],
        out_shape=[jax.ShapeDtypeStruct((n, H_B * LANES), BF16),
                   jax.ShapeDtypeStruct((n, H_B * LANES), BF16),
                   jax.ShapeDtypeStruct((n, H_B * VDIM), BF16),
                   jax.ShapeDtypeStruct((n, KV_LORA), F32),
                   jax.ShapeDtypeStruct((n, LANES), F32)],
        compiler_params=_cparams(("arbitrary",)),
        name="mla_prep",
    )(*args)


def _kv_expand_kernel(ckv_ref, kr_ref, wk_ref, wv_ref, pk_ref, k_o, v_o):
    ckv = ckv_ref[...]
    k_o[...] = (_mm(ckv, wk_ref[...]) + _mm(kr_ref[...], pk_ref[...])).astype(BF16)
    v_o[...] = _mm(ckv, wv_ref[...]).astype(BF16)


def _kv_expand(ckv, kr128, mp):
    m = ckv.shape[0]
    full = lambda a: pl.BlockSpec(a.shape, lambda i: (0,) * a.ndim)
    args = (ckv, kr128, mp["wk"], mp["wv"], mp["pk"])
    return pl.pallas_call(
        _kv_expand_kernel,
        grid=(1,),
        in_specs=[full(a) for a in args],
        out_specs=[pl.BlockSpec((m, H_B * LANES), lambda i: (0, 0)),
                   pl.BlockSpec((m, H_B * VDIM), lambda i: (0, 0))],
        out_shape=[jax.ShapeDtypeStruct((m, H_B * LANES), BF16),
                   jax.ShapeDtypeStruct((m, H_B * VDIM), BF16)],
        compiler_params=_cparams(("arbitrary",)),
        name="mla_ctx_kv",
    )(*args)


def _attn_kernel(*refs, has_ctx):
    if has_ctx:
        q_ref, k_ref, v_ref, kc_ref, vc_ref, g_ref, o_ref = refs
    else:
        q_ref, k_ref, v_ref, g_ref, o_ref = refs
    scale = (NOPE + ROPE) ** -0.5
    tq = q_ref.shape[0]
    lane_a = lax.broadcasted_iota(jnp.int32, (tq, LANES), 1) < VDIM
    outs = []
    for p in range(H_B // 2):
        vs = v_ref[:, p * LANES:(p + 1) * LANES]
        pair = []
        for h in (2 * p, 2 * p + 1):
            hs = slice(h * LANES, (h + 1) * LANES)
            qh = q_ref[:, hs]
            s = lax.dot_general(qh, k_ref[:, hs], (((1,), (1,)), ((), ())),
                                preferred_element_type=F32) * scale
            m = jnp.max(s, axis=-1, keepdims=True)
            if has_ctx:
                sc = lax.dot_general(qh, kc_ref[:, hs], (((1,), (1,)), ((), ())),
                                     preferred_element_type=F32) * scale
                m = jnp.maximum(m, jnp.max(sc, axis=-1, keepdims=True))
            e = jnp.exp(s - m)
            den = jnp.sum(e, axis=-1, keepdims=True)
            o = jnp.dot(e.astype(BF16), vs, preferred_element_type=F32)
            if has_ctx:
                ec = jnp.exp(sc - m)
                den = den + jnp.sum(ec, axis=-1, keepdims=True)
                o = o + jnp.dot(ec.astype(BF16), vc_ref[:, p * LANES:(p + 1) * LANES],
                                preferred_element_type=F32)
            pair.append(o / den)
        outs.append(jnp.where(lane_a, pair[0], pair[1]))
    o_ref[...] = jnp.concatenate(outs, axis=1) * _silu(g_ref[...])


def _attention(z, q, k, v, kc, vc, nb, seq, tq=256):
    n = z.shape[0]
    nq = seq // tq
    has_ctx = kc is not None
    in_specs = [pl.BlockSpec((tq, H_B * LANES), lambda b, i: (b * nq + i, 0)),
                pl.BlockSpec((seq, H_B * LANES), lambda b, i: (b, 0)),
                pl.BlockSpec((seq, H_B * VDIM), lambda b, i: (b, 0))]
    args = [q, k, v]
    if has_ctx:
        past = kc.shape[0] // nb
        in_specs += [pl.BlockSpec((past, H_B * LANES), lambda b, i: (b, 0)),
                     pl.BlockSpec((past, H_B * VDIM), lambda b, i: (b, 0))]
        args += [kc, vc]
    in_specs.append(pl.BlockSpec((tq, BRANCH_W), lambda b, i: (b * nq + i, COL_GB)))
    args.append(z)
    return pl.pallas_call(
        functools.partial(_attn_kernel, has_ctx=has_ctx),
        grid=(nb, nq),
        in_specs=in_specs,
        out_specs=pl.BlockSpec((tq, BRANCH_W), lambda b, i: (b * nq + i, 0)),
        out_shape=jax.ShapeDtypeStruct((n, BRANCH_W), F32),
        compiler_params=_cparams(("arbitrary", "arbitrary")),
        name="mla_attention",
    )(*args)


def _gmlp_kernel(u_ref, vc_ref, g_ref, lng_ref, lnb_ref, ws_ref, bs_ref, o_ref):
    tm = u_ref.shape[0]
    u = jax.nn.gelu(u_ref[...])
    x = jax.nn.gelu(vc_ref[...])
    mu = jnp.mean(x, axis=-1, keepdims=True)
    xc = x - mu
    var = jnp.mean(xc * xc, axis=-1, keepdims=True)
    vn = (xc * lax.rsqrt(var + 1e-5) * lng_ref[...] + lnb_ref[...]).astype(BF16)
    rows = []
    for c in range(tm // CHUNK):
        cols = []
        for g in range(G_C):
            blk = vn[c * CHUNK:(c + 1) * CHUNK, g * LANES:(g + 1) * LANES]
            cols.append(jnp.dot(ws_ref[g], blk, preferred_element_type=F32))
        rows.append(jnp.concatenate(cols, axis=1) + bs_ref[...])
    mixed = jnp.concatenate(rows, axis=0)
    o_ref[...] = u * mixed * _silu(g_ref[...])


def _gmlp(z, gp, tm=512):
    n = z.shape[0]
    zb = lambda c: pl.BlockSpec((tm, BRANCH_W), lambda i, c=c: (i, c))
    full = lambda a: pl.BlockSpec(a.shape, lambda i: (0,) * a.ndim)
    params = (gp["ln_g"], gp["ln_b"], gp["w_s"], gp["b_s"])
    return pl.pallas_call(
        _gmlp_kernel,
        grid=(n // tm,),
        in_specs=[zb(COL_U), zb(COL_VC), zb(COL_GC)] + [full(p) for p in params],
        out_specs=pl.BlockSpec((tm, BRANCH_W), lambda i: (i, 0)),
        out_shape=jax.ShapeDtypeStruct((n, BRANCH_W), F32),
        compiler_params=_cparams(("arbitrary",)),
        name="gmlp",
    )(z, z, z, *params)


def _fnet_kernel(f_ref, g_ref, cs_ref, dft_ref, o_ref, xcs_scr, *, seq):
    r = pl.program_id(1)

    @pl.when(r == 0)
    def _():
        for g in range(G_D):
            xg = _mm(f_ref[:, g * LANES:(g + 1) * LANES], cs_ref[...])
            xcs_scr[0:seq, g * LANES:(g + 1) * LANES] = xg[:, :LANES].astype(BF16)
            xcs_scr[seq:2 * seq, g * LANES:(g + 1) * LANES] = xg[:, LANES:].astype(BF16)

    y = jnp.dot(dft_ref[...], xcs_scr[...], preferred_element_type=F32)
    o_ref[...] = y * (1.0 / math.sqrt(seq * LANES)) * _silu(g_ref[...])


def _fnet(z, cs128, dft, nb, seq, tr=256):
    n = z.shape[0]
    nr = seq // tr
    return pl.pallas_call(
        functools.partial(_fnet_kernel, seq=seq),
        grid=(nb, nr),
        in_specs=[pl.BlockSpec((seq, BRANCH_W), lambda b, r: (b, COL_F)),
                  pl.BlockSpec((tr, BRANCH_W), lambda b, r: (b * nr + r, COL_GD)),
                  pl.BlockSpec((LANES, 2 * LANES), lambda b, r: (0, 0)),
                  pl.BlockSpec((tr, 2 * seq), lambda b, r: (r, 0))],
        out_specs=pl.BlockSpec((tr, BRANCH_W), lambda b, r: (b * nr + r, 0)),
        out_shape=jax.ShapeDtypeStruct((n, BRANCH_W), F32),
        scratch_shapes=[pltpu.VMEM((2 * seq, BRANCH_W), BF16)],
        compiler_params=_cparams(("arbitrary", "arbitrary")),
        name="fnet",
    )(z, z, cs128, dft)


def _back_kernel(x_ref, ss_ref, g_ref, oa_ref, ob_ref, oc_ref, od_ref, wm_ref, bm_ref,
                 wb_ref, wo_ref, fg_ref, y_ref, h_scr, acc_scr, *, final):
    nb = pl.program_id(1)

    @pl.when(nb == 0)
    def _():
        h = _rms(x_ref[...], g_ref[...]) * (1.0 + ss_ref[0, 1:2, :]) + ss_ref[0, 0:1, :]
        h_scr[...] = h.astype(BF16)
        acc_scr[...] = jnp.zeros_like(acc_scr)

    gates = jax.nn.sigmoid(jnp.dot(h_scr[...], wm_ref[...], preferred_element_type=F32)
                           + bm_ref[...])
    for i, o_ref in enumerate((oa_ref, ob_ref, oc_ref, od_ref)):
        @pl.when(nb == i)
        def _(o_ref=o_ref):
            acc_scr[...] += gates * _mm(o_ref[...], wb_ref[0])

    @pl.when(nb == 3)
    def _():
        xn = x_ref[...] + ss_ref[0, 2:3, :] * _mm(acc_scr[...], wo_ref[...])
        if final:
            xn = _rms(xn, fg_ref[...])
        y_ref[...] = xn


def _back(x2d, ss, norm_g, o_a, o_b, o_c, o_d, lw, final_g, final, tm=512):
    n = x2d.shape[0]
    rows_per_ss = FRONT_TM // tm
    ob = pl.BlockSpec((tm, BRANCH_W), lambda i, j: (i, 0))
    return pl.pallas_call(
        functools.partial(_back_kernel, final=final),
        grid=(n // tm, 4),
        in_specs=[pl.BlockSpec((tm, D_MODEL), lambda i, j: (i, 0)),
                  pl.BlockSpec((1, 3, D_MODEL), lambda i, j: (i // rows_per_ss, 0, 0)),
                  pl.BlockSpec((1, D_MODEL), lambda i, j: (0, 0)),
                  ob, ob, ob, ob,
                  pl.BlockSpec((D_MODEL, D_MODEL), lambda i, j: (0, j)),
                  pl.BlockSpec((1, D_MODEL), lambda i, j: (0, j)),
                  pl.BlockSpec((1, BRANCH_W, D_MODEL), lambda i, j: (j, 0, 0)),
                  pl.BlockSpec((D_MODEL, D_MODEL), lambda i, j: (0, 0)),
                  pl.BlockSpec((1, D_MODEL), lambda i, j: (0, 0))],
        out_specs=pl.BlockSpec((tm, D_MODEL), lambda i, j: (i, 0)),
        out_shape=jax.ShapeDtypeStruct((n, D_MODEL), F32),
        scratch_shapes=[pltpu.VMEM((tm, D_MODEL), BF16), pltpu.VMEM((tm, D_MODEL), F32)],
        compiler_params=_cparams(("arbitrary", "arbitrary")),
        name="back",
    )(x2d, ss, norm_g, o_a, o_b, o_c, o_d, lw["w_merge"], lw["b_merge"], lw["w_branch"],
      lw["w_out"], final_g)


def _pad_cols(w):
    z = lambda k: jnp.zeros(w.shape[:-1] + (k,), w.dtype)
    return jnp.concatenate([w[..., :SHIFT_W], z(Z_SHIFT_PAD - SHIFT_W), w[..., SHIFT_W:2240],
                            w[..., 2240:2656], z(3072 - 2560 - 416), w[..., 2656:]], axis=-1)


def _head_block_ones():
    i = np.arange(BRANCH_W) // N_A
    return jnp.asarray(i[:, None] == i[None, :], BF16)


def _rope_place():
    p = np.zeros((LANES, H_B * LANES), np.float32)
    for h in range(H_B):
        for j in range(ROPE):
            p[j, h * LANES + NOPE + j] = 1.0
    return jnp.asarray(p, BF16)


def _pack_q(w):
    w = w.reshape(Q_LORA, H_B, NOPE + ROPE)
    return jnp.pad(w, ((0, 0), (0, 0), (0, LANES - NOPE - ROPE))).reshape(Q_LORA, H_B * LANES)


def _pack_kv(w):
    w = w.reshape(KV_LORA, H_B, NOPE + VDIM)
    wk = jnp.pad(w[..., :NOPE], ((0, 0), (0, 0), (0, LANES - NOPE))).reshape(KV_LORA, H_B * LANES)
    return wk, w[..., NOPE:].reshape(KV_LORA, H_B * VDIM)


def _rope_tables(n_tokens):
    rows = n_tokens // GRID_W
    row = jnp.repeat(jnp.arange(rows, dtype=F32), GRID_W)
    col = jnp.tile(jnp.arange(GRID_W, dtype=F32), rows)
    n_freq = ROPE // 4
    inv = ROPE_BASE ** (-jnp.arange(n_freq, dtype=F32) / n_freq)
    ang = jnp.concatenate([row[:, None] * inv, col[:, None] * inv], axis=-1)
    cos = jnp.repeat(jnp.cos(ang), 2, axis=-1)
    sin = jnp.repeat(jnp.sin(ang), 2, axis=-1) * jnp.tile(jnp.asarray([-1.0, 1.0], F32), ROPE // 2)
    ones = lambda k: jnp.ones((n_tokens, k), F32)
    zeros = lambda k: jnp.zeros((n_tokens, k), F32)
    cq = jnp.concatenate([ones(NOPE), cos, ones(LANES - NOPE - ROPE)], axis=-1)
    sq = jnp.concatenate([zeros(NOPE), sin, zeros(LANES - NOPE - ROPE)], axis=-1)
    ck = jnp.concatenate([cos, ones(LANES - ROPE)], axis=-1)
    sk = jnp.concatenate([sin, zeros(LANES - ROPE)], axis=-1)
    return cq, sq, ck, sk


def _dft_tables(seq):
    k = np.arange(LANES)
    a = 2.0 * np.pi * ((k[:, None] * k[None, :]) % LANES) / LANES
    cs128 = np.concatenate([np.cos(a), np.sin(a)], axis=1)
    t = np.arange(seq)
    b = 2.0 * np.pi * ((t[:, None] * t[None, :]) % seq) / seq
    dft = np.concatenate([np.cos(b), -np.sin(b)], axis=1)
    return jnp.asarray(cs128, BF16), jnp.asarray(dft, BF16)


def _lora_weights(w_up, a_up):
    z = jnp.zeros((LORA, BRANCH_W), F32)
    pad = jnp.zeros((256 - 3 * LORA, 2 * BRANCH_W), F32)
    wlt = jnp.concatenate([jnp.concatenate([w_up[0], z], 1), jnp.concatenate([z, w_up[1]], 1),
                           jnp.concatenate([z, z], 1), pad], axis=0)
    wla = jnp.concatenate([jnp.concatenate([z, z], 1), jnp.concatenate([z, z], 1),
                           jnp.concatenate([a_up[0], a_up[1]], 1), pad], axis=0)
    return wlt.astype(BF16), wla.astype(BF16)


def _to_blockdiag(s):
    b = s.shape[0]
    s = s.reshape(b, H_A // 2, 2, N_A, N_A)
    z = jnp.zeros_like(s[:, :, 0])
    top = jnp.concatenate([s[:, :, 0], z], axis=-1)
    bot = jnp.concatenate([z, s[:, :, 1]], axis=-1)
    return jnp.concatenate([top, bot], axis=-2)


def _from_blockdiag(s):
    b = s.shape[0]
    return jnp.stack([s[:, :, :N_A, :N_A], s[:, :, N_A:, N_A:]], axis=2).reshape(b, H_A, N_A, N_A)


def _trunk_layer(x2d, ss, lw, nb, seq, rope_tabs, ctx, final_g, final):
    z = _front(x2d, ss[:, :2], lw["norm_g"], lw["w_in"], lw["mu"], seq)
    prep = _rwkv_prep(z, lw["rwkv"])
    s0 = None
    if ctx is not None:
        s0 = jnp.stack([_to_blockdiag(ctx[0]), _to_blockdiag(ctx[1])], axis=1)
    yf, yb, sfin = _rwkv_scan(z, prep, s0, nb, seq)
    o_a = _rwkv_post(z, yf, yb, prep[7], lw["rwkv"])
    q, k, v, ckv, kr = _mla_prep(z, lw["mla"], rope_tabs, seq)
    kc = vc = None
    if ctx is not None:
        past = ctx[2].shape[1]
        kr_c = jnp.pad(ctx[3].reshape(nb * past, ROPE), ((0, 0), (0, LANES - ROPE)))
        kc, vc = _kv_expand(ctx[2].reshape(nb * past, KV_LORA), kr_c, lw["mla"])
    o_b = _attention(z, q, k, v, kc, vc, nb, seq)
    o_c = _gmlp(z, lw["gmlp"])
    o_d = _fnet(z, *lw["dft"][seq], nb, seq)
    x_new = _back(x2d, ss, lw["norm_g"], o_a, o_b, o_c, o_d, lw, final_g, final)
    return x_new, sfin, ckv, kr


def kernel(x_prompt, x_sample, state_rwkv_fwd, state_rwkv_bwd, cache_mla_ckv, cache_mla_krope, c, c_ctx, norm_g, w_ada, b_ada, w_in, shift_mu, rwkv_w0, rwkv_w_up, rwkv_a0, rwkv_a_up, rwkv_k_k, rwkv_k_a, rwkv_r_k, rwkv_ln_g, rwkv_ln_b, mla_q_norm, mla_w_q_up, mla_kv_norm, mla_w_kv_up, gmlp_ln_g, gmlp_ln_b, gmlp_w_s, gmlp_b_s, w_branch, w_merge, b_merge, w_out, final_norm_g):
    nb_c, seq_c, _ = x_prompt.shape
    nb_l, seq_l, _ = x_sample.shape
    assert (nb_c * seq_c) % FRONT_TM == 0 and FRONT_TM % seq_c == 0 and seq_l == FRONT_TM

    cond8 = jnp.concatenate([c_ctx[None], c, jnp.zeros((8 - 1 - nb_l, D_MODEL), F32)], axis=0)
    mod = _modulation(cond8, w_ada, b_ada).reshape(DEPTH, 8, 3, D_MODEL)
    n_ctx_tiles = nb_c * seq_c // FRONT_TM

    bd = _head_block_ones()
    pk = _rope_place()
    rope_tabs = _rope_tables(seq_l)
    dft = {s: _dft_tables(s) for s in {seq_c, seq_l}}
    w_in_p = _pad_cols(w_in).astype(BF16)
    mu_p = _pad_cols(jnp.pad(shift_mu, ((0, 0), (0, w_in.shape[-1] - SHIFT_W))))[:, None, :1792]
    final_g = final_norm_g[None]

    xc = x_prompt.reshape(nb_c * seq_c, D_MODEL)
    xl = x_sample.reshape(nb_l * seq_l, D_MODEL)
    sf_list, sb_list, ckv_list, kr_list = [], [], [], []
    for l in range(DEPTH):
        wlt, wla = _lora_weights(rwkv_w_up[l], rwkv_a_up[l])
        wk, wv = _pack_kv(mla_w_kv_up[l])
        lw = {
            "norm_g": norm_g[l][None], "w_in": w_in_p[l], "mu": mu_p[l],
            "rwkv": {
                "wlt": wlt, "wla": wla, "w0": rwkv_w0[l].reshape(1, 2 * BRANCH_W),
                "a0": rwkv_a0[l].reshape(1, 2 * BRANCH_W), "k_k": rwkv_k_k[l][None],
                "k_a": rwkv_k_a[l][None], "r_k": rwkv_r_k[l].reshape(1, BRANCH_W),
                "ln_g": rwkv_ln_g[l][None], "ln_b": rwkv_ln_b[l][None], "bd": bd,
            },
            "mla": {
                "q_norm": mla_q_norm[l][None], "kv_norm": mla_kv_norm[l][None],
                "wq": _pack_q(mla_w_q_up[l]).astype(BF16), "wk": wk.astype(BF16),
                "wv": wv.astype(BF16), "pk": pk,
            },
            "gmlp": {
                "ln_g": gmlp_ln_g[l][None], "ln_b": gmlp_ln_b[l][None],
                "w_s": gmlp_w_s[l].astype(BF16),
                "b_s": jnp.repeat(gmlp_b_s[l].T, BRANCH_W // G_C, axis=1),
            },
            "dft": dft,
            "w_merge": w_merge[l].astype(BF16), "b_merge": b_merge[l][None],
            "w_branch": w_branch[l].astype(BF16), "w_out": w_out[l].astype(BF16),
        }
        final = l == DEPTH - 1
        ss_c = jnp.broadcast_to(mod[l, 0][None], (n_ctx_tiles, 3, D_MODEL))
        ss_l = mod[l, 1:1 + nb_l]
        xc, sfin, ckv, kr = _trunk_layer(xc, ss_c, lw, nb_c, seq_c, None, None, final_g, final)
        sf_list.append(_from_blockdiag(sfin[:, 0]))
        sb_list.append(_from_blockdiag(sfin[:, 1]))
        ckv_list.append(ckv.reshape(nb_c, seq_c, KV_LORA))
        kr_list.append(kr[:, :ROPE].reshape(nb_c, seq_c, ROPE))
        ctx = (state_rwkv_fwd[:, l], state_rwkv_bwd[:, l], cache_mla_ckv[:, l], cache_mla_krope[:, l])
        xl = _trunk_layer(xl, ss_l, lw, nb_l, seq_l, rope_tabs, ctx, final_g, final)[0]

    return (xc.reshape(nb_c, seq_c, D_MODEL), xl.reshape(nb_l, seq_l, D_MODEL),
            jnp.stack(sf_list, axis=1), jnp.stack(sb_list, axis=1),
            jnp.stack(ckv_list, axis=1), jnp.stack(kr_list, axis=1))
```

```python
import functools
import math

import jax
import jax.numpy as jnp
import numpy as np
from jax import lax
from jax.experimental import pallas as pl
from jax.experimental.pallas import tpu as pltpu

F32 = jnp.float32
BF16 = jnp.bfloat16

D_MODEL = 1024
DEPTH = 2
GRID_W = 64
BRANCH_W = 512
H_A = 8
N_A = 64
LORA = 64
RWKV_GN_EPS = 64e-5
H_B = 8
NOPE = 64
ROPE = 32
VDIM = 64
Q_LORA = 256
KV_LORA = 128
ROPE_BASE = 10000.0
G_C = 4
CHUNK = 128
G_D = 4
NORM_EPS = 1e-6
SHIFT_W = 1728

LANES = 128
VMEM_LIMIT = 52 * 1024 * 1024

ZW = 6144
Z_SHIFT_PAD = 2048
COL_R, COL_K, COL_V = 0, 1, 2
COL_LORA = 6
COL_GA, COL_ZB, COL_GB, COL_U, COL_VC, COL_GC, COL_F, COL_GD = 4, 5, 6, 7, 8, 9, 10, 11
FRONT_TM = 2048
FRONT_TN = 256
N_SHIFT_TILES = 1792 // FRONT_TN
SCAN_C = 64


def _mm(a, b):
    return jnp.dot(a.astype(BF16), b.astype(BF16), preferred_element_type=F32)


def _mm_nt(a, b):
    return lax.dot_general(a.astype(BF16), b.astype(BF16), (((1,), (1,)), ((), ())),
                           preferred_element_type=F32)


def _split3(x):
    hi = x.astype(BF16)
    r1 = x - hi.astype(F32)
    mid = r1.astype(BF16)
    lo = (r1 - mid.astype(F32)).astype(BF16)
    return hi, mid, lo


def _mm_exact_rhs01(x, m01):
    hi, mid, lo = _split3(x)
    d = functools.partial(jnp.dot, preferred_element_type=F32)
    return d(hi, m01) + d(mid, m01) + d(lo, m01)


def _mm_exact_lhs01(m01, x):
    hi, mid, lo = _split3(x)
    d = functools.partial(jnp.dot, preferred_element_type=F32)
    return d(m01, hi) + d(m01, mid) + d(m01, lo)


def _silu(x):
    return x * jax.nn.sigmoid(x)


def _rms(x, g):
    return x * lax.rsqrt(jnp.mean(x * x, axis=-1, keepdims=True) + NORM_EPS) * g


def _cparams(sem, vmem=VMEM_LIMIT):
    return pltpu.CompilerParams(dimension_semantics=sem, vmem_limit_bytes=vmem)


def _mod_kernel(c_ref, w_ref, b_ref, o_ref):
    o_ref[0] = _mm(_silu(c_ref[...]), w_ref[0]) + b_ref[0]


def _modulation(cond8, w_ada, b_ada):
    return pl.pallas_call(
        _mod_kernel,
        grid=(DEPTH, 3),
        in_specs=[
            pl.BlockSpec((8, D_MODEL), lambda l, j: (0, 0)),
            pl.BlockSpec((1, D_MODEL, D_MODEL), lambda l, j: (l, 0, j)),
            pl.BlockSpec((1, 1, D_MODEL), lambda l, j: (l, 0, j)),
        ],
        out_specs=pl.BlockSpec((1, 8, D_MODEL), lambda l, j: (l, 0, j)),
        out_shape=jax.ShapeDtypeStruct((DEPTH, 8, 3 * D_MODEL), F32),
        compiler_params=_cparams(("arbitrary", "arbitrary")),
        name="modulation",
    )(cond8, w_ada, b_ada.reshape(DEPTH, 1, 3 * D_MODEL))


def _front_kernel(x_ref, ss_ref, g_ref, w_ref, mu_ref, z_ref, h_scr, *, seq):
    j = pl.program_id(1)

    @pl.when(j == 0)
    def _():
        h = _rms(x_ref[...], g_ref[...]) * (1.0 + ss_ref[0, 1:2, :]) + ss_ref[0, 0:1, :]
        h_scr[...] = h.astype(BF16)

    z = jnp.dot(h_scr[...], w_ref[...], preferred_element_type=F32)

    @pl.when(j < N_SHIFT_TILES)
    def _():
        t = lax.broadcasted_iota(jnp.int32, z.shape, 0) & (seq - 1)
        prev = jnp.where(t == 0, 0.0, pltpu.roll(z, 1, axis=0))
        nxt = jnp.where(t == seq - 1, 0.0, pltpu.roll(z, FRONT_TM - 1, axis=0))
        z_ref[...] = z + (0.5 * (prev + nxt) - z) * mu_ref[...]

    @pl.when(j >= N_SHIFT_TILES)
    def _():
        z_ref[...] = z


def _front(x2d, ss, norm_g, w_in_p, mu_p, seq):
    n = x2d.shape[0]
    return pl.pallas_call(
        functools.partial(_front_kernel, seq=seq),
        grid=(n // FRONT_TM, ZW // FRONT_TN),
        in_specs=[
            pl.BlockSpec((FRONT_TM, D_MODEL), lambda i, j: (i, 0)),
            pl.BlockSpec((1, 2, D_MODEL), lambda i, j: (i, 0, 0)),
            pl.BlockSpec((1, D_MODEL), lambda i, j: (0, 0)),
            pl.BlockSpec((D_MODEL, FRONT_TN), lambda i, j: (0, j)),
            pl.BlockSpec((1, FRONT_TN), lambda i, j: (0, jnp.minimum(j, N_SHIFT_TILES - 1))),
        ],
        out_specs=pl.BlockSpec((FRONT_TM, FRONT_TN), lambda i, j: (i, j)),
        out_shape=jax.ShapeDtypeStruct((n, ZW), F32),
        scratch_shapes=[pltpu.VMEM((FRONT_TM, D_MODEL), BF16)],
        compiler_params=_cparams(("arbitrary", "arbitrary")),
        name="front",
    )(x2d, ss, norm_g, w_in_p, mu_p)


def _rwkv_prep_kernel(r_ref, k_ref, v_ref, lora_ref, wlt_ref, wla_ref, w0_ref, a0_ref,
                      kk_ref_, ka_ref, rk_ref, bd_ref,
                      kk_o, lwf_o, lwb_o, ktf_o, ktb_o, bf_o, bb_o, bonus_o):
    lora = lora_ref[...]
    pre = w0_ref[...] + _mm(jnp.tanh(lora), wlt_ref[...])
    lw = -jnp.exp(-jax.nn.softplus(-pre) - 0.5)
    a = jax.nn.sigmoid(a0_ref[...] + _mm(lora, wla_ref[...]))
    k = k_ref[...]
    bd = bd_ref[...]
    kkraw = k * kk_ref_[...]
    nrm = jnp.sqrt(_mm_exact_rhs01(kkraw * kkraw, bd))
    kk = kkraw / jnp.maximum(nrm, 1e-12)
    af = a[:, :BRANCH_W]
    ab = a[:, BRANCH_W:]
    ka = ka_ref[...]
    ktf = k * (1.0 + (af - 1.0) * ka)
    ktb = k * (1.0 + (ab - 1.0) * ka)
    rr = r_ref[...] * rk_ref[...]
    bonus = (_mm_exact_rhs01(rr * ktf, bd) + _mm_exact_rhs01(rr * ktb, bd)) * v_ref[...]
    kk_o[...] = kk
    lwf_o[...] = lw[:, :BRANCH_W]
    lwb_o[...] = lw[:, BRANCH_W:]
    ktf_o[...] = ktf
    ktb_o[...] = ktb
    bf_o[...] = kk * af
    bb_o[...] = kk * ab
    bonus_o[...] = bonus


def _rwkv_prep(z, wp, tm=256):
    n = z.shape[0]
    zb = lambda c: pl.BlockSpec((tm, BRANCH_W), lambda i, c=c: (i, c))
    full = lambda a: pl.BlockSpec(a.shape, lambda i: (0,) * a.ndim)
    params = (wp["wlt"], wp["wla"], wp["w0"], wp["a0"], wp["k_k"], wp["k_a"], wp["r_k"], wp["bd"])
    ob = pl.BlockSpec((tm, BRANCH_W), lambda i: (i, 0))
    return pl.pallas_call(
        _rwkv_prep_kernel,
        grid=(n // tm,),
        in_specs=[zb(COL_R), zb(COL_K), zb(COL_V),
                  pl.BlockSpec((tm, 256), lambda i: (i, COL_LORA))] + [full(p) for p in params],
        out_specs=[ob] * 8,
        out_shape=[jax.ShapeDtypeStruct((n, BRANCH_W), F32)] * 8,
        compiler_params=_cparams(("arbitrary",)),
        name="rwkv_prep",
    )(z, z, z, z, *params)


def _scan_chunks(dirs, s_ref):
    c = SCAN_C
    n2 = 2 * c
    ri = lax.broadcasted_iota(jnp.int32, (c, c), 0)
    ci = lax.broadcasted_iota(jnp.int32, (c, c), 1)
    row = lax.broadcasted_iota(jnp.int32, (n2, n2), 0)
    col = lax.broadcasted_iota(jnp.int32, (n2, n2), 1)
    same64 = (row >> 6) == (col >> 6)
    same16 = (row >> 4) == (col >> 4)
    tl = row & (c - 1)
    il = col & (c - 1)
    eye = (row == col).astype(F32)
    lane_a = lax.broadcasted_iota(jnp.int32, (c, LANES), 1) < N_A

    def stack2(x):
        return jnp.concatenate([jnp.where(lane_a, x, 0.0), jnp.where(lane_a, 0.0, x)], axis=0)

    def dup2(x):
        return jnp.concatenate([x, x], axis=0)

    lhs, rhs, v2, bk, gtot, strict, incl, key = [], [], [], [], [], [], [], []
    for d, (r, kk, v, lw, kt, bv, rev) in enumerate(dirs):
        tri = ((ri <= ci) if rev else (ri >= ci)).astype(BF16)
        cs = _mm_exact_lhs01(tri, lw)
        tot = cs[0:1] if rev else cs[c - 1:c]
        g_tot = jnp.exp(tot)
        a_t = -kk * jnp.exp(cs - lw)
        r_t = r * jnp.exp(cs)
        g_inv = jnp.exp(-cs)
        b_t = bv * g_inv
        k_t = kt * g_inv
        g_rem = jnp.exp(tot - cs)
        b_h = bv * g_rem
        k_h = kt * g_rem
        st = same64 & ((il > tl) if rev else (il < tl))
        inc = same64 & ((il >= tl) if rev else (il <= tl))
        for p in range(H_A // 2):
            sl = slice(p * LANES, (p + 1) * LANES)
            lhs.append(jnp.concatenate([stack2(a_t[:, sl]), stack2(r_t[:, sl])], axis=0))
            rhs.append(jnp.concatenate([dup2(b_t[:, sl]), dup2(k_t[:, sl])], axis=0))
            v2.append(stack2(v[:, sl]))
            bk.append(jnp.concatenate([stack2(b_h[:, sl]), stack2(k_h[:, sl])], axis=0))
            gtot.append(g_tot[:, sl])
            strict.append(st)
            incl.append(inc)
            key.append((d, p))
    ch = range(len(key))

    s = [s_ref[key[i][0], key[i][1]] for i in ch]
    big = [_mm_nt(lhs[i], jnp.concatenate([rhs[i], s[i]], axis=0)) for i in ch]
    lab = [jnp.where(strict[i], big[i][:n2, :n2], 0.0) for i in ch]
    lak = [jnp.where(strict[i], big[i][:n2, n2:2 * n2], 0.0) for i in ch]
    mrbk = [jnp.concatenate([jnp.where(incl[i], big[i][n2:, :n2], 0.0),
                             jnp.where(incl[i], big[i][n2:, n2:2 * n2], 0.0)], axis=1) for i in ch]
    dg = [jnp.where(same16, lab[i], 0.0) for i in ch]
    off = [lab[i] - dg[i] for i in ch]
    pw = dg
    pinv = [eye + dg[i] for i in ch]
    for _ in range(3):
        pw = [_mm(pw[i], pw[i]) for i in ch]
        pinv = [pinv[i] + _mm(pinv[i], pw[i]) for i in ch]
    f = [_mm(pinv[i], off[i]) for i in ch]
    f2 = [_mm(f[i], f[i]) for i in ch]
    q = [eye + f[i] + f2[i] + _mm(f[i], f2[i]) for i in ch]
    tinv = [_mm(q[i], pinv[i]) for i in ch]
    lv = [_mm(lak[i], v2[i]) for i in ch]
    u2 = [_mm(tinv[i], big[i][:n2, 2 * n2:] + lv[i]) for i in ch]
    uv = [jnp.concatenate([u2[i], v2[i]], axis=0) for i in ch]
    y2 = [big[i][n2:, 2 * n2:] + _mm(mrbk[i], uv[i]) for i in ch]
    for i in ch:
        s_ref[key[i][0], key[i][1]] = s[i] * gtot[i] + _mm(uv[i].T, bk[i])
    npair = H_A // 2
    return [jnp.concatenate([y2[d * npair + p][:c] + y2[d * npair + p][c:] for p in range(npair)], axis=1)
            for d in range(len(dirs))]


def _scan_kernel(*refs, has_init):
    if has_init:
        s0_ref, refs = refs[0], refs[1:]
    fwd, bwd = refs[0:6], refs[6:12]
    yf_ref, yb_ref, sfin_ref, s_scr = refs[12:16]
    c = pl.program_id(1)

    @pl.when(c == 0)
    def _():
        if has_init:
            s_scr[...] = s0_ref[0]
        else:
            s_scr[...] = jnp.zeros_like(s_scr)

    yf, yb = _scan_chunks([tuple(x[...] for x in fwd) + (False,),
                           tuple(x[...] for x in bwd) + (True,)], s_scr)
    yf_ref[...] = yf
    yb_ref[...] = yb

    @pl.when(c == pl.num_programs(1) - 1)
    def _():
        sfin_ref[0] = s_scr[...]


def _rwkv_scan(z, prep, s0_bd, nb, seq):
    n = z.shape[0]
    nc = seq // SCAN_C
    kk, lwf, lwb, ktf, ktb, bf, bb, _ = prep
    fi = lambda b, c: b * nc + c
    bi = lambda b, c: b * nc + (nc - 1 - c)
    spec = lambda idx, col=0: pl.BlockSpec((SCAN_C, BRANCH_W), lambda b, c: (idx(b, c), col))
    in_specs = [spec(fi, COL_R), spec(fi), spec(fi, COL_V), spec(fi), spec(fi), spec(fi),
                spec(bi, COL_R), spec(bi), spec(bi, COL_V), spec(bi), spec(bi), spec(bi)]
    args = [z, kk, z, lwf, ktf, bf, z, kk, z, lwb, ktb, bb]
    sspec = pl.BlockSpec((1, 2, H_A // 2, LANES, LANES), lambda b, c: (b, 0, 0, 0, 0))
    has_init = s0_bd is not None
    if has_init:
        in_specs = [sspec] + in_specs
        args = [s0_bd] + args
    return pl.pallas_call(
        functools.partial(_scan_kernel, has_init=has_init),
        grid=(nb, nc),
        in_specs=in_specs,
        out_specs=[spec(fi), spec(bi), sspec],
        out_shape=[jax.ShapeDtypeStruct((n, BRANCH_W), F32),
                   jax.ShapeDtypeStruct((n, BRANCH_W), F32),
                   jax.ShapeDtypeStruct((nb, 2, H_A // 2, LANES, LANES), F32)],
        scratch_shapes=[pltpu.VMEM((2, H_A // 2, LANES, LANES), F32)],
        compiler_params=_cparams(("arbitrary", "arbitrary")),
        name="rwkv_scan",
    )(*args)


def _rwkv_post_kernel(yf_ref, yb_ref, bonus_ref, g_ref, lng_ref, lnb_ref, bd_ref, o_ref):
    bd = bd_ref[...]
    o = yf_ref[...] + yb_ref[...]
    mu = _mm_exact_rhs01(o, bd) * (1.0 / N_A)
    dlt = o - mu
    var = _mm_exact_rhs01(dlt * dlt, bd) * (1.0 / N_A)
    y = dlt * lax.rsqrt(var + RWKV_GN_EPS) * lng_ref[...] + lnb_ref[...] + bonus_ref[...]
    o_ref[...] = y * _silu(g_ref[...])


def _rwkv_post(z, yf, yb, bonus, wp, tm=1024):
    n = z.shape[0]
    rb = pl.BlockSpec((tm, BRANCH_W), lambda i: (i, 0))
    full = lambda a: pl.BlockSpec(a.shape, lambda i: (0,) * a.ndim)
    return pl.pallas_call(
        _rwkv_post_kernel,
        grid=(n // tm,),
        in_specs=[rb, rb, rb, pl.BlockSpec((tm, BRANCH_W), lambda i: (i, COL_GA)),
                  full(wp["ln_g"]), full(wp["ln_b"]), full(wp["bd"])],
        out_specs=rb,
        out_shape=jax.ShapeDtypeStruct((n, BRANCH_W), F32),
        compiler_params=_cparams(("arbitrary",)),
        name="rwkv_post",
    )(yf, yb, bonus, z, wp["ln_g"], wp["ln_b"], wp["bd"])


def _swap_pairs(x):
    w = x.shape[1]
    even = (lax.broadcasted_iota(jnp.int32, x.shape, 1) & 1) == 0
    return jnp.where(even, pltpu.roll(x, w - 1, axis=1), pltpu.roll(x, 1, axis=1))


def _mla_prep_kernel(*refs, rope):
    zb_ref, gq_ref, gkv_ref, wq_ref, wk_ref, wv_ref, pk_ref = refs[:7]
    if rope:
        cq_ref, sq_ref, ck_ref, sk_ref = refs[7:11]
        refs = refs[11:]
    else:
        refs = refs[7:]
    q_o, k_o, v_o, ckv_o, kr_o = refs
    zb = zb_ref[...]
    q = _mm(_rms(zb[:, :Q_LORA], gq_ref[...]), wq_ref[...])
    ckv = _rms(zb[:, Q_LORA:Q_LORA + KV_LORA], gkv_ref[...])
    kr = zb[:, Q_LORA + KV_LORA:]
    if rope:
        cq = jnp.concatenate([cq_ref[...]] * H_B, axis=1)
        sq = jnp.concatenate([sq_ref[...]] * H_B, axis=1)
        q = q * cq + _swap_pairs(q) * sq
        kr = kr * ck_ref[...] + _swap_pairs(kr) * sk_ref[...]
    q_o[...] = q.astype(BF16)
    k_o[...] = (_mm(ckv, wk_ref[...]) + _mm(kr, pk_ref[...])).astype(BF16)
    v_o[...] = _mm(ckv, wv_ref[...]).astype(BF16)
    ckv_o[...] = ckv
    kr_o[...] = kr


def _mla_prep(z, mp, rope_tabs, seq, tm=1024):
    n = z.shape[0]
    full = lambda a: pl.BlockSpec(a.shape, lambda i: (0,) * a.ndim)
    params = (mp["q_norm"], mp["kv_norm"], mp["wq"], mp["wk"], mp["wv"], mp["pk"])
    in_specs = [pl.BlockSpec((tm, BRANCH_W), lambda i: (i, COL_ZB))] + [full(p) for p in params]
    args = [z, *params]
    rope = rope_tabs is not None
    if rope:
        per = seq // tm
        in_specs += [pl.BlockSpec((tm, LANES), lambda i: (i % per, 0))] * 4
        args += list(rope_tabs)
    rb = lambda w: pl.BlockSpec((tm, w), lambda i: (i, 0))
    return pl.pallas_call(
        functools.partial(_mla_prep_kernel, rope=rope),
        grid=(n // tm,),
        in_specs=in_specs,
        out_specs=[rb(H_B * LANES), rb(H_B * LANES), rb(H_B * VDIM), rb(KV_LORA), rb(LANES)],
        out_shape=[jax.ShapeDtypeStruct((n, H_B * LANES), BF16),
                   jax.ShapeDtypeStruct((n, H_B * LANES), BF16),
                   jax.ShapeDtypeStruct((n, H_B * VDIM), BF16),
                   jax.ShapeDtypeStruct((n, KV_LORA), F32),
                   jax.ShapeDtypeStruct((n, LANES), F32)],
        compiler_params=_cparams(("arbitrary",)),
        name="mla_prep",
    )(*args)


def _kv_expand_kernel(ckv_ref, kr_ref, wk_ref, wv_ref, pk_ref, k_o, v_o):
    ckv = ckv_ref[...]
    k_o[...] = (_mm(ckv, wk_ref[...]) + _mm(kr_ref[...], pk_ref[...])).astype(BF16)
    v_o[...] = _mm(ckv, wv_ref[...]).astype(BF16)


def _kv_expand(ckv, kr128, mp):
    m = ckv.shape[0]
    full = lambda a: pl.BlockSpec(a.shape, lambda i: (0,) * a.ndim)
    args = (ckv, kr128, mp["wk"], mp["wv"], mp["pk"])
    return pl.pallas_call(
        _kv_expand_kernel,
        grid=(1,),
        in_specs=[full(a) for a in args],
        out_specs=[pl.BlockSpec((m, H_B * LANES), lambda i: (0, 0)),
                   pl.BlockSpec((m, H_B * VDIM), lambda i: (0, 0))],
        out_shape=[jax.ShapeDtypeStruct((m, H_B * LANES), BF16),
                   jax.ShapeDtypeStruct((m, H_B * VDIM), BF16)],
        compiler_params=_cparams(("arbitrary",)),
        name="mla_ctx_kv",
    )(*args)


def _attn_kernel(*refs, has_ctx):
    if has_ctx:
        q_ref, k_ref, v_ref, kc_ref, vc_ref, g_ref, o_ref = refs
    else:
        q_ref, k_ref, v_ref, g_ref, o_ref = refs
    scale = (NOPE + ROPE) ** -0.5
    tq = q_ref.shape[0]
    lane_a = lax.broadcasted_iota(jnp.int32, (tq, LANES), 1) < VDIM
    outs = []
    for p in range(H_B // 2):
        vs = v_ref[:, p * LANES:(p + 1) * LANES]
        pair = []
        for h in (2 * p, 2 * p + 1):
            hs = slice(h * LANES, (h + 1) * LANES)
            qh = q_ref[:, hs]
            s = lax.dot_general(qh, k_ref[:, hs], (((1,), (1,)), ((), ())),
                                preferred_element_type=F32) * scale
            m = jnp.max(s, axis=-1, keepdims=True)
            if has_ctx:
                sc = lax.dot_general(qh, kc_ref[:, hs], (((1,), (1,)), ((), ())),
                                     preferred_element_type=F32) * scale
                m = jnp.maximum(m, jnp.max(sc, axis=-1, keepdims=True))
            e = jnp.exp(s - m)
            den = jnp.sum(e, axis=-1, keepdims=True)
            o = jnp.dot(e.astype(BF16), vs, preferred_element_type=F32)
            if has_ctx:
                ec = jnp.exp(sc - m)
                den = den + jnp.sum(ec, axis=-1, keepdims=True)
                o = o + jnp.dot(ec.astype(BF16), vc_ref[:, p * LANES:(p + 1) * LANES],
                                preferred_element_type=F32)
            pair.append(o / den)
        outs.append(jnp.where(lane_a, pair[0], pair[1]))
    o_ref[...] = jnp.concatenate(outs, axis=1) * _silu(g_ref[...])


def _attention(z, q, k, v, kc, vc, nb, seq, tq=256):
    n = z.shape[0]
    nq = seq // tq
    has_ctx = kc is not None
    in_specs = [pl.BlockSpec((tq, H_B * LANES), lambda b, i: (b * nq + i, 0)),
                pl.BlockSpec((seq, H_B * LANES), lambda b, i: (b, 0)),
                pl.BlockSpec((seq, H_B * VDIM), lambda b, i: (b, 0))]
    args = [q, k, v]
    if has_ctx:
        past = kc.shape[0] // nb
        in_specs += [pl.BlockSpec((past, H_B * LANES), lambda b, i: (b, 0)),
                     pl.BlockSpec((past, H_B * VDIM), lambda b, i: (b, 0))]
        args += [kc, vc]
    in_specs.append(pl.BlockSpec((tq, BRANCH_W), lambda b, i: (b * nq + i, COL_GB)))
    args.append(z)
    return pl.pallas_call(
        functools.partial(_attn_kernel, has_ctx=has_ctx),
        grid=(nb, nq),
        in_specs=in_specs,
        out_specs=pl.BlockSpec((tq, BRANCH_W), lambda b, i: (b * nq + i, 0)),
        out_shape=jax.ShapeDtypeStruct((n, BRANCH_W), F32),
        compiler_params=_cparams(("arbitrary", "arbitrary")),
        name="mla_attention",
    )(*args)


def _gmlp_kernel(u_ref, vc_ref, g_ref, lng_ref, lnb_ref, ws_ref, bs_ref, o_ref):
    tm = u_ref.shape[0]
    u = jax.nn.gelu(u_ref[...])
    x = jax.nn.gelu(vc_ref[...])
    mu = jnp.mean(x, axis=-1, keepdims=True)
    xc = x - mu
    var = jnp.mean(xc * xc, axis=-1, keepdims=True)
    vn = (xc * lax.rsqrt(var + 1e-5) * lng_ref[...] + lnb_ref[...]).astype(BF16)
    rows = []
    for c in range(tm // CHUNK):
        cols = []
        for g in range(G_C):
            blk = vn[c * CHUNK:(c + 1) * CHUNK, g * LANES:(g + 1) * LANES]
            cols.append(jnp.dot(ws_ref[g], blk, preferred_element_type=F32))
        rows.append(jnp.concatenate(cols, axis=1) + bs_ref[...])
    mixed = jnp.concatenate(rows, axis=0)
    o_ref[...] = u * mixed * _silu(g_ref[...])


def _gmlp(z, gp, tm=512):
    n = z.shape[0]
    zb = lambda c: pl.BlockSpec((tm, BRANCH_W), lambda i, c=c: (i, c))
    full = lambda a: pl.BlockSpec(a.shape, lambda i: (0,) * a.ndim)
    params = (gp["ln_g"], gp["ln_b"], gp["w_s"], gp["b_s"])
    return pl.pallas_call(
        _gmlp_kernel,
        grid=(n // tm,),
        in_specs=[zb(COL_U), zb(COL_VC), zb(COL_GC)] + [full(p) for p in params],
        out_specs=pl.BlockSpec((tm, BRANCH_W), lambda i: (i, 0)),
        out_shape=jax.ShapeDtypeStruct((n, BRANCH_W), F32),
        compiler_params=_cparams(("arbitrary",)),
        name="gmlp",
    )(z, z, z, *params)


def _fnet_kernel(f_ref, g_ref, cs_ref, dft_ref, o_ref, xcs_scr, *, seq):
    r = pl.program_id(1)

    @pl.when(r == 0)
    def _():
        for g in range(G_D):
            xg = _mm(f_ref[:, g * LANES:(g + 1) * LANES], cs_ref[...])
            xcs_scr[0:seq, g * LANES:(g + 1) * LANES] = xg[:, :LANES].astype(BF16)
            xcs_scr[seq:2 * seq, g * LANES:(g + 1) * LANES] = xg[:, LANES:].astype(BF16)

    y = jnp.dot(dft_ref[...], xcs_scr[...], preferred_element_type=F32)
    o_ref[...] = y * (1.0 / math.sqrt(seq * LANES)) * _silu(g_ref[...])


def _fnet(z, cs128, dft, nb, seq, tr=256):
    n = z.shape[0]
    nr = seq // tr
    return pl.pallas_call(
        functools.partial(_fnet_kernel, seq=seq),
        grid=(nb, nr),
        in_specs=[pl.BlockSpec((seq, BRANCH_W), lambda b, r: (b, COL_F)),
                  pl.BlockSpec((tr, BRANCH_W), lambda b, r: (b * nr + r, COL_GD)),
                  pl.BlockSpec((LANES, 2 * LANES), lambda b, r: (0, 0)),
                  pl.BlockSpec((tr, 2 * seq), lambda b, r: (r, 0))],
        out_specs=pl.BlockSpec((tr, BRANCH_W), lambda b, r: (b * nr + r, 0)),
        out_shape=jax.ShapeDtypeStruct((n, BRANCH_W), F32),
        scratch_shapes=[pltpu.VMEM((2 * seq, BRANCH_W), BF16)],
        compiler_params=_cparams(("arbitrary", "arbitrary")),
        name="fnet",
    )(z, z, cs128, dft)


def _back_kernel(x_ref, ss_ref, g_ref, oa_ref, ob_ref, oc_ref, od_ref, wm_ref, bm_ref,
                 wb_ref, wo_ref, fg_ref, y_ref, h_scr, acc_scr, *, final):
    nb = pl.program_id(1)

    @pl.when(nb == 0)
    def _():
        h = _rms(x_ref[...], g_ref[...]) * (1.0 + ss_ref[0, 1:2, :]) + ss_ref[0, 0:1, :]
        h_scr[...] = h.astype(BF16)
        acc_scr[...] = jnp.zeros_like(acc_scr)

    gates = jax.nn.sigmoid(jnp.dot(h_scr[...], wm_ref[...], preferred_element_type=F32)
                           + bm_ref[...])
    for i, o_ref in enumerate((oa_ref, ob_ref, oc_ref, od_ref)):
        @pl.when(nb == i)
        def _(o_ref=o_ref):
            acc_scr[...] += gates * _mm(o_ref[...], wb_ref[0])

    @pl.when(nb == 3)
    def _():
        xn = x_ref[...] + ss_ref[0, 2:3, :] * _mm(acc_scr[...], wo_ref[...])
        if final:
            xn = _rms(xn, fg_ref[...])
        y_ref[...] = xn


def _back(x2d, ss, norm_g, o_a, o_b, o_c, o_d, lw, final_g, final, tm=512):
    n = x2d.shape[0]
    rows_per_ss = FRONT_TM // tm
    ob = pl.BlockSpec((tm, BRANCH_W), lambda i, j: (i, 0))
    return pl.pallas_call(
        functools.partial(_back_kernel, final=final),
        grid=(n // tm, 4),
        in_specs=[pl.BlockSpec((tm, D_MODEL), lambda i, j: (i, 0)),
                  pl.BlockSpec((1, 3, D_MODEL), lambda i, j: (i // rows_per_ss, 0, 0)),
                  pl.BlockSpec((1, D_MODEL), lambda i, j: (0, 0)),
                  ob, ob, ob, ob,
                  pl.BlockSpec((D_MODEL, D_MODEL), lambda i, j: (0, j)),
                  pl.BlockSpec((1, D_MODEL), lambda i, j: (0, j)),
                  pl.BlockSpec((1, BRANCH_W, D_MODEL), lambda i, j: (j, 0, 0)),
                  pl.BlockSpec((D_MODEL, D_MODEL), lambda i, j: (0, 0)),
                  pl.BlockSpec((1, D_MODEL), lambda i, j: (0, 0))],
        out_specs=pl.BlockSpec((tm, D_MODEL), lambda i, j: (i, 0)),
        out_shape=jax.ShapeDtypeStruct((n, D_MODEL), F32),
        scratch_shapes=[pltpu.VMEM((tm, D_MODEL), BF16), pltpu.VMEM((tm, D_MODEL), F32)],
        compiler_params=_cparams(("arbitrary", "arbitrary")),
        name="back",
    )(x2d, ss, norm_g, o_a, o_b, o_c, o_d, lw["w_merge"], lw["b_merge"], lw["w_branch"],
      lw["w_out"], final_g)


def _pad_cols(w):
    z = lambda k: jnp.zeros(w.shape[:-1] + (k,), w.dtype)
    return jnp.concatenate([w[..., :SHIFT_W], z(Z_SHIFT_PAD - SHIFT_W), w[..., SHIFT_W:2240],
                            w[..., 2240:2656], z(3072 - 2560 - 416), w[..., 2656:]], axis=-1)


def _head_block_ones():
    i = np.arange(BRANCH_W) // N_A
    return jnp.asarray(i[:, None] == i[None, :], BF16)


def _rope_place():
    p = np.zeros((LANES, H_B * LANES), np.float32)
    for h in range(H_B):
        for j in range(ROPE):
            p[j, h * LANES + NOPE + j] = 1.0
    return jnp.asarray(p, BF16)


def _pack_q(w):
    w = w.reshape(Q_LORA, H_B, NOPE + ROPE)
    return jnp.pad(w, ((0, 0), (0, 0), (0, LANES - NOPE - ROPE))).reshape(Q_LORA, H_B * LANES)


def _pack_kv(w):
    w = w.reshape(KV_LORA, H_B, NOPE + VDIM)
    wk = jnp.pad(w[..., :NOPE], ((0, 0), (0, 0), (0, LANES - NOPE))).reshape(KV_LORA, H_B * LANES)
    return wk, w[..., NOPE:].reshape(KV_LORA, H_B * VDIM)


def _rope_tables(n_tokens):
    rows = n_tokens // GRID_W
    row = jnp.repeat(jnp.arange(rows, dtype=F32), GRID_W)
    col = jnp.tile(jnp.arange(GRID_W, dtype=F32), rows)
    n_freq = ROPE // 4
    inv = ROPE_BASE ** (-jnp.arange(n_freq, dtype=F32) / n_freq)
    ang = jnp.concatenate([row[:, None] * inv, col[:, None] * inv], axis=-1)
    cos = jnp.repeat(jnp.cos(ang), 2, axis=-1)
    sin = jnp.repeat(jnp.sin(ang), 2, axis=-1) * jnp.tile(jnp.asarray([-1.0, 1.0], F32), ROPE // 2)
    ones = lambda k: jnp.ones((n_tokens, k), F32)
    zeros = lambda k: jnp.zeros((n_tokens, k), F32)
    cq = jnp.concatenate([ones(NOPE), cos, ones(LANES - NOPE - ROPE)], axis=-1)
    sq = jnp.concatenate([zeros(NOPE), sin, zeros(LANES - NOPE - ROPE)], axis=-1)
    ck = jnp.concatenate([cos, ones(LANES - ROPE)], axis=-1)
    sk = jnp.concatenate([sin, zeros(LANES - ROPE)], axis=-1)
    return cq, sq, ck, sk


def _dft_tables(seq):
    k = np.arange(LANES)
    a = 2.0 * np.pi * ((k[:, None] * k[None, :]) % LANES) / LANES
    cs128 = np.concatenate([np.cos(a), np.sin(a)], axis=1)
    t = np.arange(seq)
    b = 2.0 * np.pi * ((t[:, None] * t[None, :]) % seq) / seq
    dft = np.concatenate([np.cos(b), -np.sin(b)], axis=1)
    return (jnp.asarray(cs128.astype(np.float32)).astype(BF16),
            jnp.asarray(dft.astype(np.float32)).astype(BF16))


def _lora_weights(w_up, a_up):
    z = jnp.zeros((LORA, BRANCH_W), F32)
    pad = jnp.zeros((256 - 3 * LORA, 2 * BRANCH_W), F32)
    wlt = jnp.concatenate([jnp.concatenate([w_up[0], z], 1), jnp.concatenate([z, w_up[1]], 1),
                           jnp.concatenate([z, z], 1), pad], axis=0)
    wla = jnp.concatenate([jnp.concatenate([z, z], 1), jnp.concatenate([z, z], 1),
                           jnp.concatenate([a_up[0], a_up[1]], 1), pad], axis=0)
    return wlt.astype(BF16), wla.astype(BF16)


def _to_blockdiag(s):
    b = s.shape[0]
    s = s.reshape(b, H_A // 2, 2, N_A, N_A)
    z = jnp.zeros_like(s[:, :, 0])
    top = jnp.concatenate([s[:, :, 0], z], axis=-1)
    bot = jnp.concatenate([z, s[:, :, 1]], axis=-1)
    return jnp.concatenate([top, bot], axis=-2)


def _from_blockdiag(s):
    b = s.shape[0]
    return jnp.stack([s[:, :, :N_A, :N_A], s[:, :, N_A:, N_A:]], axis=2).reshape(b, H_A, N_A, N_A)


def _trunk_layer(x2d, ss, lw, nb, seq, rope_tabs, ctx, final_g, final):
    z = _front(x2d, ss[:, :2], lw["norm_g"], lw["w_in"], lw["mu"], seq)
    prep = _rwkv_prep(z, lw["rwkv"])
    s0 = None
    if ctx is not None:
        s0 = jnp.stack([_to_blockdiag(ctx[0]), _to_blockdiag(ctx[1])], axis=1)
    yf, yb, sfin = _rwkv_scan(z, prep, s0, nb, seq)
    o_a = _rwkv_post(z, yf, yb, prep[7], lw["rwkv"])
    q, k, v, ckv, kr = _mla_prep(z, lw["mla"], rope_tabs, seq)
    kc = vc = None
    if ctx is not None:
        past = ctx[2].shape[1]
        kr_c = jnp.pad(ctx[3].reshape(nb * past, ROPE), ((0, 0), (0, LANES - ROPE)))
        kc, vc = _kv_expand(ctx[2].reshape(nb * past, KV_LORA), kr_c, lw["mla"])
    o_b = _attention(z, q, k, v, kc, vc, nb, seq)
    o_c = _gmlp(z, lw["gmlp"])
    o_d = _fnet(z, *lw["dft"][seq], nb, seq)
    x_new = _back(x2d, ss, lw["norm_g"], o_a, o_b, o_c, o_d, lw, final_g, final)
    return x_new, sfin, ckv, kr


def kernel(x_prompt, x_sample, state_rwkv_fwd, state_rwkv_bwd, cache_mla_ckv, cache_mla_krope, c, c_ctx, norm_g, w_ada, b_ada, w_in, shift_mu, rwkv_w0, rwkv_w_up, rwkv_a0, rwkv_a_up, rwkv_k_k, rwkv_k_a, rwkv_r_k, rwkv_ln_g, rwkv_ln_b, mla_q_norm, mla_w_q_up, mla_kv_norm, mla_w_kv_up, gmlp_ln_g, gmlp_ln_b, gmlp_w_s, gmlp_b_s, w_branch, w_merge, b_merge, w_out, final_norm_g):
    nb_c, seq_c, _ = x_prompt.shape
    nb_l, seq_l, _ = x_sample.shape
    assert (nb_c * seq_c) % FRONT_TM == 0 and FRONT_TM % seq_c == 0 and seq_l == FRONT_TM

    cond8 = jnp.concatenate([c_ctx[None], c, jnp.zeros((8 - 1 - nb_l, D_MODEL), F32)], axis=0)
    mod = _modulation(cond8, w_ada, b_ada).reshape(DEPTH, 8, 3, D_MODEL)
    n_ctx_tiles = nb_c * seq_c // FRONT_TM

    bd = _head_block_ones()
    pk = _rope_place()
    rope_tabs = _rope_tables(seq_l)
    dft = {s: _dft_tables(s) for s in {seq_c, seq_l}}
    w_in_p = _pad_cols(w_in).astype(BF16)
    mu_p = _pad_cols(jnp.pad(shift_mu, ((0, 0), (0, w_in.shape[-1] - SHIFT_W))))[:, None, :1792]
    final_g = final_norm_g[None]

    xc = x_prompt.reshape(nb_c * seq_c, D_MODEL)
    xl = x_sample.reshape(nb_l * seq_l, D_MODEL)
    sf_list, sb_list, ckv_list, kr_list = [], [], [], []
    for l in range(DEPTH):
        wlt, wla = _lora_weights(rwkv_w_up[l], rwkv_a_up[l])
        wk, wv = _pack_kv(mla_w_kv_up[l])
        lw = {
            "norm_g": norm_g[l][None], "w_in": w_in_p[l], "mu": mu_p[l],
            "rwkv": {
                "wlt": wlt, "wla": wla, "w0": rwkv_w0[l].reshape(1, 2 * BRANCH_W),
                "a0": rwkv_a0[l].reshape(1, 2 * BRANCH_W), "k_k": rwkv_k_k[l][None],
                "k_a": rwkv_k_a[l][None], "r_k": rwkv_r_k[l].reshape(1, BRANCH_W),
                "ln_g": rwkv_ln_g[l][None], "ln_b": rwkv_ln_b[l][None], "bd": bd,
            },
            "mla": {
                "q_norm": mla_q_norm[l][None], "kv_norm": mla_kv_norm[l][None],
                "wq": _pack_q(mla_w_q_up[l]).astype(BF16), "wk": wk.astype(BF16),
                "wv": wv.astype(BF16), "pk": pk,
            },
            "gmlp": {
                "ln_g": gmlp_ln_g[l][None], "ln_b": gmlp_ln_b[l][None],
                "w_s": gmlp_w_s[l].astype(BF16),
                "b_s": jnp.repeat(gmlp_b_s[l].T, BRANCH_W // G_C, axis=1),
            },
            "dft": dft,
            "w_merge": w_merge[l].astype(BF16), "b_merge": b_merge[l][None],
            "w_branch": w_branch[l].astype(BF16), "w_out": w_out[l].astype(BF16),
        }
        final = l == DEPTH - 1
        ss_c = jnp.broadcast_to(mod[l, 0][None], (n_ctx_tiles, 3, D_MODEL))
        ss_l = mod[l, 1:1 + nb_l]
        xc, sfin, ckv, kr = _trunk_layer(xc, ss_c, lw, nb_c, seq_c, None, None, final_g, final)
        sf_list.append(_from_blockdiag(sfin[:, 0]))
        sb_list.append(_from_blockdiag(sfin[:, 1]))
        ckv_list.append(ckv.reshape(nb_c, seq_c, KV_LORA))
        kr_list.append(kr[:, :ROPE].reshape(nb_c, seq_c, ROPE))
        ctx = (state_rwkv_fwd[:, l], state_rwkv_bwd[:, l], cache_mla_ckv[:, l], cache_mla_krope[:, l])
        xl = _trunk_layer(xl, ss_l, lw, nb_l, seq_l, rope_tabs, ctx, final_g, final)[0]

    return (xc.reshape(nb_c, seq_c, D_MODEL), xl.reshape(nb_l, seq_l, D_MODEL),
            jnp.stack(sf_list, axis=1), jnp.stack(sb_list, axis=1),
            jnp.stack(ckv_list, axis=1), jnp.stack(kr_list, axis=1))
```

```python
import functools
import math

import jax
import jax.numpy as jnp
import numpy as np
from jax import lax
from jax.experimental import pallas as pl
from jax.experimental.pallas import tpu as pltpu

F32 = jnp.float32
BF16 = jnp.bfloat16

D_MODEL = 1024
DEPTH = 2
GRID_W = 64
BRANCH_W = 512
H_A = 8
N_A = 64
LORA = 64
RWKV_GN_EPS = 64e-5
H_B = 8
NOPE = 64
ROPE = 32
VDIM = 64
Q_LORA = 256
KV_LORA = 128
ROPE_BASE = 10000.0
Q_PRESCALE = (NOPE + ROPE) ** -0.5 * math.log2(math.e)
G_C = 4
CHUNK = 128
G_D = 4
NORM_EPS = 1e-6
SHIFT_W = 1728

LANES = 128
VMEM_LIMIT = 52 * 1024 * 1024

ZW = 6144
Z_SHIFT_PAD = 2048
COL_R, COL_K, COL_V = 0, 1, 2
COL_LORA = 6
COL_GA, COL_ZB, COL_GB, COL_U, COL_VC, COL_GC, COL_F, COL_GD = 4, 5, 6, 7, 8, 9, 10, 11
FRONT_TM = 2048
FRONT_TN = 512
N_SHIFT_TILES = Z_SHIFT_PAD // FRONT_TN
SCAN_C = 64
SCAN_NB = 2


def _mm(a, b):
    return jnp.dot(a.astype(BF16), b.astype(BF16), preferred_element_type=F32)


def _mm_nt(a, b):
    return lax.dot_general(a.astype(BF16), b.astype(BF16), (((1,), (1,)), ((), ())),
                           preferred_element_type=F32)


def _split3(x):
    hi = x.astype(BF16)
    r1 = x - hi.astype(F32)
    mid = r1.astype(BF16)
    lo = (r1 - mid.astype(F32)).astype(BF16)
    return hi, mid, lo


def _mm_exact_lhs01(m01, x):
    hi, mid, lo = _split3(x)
    d = functools.partial(jnp.dot, preferred_element_type=F32)
    return d(m01, hi) + d(m01, mid) + d(m01, lo)


def _silu(x):
    return x * jax.nn.sigmoid(x)


def _rms(x, g):
    return x * lax.rsqrt(jnp.mean(x * x, axis=-1, keepdims=True) + NORM_EPS) * g


def _cparams(sem, vmem=VMEM_LIMIT):
    return pltpu.CompilerParams(dimension_semantics=sem, vmem_limit_bytes=vmem)


def _mod_kernel(c_ref, w_ref, b_ref, o_ref):
    o_ref[0] = _mm(_silu(c_ref[...]), w_ref[0]) + b_ref[0]


def _modulation(cond8, w_ada, b_ada):
    return pl.pallas_call(
        _mod_kernel,
        grid=(DEPTH, 3),
        in_specs=[
            pl.BlockSpec((8, D_MODEL), lambda l, j: (0, 0)),
            pl.BlockSpec((1, D_MODEL, D_MODEL), lambda l, j: (l, 0, j)),
            pl.BlockSpec((1, 1, D_MODEL), lambda l, j: (l, 0, j)),
        ],
        out_specs=pl.BlockSpec((1, 8, D_MODEL), lambda l, j: (l, 0, j)),
        out_shape=jax.ShapeDtypeStruct((DEPTH, 8, 3 * D_MODEL), F32),
        compiler_params=_cparams(("arbitrary", "arbitrary")),
        name="modulation",
    )(cond8, w_ada, b_ada.reshape(DEPTH, 1, 3 * D_MODEL))


def _front_kernel(x_ref, ss_ref, g_ref, w_ref, mu_ref, z_ref, h_scr, *, seq):
    j = pl.program_id(1)

    @pl.when(j == 0)
    def _():
        h = _rms(x_ref[...], g_ref[...]) * (1.0 + ss_ref[0, 1:2, :]) + ss_ref[0, 0:1, :]
        h_scr[...] = h.astype(BF16)

    z = jnp.dot(h_scr[...], w_ref[...], preferred_element_type=F32)

    @pl.when(j < N_SHIFT_TILES)
    def _():
        t = lax.broadcasted_iota(jnp.int32, z.shape, 0) & (seq - 1)
        prev = jnp.where(t == 0, 0.0, pltpu.roll(z, 1, axis=0))
        nxt = jnp.where(t == seq - 1, 0.0, pltpu.roll(z, FRONT_TM - 1, axis=0))
        z_ref[...] = z + (0.5 * (prev + nxt) - z) * mu_ref[...]

    @pl.when(j >= N_SHIFT_TILES)
    def _():
        z_ref[...] = z


def _front(x2d, ss, norm_g, w_in_p, mu_p, seq):
    n = x2d.shape[0]
    return pl.pallas_call(
        functools.partial(_front_kernel, seq=seq),
        grid=(n // FRONT_TM, ZW // FRONT_TN),
        in_specs=[
            pl.BlockSpec((FRONT_TM, D_MODEL), lambda i, j: (i, 0)),
            pl.BlockSpec((1, 2, D_MODEL), lambda i, j: (i, 0, 0)),
            pl.BlockSpec((1, D_MODEL), lambda i, j: (0, 0)),
            pl.BlockSpec((D_MODEL, FRONT_TN), lambda i, j: (0, j)),
            pl.BlockSpec((1, FRONT_TN), lambda i, j: (0, jnp.minimum(j, N_SHIFT_TILES - 1))),
        ],
        out_specs=pl.BlockSpec((FRONT_TM, FRONT_TN), lambda i, j: (i, j)),
        out_shape=jax.ShapeDtypeStruct((n, ZW), F32),
        scratch_shapes=[pltpu.VMEM((FRONT_TM, D_MODEL), BF16)],
        compiler_params=_cparams(("arbitrary", "arbitrary")),
        name="front",
    )(x2d, ss, norm_g, w_in_p, mu_p)


def _scan_chunks(dirs, s_ref):
    c = SCAN_C
    n2 = 2 * c
    ri = lax.broadcasted_iota(jnp.int32, (c, c), 0)
    ci = lax.broadcasted_iota(jnp.int32, (c, c), 1)
    row = lax.broadcasted_iota(jnp.int32, (n2, n2), 0)
    col = lax.broadcasted_iota(jnp.int32, (n2, n2), 1)
    same64 = (row >> 6) == (col >> 6)
    same16 = (row >> 4) == (col >> 4)
    tl = row & (c - 1)
    il = col & (c - 1)
    eye = (row == col).astype(F32)
    rows_a = row < c
    lane_a = lax.broadcasted_iota(jnp.int32, (c, LANES), 1) < N_A

    def stack2(x):
        return jnp.concatenate([jnp.where(lane_a, x, 0.0), jnp.where(lane_a, 0.0, x)], axis=0)

    lhs, rhs, v2, bk, gtot, strict, incl, key = [], [], [], [], [], [], [], []
    for r, kk, v, lw, kt, bv, rev, j, d in dirs:
        tri = ((ri <= ci) if rev else (ri >= ci)).astype(BF16)
        cs = _mm_exact_lhs01(tri, lw)
        tot = cs[0:1] if rev else cs[c - 1:c]
        g_tot = jnp.exp(tot)
        a_t = -kk * jnp.exp(cs - lw)
        r_t = r * jnp.exp(cs)
        g_inv = jnp.exp(-cs)
        b_t = bv * g_inv
        k_t = kt * g_inv
        g_rem = jnp.exp(tot - cs)
        b_h = bv * g_rem
        k_h = kt * g_rem
        st = same64 & ((il > tl) if rev else (il < tl))
        inc = same64 & ((il >= tl) if rev else (il <= tl))
        for p in range(H_A // 2):
            sl = slice(p * LANES, (p + 1) * LANES)
            lhs.append(jnp.concatenate([stack2(a_t[:, sl]), stack2(r_t[:, sl])], axis=0))
            rhs.append(jnp.concatenate([b_t[:, sl], k_t[:, sl]], axis=0))
            v2.append(stack2(v[:, sl]))
            bk.append(jnp.concatenate([stack2(b_h[:, sl]), stack2(k_h[:, sl])], axis=0))
            gtot.append(g_tot[:, sl])
            strict.append(st)
            incl.append(inc)
            key.append((j, d, p))
    ch = range(len(key))

    s = [s_ref[key[i]] for i in ch]
    big = [_mm_nt(lhs[i], jnp.concatenate([rhs[i], s[i]], axis=0)) for i in ch]
    lab, lak, mrbk = [], [], []
    for i in ch:
        x = big[i][:n2, :n2]
        xr = pltpu.roll(x, c, axis=1)
        y = big[i][n2:, :n2]
        yr = pltpu.roll(y, c, axis=1)
        lab.append(jnp.where(strict[i], jnp.where(rows_a, x, xr), 0.0))
        lak.append(jnp.where(strict[i], jnp.where(rows_a, xr, x), 0.0))
        mrbk.append(jnp.concatenate([jnp.where(incl[i], jnp.where(rows_a, y, yr), 0.0),
                                     jnp.where(incl[i], jnp.where(rows_a, yr, y), 0.0)], axis=1))
    dg = [jnp.where(same16, lab[i], 0.0) for i in ch]
    off = [lab[i] - dg[i] for i in ch]
    pinv = [eye + dg[i] for i in ch]
    pw = [_mm(dg[i], dg[i]) for i in ch]
    for _ in range(2):
        t = [_mm(pw[i], jnp.concatenate([pw[i], pinv[i]], axis=1)) for i in ch]
        pinv = [pinv[i] + t[i][:, n2:] for i in ch]
        pw = [t[i][:, :n2] for i in ch]
    pinv = [pinv[i] + _mm(pw[i], pinv[i]) for i in ch]
    f = [_mm(pinv[i], off[i]) for i in ch]
    t = [_mm(f[i], jnp.concatenate([f[i], pinv[i]], axis=1)) for i in ch]
    g = [pinv[i] + t[i][:, n2:] for i in ch]
    tinv = [g[i] + _mm(t[i][:, :n2], g[i]) for i in ch]
    lv = [_mm(lak[i], v2[i]) for i in ch]
    u2 = [_mm(tinv[i], big[i][:n2, n2:] + lv[i]) for i in ch]
    uv = [jnp.concatenate([u2[i], v2[i]], axis=0) for i in ch]
    y2 = [big[i][n2:, n2:] + _mm(mrbk[i], uv[i]) for i in ch]
    for i in ch:
        s_ref[key[i]] = s[i] * gtot[i] + _mm(uv[i].T, bk[i])
    npair = H_A // 2
    return [jnp.concatenate([y2[g * npair + p][:c] + y2[g * npair + p][c:] for p in range(npair)], axis=1)
            for g in range(len(dirs))]


def _head_sums(x):
    lane_a = lax.broadcasted_iota(jnp.int32, (x.shape[0], LANES), 1) < N_A
    outs = []
    for p in range(H_A // 2):
        xs = x[:, p * LANES:(p + 1) * LANES]
        sa = jnp.sum(jnp.where(lane_a, xs, 0.0), axis=-1, keepdims=True)
        sb = jnp.sum(jnp.where(lane_a, 0.0, xs), axis=-1, keepdims=True)
        outs.append(jnp.where(lane_a, sa, sb))
    return jnp.concatenate(outs, axis=1)


def _rwkv_prepare(k, lora, d, w):
    wlt_ref, wla_ref, w0_ref, a0_ref, kk_ref, ka_ref, rk_ref = w
    hs = slice(d * BRANCH_W, (d + 1) * BRANCH_W)
    pre = w0_ref[:, hs] + _mm(jnp.tanh(lora), wlt_ref[:, hs])
    lw = -jnp.exp(-jax.nn.softplus(-pre) - 0.5)
    a2 = jax.nn.sigmoid(a0_ref[...] + _mm(lora, wla_ref[...]))
    kkraw = k * kk_ref[...]
    kk = kkraw / jnp.maximum(jnp.sqrt(_head_sums(kkraw * kkraw)), 1e-12)
    k_a = ka_ref[...]
    ktf = k * (1.0 + (a2[:, :BRANCH_W] - 1.0) * k_a)
    ktb = k * (1.0 + (a2[:, BRANCH_W:] - 1.0) * k_a)
    return kk, lw, (ktb if d else ktf), kk * a2[:, hs], (ktf + ktb) * rk_ref[...]


def _rwkv_kernel(*refs, has_init):
    if has_init:
        s0_ref, refs = refs[0], refs[1:]
    cur = (refs[0:5], refs[5:10])
    nxt = (refs[10:12], refs[12:14])
    w = refs[14:21]
    lng_ref, lnb_ref = refs[21:23]
    o_ref, sf_ref, sb_ref, s_scr, ysum_scr, p_scr = refs[23:29]
    c = pl.program_id(1)
    nc = pl.num_programs(1)

    @pl.when(c == 0)
    def _():
        if has_init:
            s_scr[...] = s0_ref[...]
        else:
            s_scr[...] = jnp.zeros_like(s_scr)
        for j in range(SCAN_NB):
            for d in range(2):
                for i, a in enumerate(_rwkv_prepare(cur[d][3][j], cur[d][4][j], d, w)):
                    p_scr[j, d, i] = a

    dirs = []
    for j in range(SCAN_NB):
        for d in range(2):
            kk, lw, kt, bv = [p_scr[j, d, i] for i in range(4)]
            dirs.append((cur[d][0][j], kk, cur[d][1][j], lw, kt, bv, d == 1, j, d))
    kts = [[p_scr[j, d, 4] for d in range(2)] for j in range(SCAN_NB)]
    ys = _scan_chunks(dirs, s_scr)

    for j in range(SCAN_NB):
        for d in range(2):
            for i, a in enumerate(_rwkv_prepare(nxt[d][0][j], nxt[d][1][j], d, w)):
                p_scr[j, d, i] = a

    rows = (pl.multiple_of(c * SCAN_C, SCAN_C), pl.multiple_of((nc - 1 - c) * SCAN_C, SCAN_C))

    @pl.when(c < nc // 2)
    def _():
        for j in range(SCAN_NB):
            for d in range(2):
                ysum_scr[j, pl.ds(rows[d], SCAN_C), :] = ys[2 * j + d]

    @pl.when(c >= nc // 2)
    def _():
        for j in range(SCAN_NB):
            for d in range(2):
                r, v, g = cur[d][0][j], cur[d][1][j], cur[d][2][j]
                bonus = _head_sums(r * kts[j][d]) * v
                o = ysum_scr[j, pl.ds(rows[d], SCAN_C), :] + ys[2 * j + d]
                dlt = o - _head_sums(o) * (1.0 / N_A)
                var = _head_sums(dlt * dlt) * (1.0 / N_A)
                y = dlt * lax.rsqrt(var + RWKV_GN_EPS) * lng_ref[...] + lnb_ref[...] + bonus
                o_ref[j, pl.ds(rows[d], SCAN_C), :] = (y * _silu(g)).astype(BF16)

    @pl.when(c == nc - 1)
    def _():
        for j in range(SCAN_NB):
            for d, st_ref in enumerate((sf_ref, sb_ref)):
                for p in range(H_A // 2):
                    s = s_scr[j, d, p]
                    st_ref[j, 2 * p] = s[:N_A, :N_A]
                    st_ref[j, 2 * p + 1] = pltpu.roll(s, N_A, axis=1)[N_A:, :N_A]


def _rwkv(z, wp, s0_bd, nb, seq):
    n = z.shape[0]
    nc = seq // SCAN_C
    assert nc % 2 == 0 and nb % SCAN_NB == 0
    z3 = z.reshape(nb, seq, ZW)
    chunk = ((lambda c: c), (lambda c: nc - 1 - c))
    chunk_next = ((lambda c: jnp.minimum(c + 1, nc - 1)), (lambda c: jnp.maximum(nc - 2 - c, 0)))
    spec = lambda idx, col, w=BRANCH_W: pl.BlockSpec((SCAN_NB, SCAN_C, w), lambda b, c: (b, idx(c), col))
    in_specs = []
    for d in range(2):
        in_specs += [spec(chunk[d], COL_R), spec(chunk[d], COL_V), spec(chunk[d], COL_GA),
                     spec(chunk[d], COL_K), spec(chunk[d], COL_LORA, 256)]
    for d in range(2):
        in_specs += [spec(chunk_next[d], COL_K), spec(chunk_next[d], COL_LORA, 256)]
    full = lambda a: pl.BlockSpec(a.shape, lambda b, c: (0,) * a.ndim)
    params = (wp["wlt"], wp["wla"], wp["w0"], wp["a0"], wp["k_k"], wp["k_a"], wp["r_k"], wp["ln_g"], wp["ln_b"])
    in_specs += [full(p) for p in params]
    args = [z3] * 14 + list(params)
    sspec = pl.BlockSpec((SCAN_NB, 2, H_A // 2, LANES, LANES), lambda b, c: (b, 0, 0, 0, 0))
    has_init = s0_bd is not None
    if has_init:
        in_specs = [sspec] + in_specs
        args = [s0_bd] + args
    fspec = pl.BlockSpec((SCAN_NB, H_A, N_A, N_A), lambda b, c: (b, 0, 0, 0))
    o_a, sf, sb = pl.pallas_call(
        functools.partial(_rwkv_kernel, has_init=has_init),
        grid=(nb // SCAN_NB, nc),
        in_specs=in_specs,
        out_specs=[pl.BlockSpec((SCAN_NB, seq, BRANCH_W), lambda b, c: (b, 0, 0)), fspec, fspec],
        out_shape=[jax.ShapeDtypeStruct((nb, seq, BRANCH_W), BF16),
                   jax.ShapeDtypeStruct((nb, H_A, N_A, N_A), F32),
                   jax.ShapeDtypeStruct((nb, H_A, N_A, N_A), F32)],
        scratch_shapes=[pltpu.VMEM((SCAN_NB, 2, H_A // 2, LANES, LANES), F32),
                        pltpu.VMEM((SCAN_NB, seq, BRANCH_W), F32),
                        pltpu.VMEM((SCAN_NB, 2, 5, SCAN_C, BRANCH_W), F32)],
        compiler_params=_cparams(("arbitrary", "arbitrary")),
        name="rwkv",
    )(*args)
    return o_a.reshape(n, BRANCH_W), sf, sb


def _swap_pairs(x):
    w = x.shape[1]
    even = (lax.broadcasted_iota(jnp.int32, x.shape, 1) & 1) == 0
    return jnp.where(even, pltpu.roll(x, w - 1, axis=1), pltpu.roll(x, 1, axis=1))


def _mla_prep_kernel(*refs, rope):
    zb_ref, gq_ref, gkv_ref, wq_ref, wk_ref, wv_ref, pk_ref = refs[:7]
    if rope:
        cq_ref, sq_ref, ck_ref, sk_ref = refs[7:11]
        refs = refs[11:]
    else:
        refs = refs[7:]
    q_o, k_o, v_o, ckv_o, kr_o = refs
    zb = zb_ref[...]
    q = _mm(_rms(zb[:, :Q_LORA], gq_ref[...]), wq_ref[...])
    ckv = _rms(zb[:, Q_LORA:Q_LORA + KV_LORA], gkv_ref[...])
    kr = zb[:, Q_LORA + KV_LORA:]
    if rope:
        cq = jnp.concatenate([cq_ref[...]] * H_B, axis=1)
        sq = jnp.concatenate([sq_ref[...]] * H_B, axis=1)
        q = q * cq + _swap_pairs(q) * sq
        kr = kr * ck_ref[...] + _swap_pairs(kr) * sk_ref[...]
    q_o[...] = (q * Q_PRESCALE).astype(BF16)
    k_o[...] = (_mm(ckv, wk_ref[...]) + _mm(kr, pk_ref[...])).astype(BF16)
    v_o[...] = _mm(ckv, wv_ref[...]).astype(BF16)
    ckv_o[...] = ckv
    kr_o[...] = kr


def _mla_prep(z, mp, rope_tabs, seq, tm=1024):
    n = z.shape[0]
    full = lambda a: pl.BlockSpec(a.shape, lambda i: (0,) * a.ndim)
    params = (mp["q_norm"], mp["kv_norm"], mp["wq"], mp["wk"], mp["wv"], mp["pk"])
    in_specs = [pl.BlockSpec((tm, BRANCH_W), lambda i: (i, COL_ZB))] + [full(p) for p in params]
    args = [z, *params]
    rope = rope_tabs is not None
    if rope:
        per = seq // tm
        in_specs += [pl.BlockSpec((tm, LANES), lambda i: (i % per, 0))] * 4
        args += list(rope_tabs)
    rb = lambda w: pl.BlockSpec((tm, w), lambda i: (i, 0))
    return pl.pallas_call(
        functools.partial(_mla_prep_kernel, rope=rope),
        grid=(n // tm,),
        in_specs=in_specs,
        out_specs=[rb(H_B * LANES), rb(H_B * LANES), rb(H_B * VDIM), rb(KV_LORA), rb(LANES)],
        out_shape=[jax.ShapeDtypeStruct((n, H_B * LANES), BF16),
                   jax.ShapeDtypeStruct((n, H_B * LANES), BF16),
                   jax.ShapeDtypeStruct((n, H_B * VDIM), BF16),
                   jax.ShapeDtypeStruct((n, KV_LORA), F32),
                   jax.ShapeDtypeStruct((n, LANES), F32)],
        compiler_params=_cparams(("arbitrary",)),
        name="mla_prep",
    )(*args)


def _kv_expand_kernel(ckv_ref, kr_ref, wk_ref, wv_ref, pk_ref, k_o, v_o):
    ckv = ckv_ref[...]
    k_o[...] = (_mm(ckv, wk_ref[...]) + _mm(kr_ref[...], pk_ref[...])).astype(BF16)
    v_o[...] = _mm(ckv, wv_ref[...]).astype(BF16)


def _kv_expand(ckv, kr128, mp):
    m = ckv.shape[0]
    full = lambda a: pl.BlockSpec(a.shape, lambda i: (0,) * a.ndim)
    args = (ckv, kr128, mp["wk"], mp["wv"], mp["pk"])
    return pl.pallas_call(
        _kv_expand_kernel,
        grid=(1,),
        in_specs=[full(a) for a in args],
        out_specs=[pl.BlockSpec((m, H_B * LANES), lambda i: (0, 0)),
                   pl.BlockSpec((m, H_B * VDIM), lambda i: (0, 0))],
        out_shape=[jax.ShapeDtypeStruct((m, H_B * LANES), BF16),
                   jax.ShapeDtypeStruct((m, H_B * VDIM), BF16)],
        compiler_params=_cparams(("arbitrary",)),
        name="mla_ctx_kv",
    )(*args)


def _attn_kernel(*refs, has_ctx):
    if has_ctx:
        q_ref, k_ref, v_ref, kc_ref, vc_ref, g_ref, o_ref = refs
    else:
        q_ref, k_ref, v_ref, g_ref, o_ref = refs
    tq = q_ref.shape[0]
    lane_a = lax.broadcasted_iota(jnp.int32, (tq, LANES), 1) < VDIM
    outs = []
    for p in range(H_B // 2):
        vs = v_ref[:, p * LANES:(p + 1) * LANES]
        pair = []
        for h in (2 * p, 2 * p + 1):
            hs = slice(h * LANES, (h + 1) * LANES)
            qh = q_ref[:, hs]
            s = lax.dot_general(qh, k_ref[:, hs], (((1,), (1,)), ((), ())),
                                preferred_element_type=F32)
            m = jnp.max(s, axis=-1, keepdims=True)
            if has_ctx:
                sc = lax.dot_general(qh, kc_ref[:, hs], (((1,), (1,)), ((), ())),
                                     preferred_element_type=F32)
                m = jnp.maximum(m, jnp.max(sc, axis=-1, keepdims=True))
            e = jnp.exp2(s - m)
            den = jnp.sum(e, axis=-1, keepdims=True)
            o = jnp.dot(e.astype(BF16), vs, preferred_element_type=F32)
            if has_ctx:
                ec = jnp.exp2(sc - m)
                den = den + jnp.sum(ec, axis=-1, keepdims=True)
                o = o + jnp.dot(ec.astype(BF16), vc_ref[:, p * LANES:(p + 1) * LANES],
                                preferred_element_type=F32)
            pair.append(o / den)
        outs.append(jnp.where(lane_a, pair[0], pair[1]))
    o_ref[...] = (jnp.concatenate(outs, axis=1) * _silu(g_ref[...])).astype(BF16)


def _attention(z, q, k, v, kc, vc, nb, seq, tq=256):
    n = z.shape[0]
    nq = seq // tq
    has_ctx = kc is not None
    in_specs = [pl.BlockSpec((tq, H_B * LANES), lambda b, i: (b * nq + i, 0)),
                pl.BlockSpec((seq, H_B * LANES), lambda b, i: (b, 0)),
                pl.BlockSpec((seq, H_B * VDIM), lambda b, i: (b, 0))]
    args = [q, k, v]
    if has_ctx:
        past = kc.shape[0] // nb
        in_specs += [pl.BlockSpec((past, H_B * LANES), lambda b, i: (b, 0)),
                     pl.BlockSpec((past, H_B * VDIM), lambda b, i: (b, 0))]
        args += [kc, vc]
    in_specs.append(pl.BlockSpec((tq, BRANCH_W), lambda b, i: (b * nq + i, COL_GB)))
    args.append(z)
    return pl.pallas_call(
        functools.partial(_attn_kernel, has_ctx=has_ctx),
        grid=(nb, nq),
        in_specs=in_specs,
        out_specs=pl.BlockSpec((tq, BRANCH_W), lambda b, i: (b * nq + i, 0)),
        out_shape=jax.ShapeDtypeStruct((n, BRANCH_W), BF16),
        compiler_params=_cparams(("arbitrary", "arbitrary")),
        name="mla_attention",
    )(*args)


def _gmlp_kernel(u_ref, vc_ref, g_ref, lng_ref, lnb_ref, ws_ref, bs_ref, o_ref):
    tm = u_ref.shape[0]
    u = jax.nn.gelu(u_ref[...])
    x = jax.nn.gelu(vc_ref[...])
    mu = jnp.mean(x, axis=-1, keepdims=True)
    xc = x - mu
    var = jnp.mean(xc * xc, axis=-1, keepdims=True)
    vn = (xc * lax.rsqrt(var + 1e-5) * lng_ref[...] + lnb_ref[...]).astype(BF16)
    rows = []
    for c in range(tm // CHUNK):
        cols = []
        for g in range(G_C):
            blk = vn[c * CHUNK:(c + 1) * CHUNK, g * LANES:(g + 1) * LANES]
            cols.append(jnp.dot(ws_ref[g], blk, preferred_element_type=F32))
        rows.append(jnp.concatenate(cols, axis=1) + bs_ref[...])
    mixed = jnp.concatenate(rows, axis=0)
    o_ref[...] = (u * mixed * _silu(g_ref[...])).astype(BF16)


def _gmlp(z, gp, tm=512):
    n = z.shape[0]
    zb = lambda c: pl.BlockSpec((tm, BRANCH_W), lambda i, c=c: (i, c))
    full = lambda a: pl.BlockSpec(a.shape, lambda i: (0,) * a.ndim)
    params = (gp["ln_g"], gp["ln_b"], gp["w_s"], gp["b_s"])
    return pl.pallas_call(
        _gmlp_kernel,
        grid=(n // tm,),
        in_specs=[zb(COL_U), zb(COL_VC), zb(COL_GC)] + [full(p) for p in params],
        out_specs=pl.BlockSpec((tm, BRANCH_W), lambda i: (i, 0)),
        out_shape=jax.ShapeDtypeStruct((n, BRANCH_W), BF16),
        compiler_params=_cparams(("arbitrary",)),
        name="gmlp",
    )(z, z, z, *params)


def _fnet_kernel(f_ref, g_ref, cs_ref, dft_ref, o_ref, xcs_scr, *, seq):
    r = pl.program_id(1)

    @pl.when(r == 0)
    def _():
        for g in range(G_D):
            xg = _mm(f_ref[:, g * LANES:(g + 1) * LANES], cs_ref[...])
            xcs_scr[0:seq, g * LANES:(g + 1) * LANES] = xg[:, :LANES].astype(BF16)
            xcs_scr[seq:2 * seq, g * LANES:(g + 1) * LANES] = xg[:, LANES:].astype(BF16)

    y = jnp.dot(dft_ref[...], xcs_scr[...], preferred_element_type=F32)
    o_ref[...] = (y * (1.0 / math.sqrt(seq * LANES)) * _silu(g_ref[...])).astype(BF16)


def _fnet(z, cs128, dft, nb, seq, tr=256):
    n = z.shape[0]
    nr = seq // tr
    return pl.pallas_call(
        functools.partial(_fnet_kernel, seq=seq),
        grid=(nb, nr),
        in_specs=[pl.BlockSpec((seq, BRANCH_W), lambda b, r: (b, COL_F)),
                  pl.BlockSpec((tr, BRANCH_W), lambda b, r: (b * nr + r, COL_GD)),
                  pl.BlockSpec((LANES, 2 * LANES), lambda b, r: (0, 0)),
                  pl.BlockSpec((tr, 2 * seq), lambda b, r: (r, 0))],
        out_specs=pl.BlockSpec((tr, BRANCH_W), lambda b, r: (b * nr + r, 0)),
        out_shape=jax.ShapeDtypeStruct((n, BRANCH_W), BF16),
        scratch_shapes=[pltpu.VMEM((2 * seq, BRANCH_W), BF16)],
        compiler_params=_cparams(("arbitrary", "arbitrary")),
        name="fnet",
    )(z, z, cs128, dft)


def _back_kernel(x_ref, ss_ref, g_ref, oa_ref, ob_ref, oc_ref, od_ref, wm_ref, bm_ref,
                 wb_ref, wo_ref, fg_ref, y_ref, h_scr, acc_scr, *, final):
    nb = pl.program_id(1)

    @pl.when(nb == 0)
    def _():
        h = _rms(x_ref[...], g_ref[...]) * (1.0 + ss_ref[0, 1:2, :]) + ss_ref[0, 0:1, :]
        h_scr[...] = h.astype(BF16)
        acc_scr[...] = jnp.zeros_like(acc_scr)

    gates = jax.nn.sigmoid(jnp.dot(h_scr[...], wm_ref[...], preferred_element_type=F32)
                           + bm_ref[...])
    for i, o_ref in enumerate((oa_ref, ob_ref, oc_ref, od_ref)):
        @pl.when(nb == i)
        def _(o_ref=o_ref):
            acc_scr[...] += gates * jnp.dot(o_ref[...], wb_ref[0], preferred_element_type=F32)

    @pl.when(nb == 3)
    def _():
        xn = x_ref[...] + ss_ref[0, 2:3, :] * _mm(acc_scr[...], wo_ref[...])
        if final:
            xn = _rms(xn, fg_ref[...])
        y_ref[...] = xn


def _back(x2d, ss, norm_g, o_a, o_b, o_c, o_d, lw, final_g, final, tm=1024):
    n = x2d.shape[0]
    rows_per_ss = FRONT_TM // tm
    ob = pl.BlockSpec((tm, BRANCH_W), lambda i, j: (i, 0))
    return pl.pallas_call(
        functools.partial(_back_kernel, final=final),
        grid=(n // tm, 4),
        in_specs=[pl.BlockSpec((tm, D_MODEL), lambda i, j: (i, 0)),
                  pl.BlockSpec((1, 3, D_MODEL), lambda i, j: (i // rows_per_ss, 0, 0)),
                  pl.BlockSpec((1, D_MODEL), lambda i, j: (0, 0)),
                  ob, ob, ob, ob,
                  pl.BlockSpec((D_MODEL, D_MODEL), lambda i, j: (0, j)),
                  pl.BlockSpec((1, D_MODEL), lambda i, j: (0, j)),
                  pl.BlockSpec((1, BRANCH_W, D_MODEL), lambda i, j: (j, 0, 0)),
                  pl.BlockSpec((D_MODEL, D_MODEL), lambda i, j: (0, 0)),
                  pl.BlockSpec((1, D_MODEL), lambda i, j: (0, 0))],
        out_specs=pl.BlockSpec((tm, D_MODEL), lambda i, j: (i, 0)),
        out_shape=jax.ShapeDtypeStruct((n, D_MODEL), F32),
        scratch_shapes=[pltpu.VMEM((tm, D_MODEL), BF16), pltpu.VMEM((tm, D_MODEL), F32)],
        compiler_params=_cparams(("arbitrary", "arbitrary")),
        name="back",
    )(x2d, ss, norm_g, o_a, o_b, o_c, o_d, lw["w_merge"], lw["b_merge"], lw["w_branch"],
      lw["w_out"], final_g)


def _pad_cols(w):
    z = lambda k: jnp.zeros(w.shape[:-1] + (k,), w.dtype)
    return jnp.concatenate([w[..., :SHIFT_W], z(Z_SHIFT_PAD - SHIFT_W), w[..., SHIFT_W:2240],
                            w[..., 2240:2656], z(3072 - 2560 - 416), w[..., 2656:]], axis=-1)


def _rope_place():
    p = np.zeros((LANES, H_B * LANES), np.float32)
    for h in range(H_B):
        for j in range(ROPE):
            p[j, h * LANES + NOPE + j] = 1.0
    return jnp.asarray(p, BF16)


def _pack_q(w):
    w = w.reshape(Q_LORA, H_B, NOPE + ROPE)
    return jnp.pad(w, ((0, 0), (0, 0), (0, LANES - NOPE - ROPE))).reshape(Q_LORA, H_B * LANES)


def _pack_kv(w):
    w = w.reshape(KV_LORA, H_B, NOPE + VDIM)
    wk = jnp.pad(w[..., :NOPE], ((0, 0), (0, 0), (0, LANES - NOPE))).reshape(KV_LORA, H_B * LANES)
    return wk, w[..., NOPE:].reshape(KV_LORA, H_B * VDIM)


def _rope_tables(n_tokens):
    rows = n_tokens // GRID_W
    row = jnp.repeat(jnp.arange(rows, dtype=F32), GRID_W)
    col = jnp.tile(jnp.arange(GRID_W, dtype=F32), rows)
    n_freq = ROPE // 4
    inv = ROPE_BASE ** (-jnp.arange(n_freq, dtype=F32) / n_freq)
    ang = jnp.concatenate([row[:, None] * inv, col[:, None] * inv], axis=-1)
    cos = jnp.repeat(jnp.cos(ang), 2, axis=-1)
    sin = jnp.repeat(jnp.sin(ang), 2, axis=-1) * jnp.tile(jnp.asarray([-1.0, 1.0], F32), ROPE // 2)
    ones = lambda k: jnp.ones((n_tokens, k), F32)
    zeros = lambda k: jnp.zeros((n_tokens, k), F32)
    cq = jnp.concatenate([ones(NOPE), cos, ones(LANES - NOPE - ROPE)], axis=-1)
    sq = jnp.concatenate([zeros(NOPE), sin, zeros(LANES - NOPE - ROPE)], axis=-1)
    ck = jnp.concatenate([cos, ones(LANES - ROPE)], axis=-1)
    sk = jnp.concatenate([sin, zeros(LANES - ROPE)], axis=-1)
    return cq, sq, ck, sk


def _dft_tables(seq):
    k = np.arange(LANES)
    a = 2.0 * np.pi * ((k[:, None] * k[None, :]) % LANES) / LANES
    cs128 = np.concatenate([np.cos(a), np.sin(a)], axis=1)
    t = np.arange(seq)
    b = 2.0 * np.pi * ((t[:, None] * t[None, :]) % seq) / seq
    dft = np.concatenate([np.cos(b), -np.sin(b)], axis=1)
    return (jnp.asarray(cs128.astype(np.float32)).astype(BF16),
            jnp.asarray(dft.astype(np.float32)).astype(BF16))


def _lora_weights(w_up, a_up):
    z = jnp.zeros((LORA, BRANCH_W), F32)
    pad = jnp.zeros((256 - 3 * LORA, 2 * BRANCH_W), F32)
    wlt = jnp.concatenate([jnp.concatenate([w_up[0], z], 1), jnp.concatenate([z, w_up[1]], 1),
                           jnp.concatenate([z, z], 1), pad], axis=0)
    wla = jnp.concatenate([jnp.concatenate([z, z], 1), jnp.concatenate([z, z], 1),
                           jnp.concatenate([a_up[0], a_up[1]], 1), pad], axis=0)
    return wlt.astype(BF16), wla.astype(BF16)


def _to_blockdiag(s):
    b = s.shape[0]
    s = s.reshape(b, H_A // 2, 2, N_A, N_A)
    z = jnp.zeros_like(s[:, :, 0])
    top = jnp.concatenate([s[:, :, 0], z], axis=-1)
    bot = jnp.concatenate([z, s[:, :, 1]], axis=-1)
    return jnp.concatenate([top, bot], axis=-2)


def _trunk_layer(x2d, ss, lw, nb, seq, rope_tabs, ctx, final_g, final):
    z = _front(x2d, ss[:, :2], lw["norm_g"], lw["w_in"], lw["mu"], seq)
    s0 = None
    if ctx is not None:
        s0 = jnp.stack([_to_blockdiag(ctx[0]), _to_blockdiag(ctx[1])], axis=1)
    o_a, sf, sb = _rwkv(z, lw["rwkv"], s0, nb, seq)
    q, k, v, ckv, kr = _mla_prep(z, lw["mla"], rope_tabs, seq)
    kc = vc = None
    if ctx is not None:
        past = ctx[2].shape[1]
        kr_c = jnp.pad(ctx[3].reshape(nb * past, ROPE), ((0, 0), (0, LANES - ROPE)))
        kc, vc = _kv_expand(ctx[2].reshape(nb * past, KV_LORA), kr_c, lw["mla"])
    o_b = _attention(z, q, k, v, kc, vc, nb, seq)
    o_c = _gmlp(z, lw["gmlp"])
    o_d = _fnet(z, *lw["dft"][seq], nb, seq)
    x_new = _back(x2d, ss, lw["norm_g"], o_a, o_b, o_c, o_d, lw, final_g, final)
    return x_new, sf, sb, ckv, kr


def kernel(x_prompt, x_sample, state_rwkv_fwd, state_rwkv_bwd, cache_mla_ckv, cache_mla_krope, c, c_ctx, norm_g, w_ada, b_ada, w_in, shift_mu, rwkv_w0, rwkv_w_up, rwkv_a0, rwkv_a_up, rwkv_k_k, rwkv_k_a, rwkv_r_k, rwkv_ln_g, rwkv_ln_b, mla_q_norm, mla_w_q_up, mla_kv_norm, mla_w_kv_up, gmlp_ln_g, gmlp_ln_b, gmlp_w_s, gmlp_b_s, w_branch, w_merge, b_merge, w_out, final_norm_g):
    nb_c, seq_c, _ = x_prompt.shape
    nb_l, seq_l, _ = x_sample.shape
    assert (nb_c * seq_c) % FRONT_TM == 0 and FRONT_TM % seq_c == 0 and seq_l == FRONT_TM

    cond8 = jnp.concatenate([c_ctx[None], c, jnp.zeros((8 - 1 - nb_l, D_MODEL), F32)], axis=0)
    mod = _modulation(cond8, w_ada, b_ada).reshape(DEPTH, 8, 3, D_MODEL)
    n_ctx_tiles = nb_c * seq_c // FRONT_TM

    pk = _rope_place()
    rope_tabs = _rope_tables(seq_l)
    dft = {s: _dft_tables(s) for s in {seq_c, seq_l}}
    w_in_p = _pad_cols(w_in).astype(BF16)
    mu_p = _pad_cols(jnp.pad(shift_mu, ((0, 0), (0, w_in.shape[-1] - SHIFT_W))))[:, None, :Z_SHIFT_PAD]
    final_g = final_norm_g[None]

    xc = x_prompt.reshape(nb_c * seq_c, D_MODEL)
    xl = x_sample.reshape(nb_l * seq_l, D_MODEL)
    sf_list, sb_list, ckv_list, kr_list = [], [], [], []
    for l in range(DEPTH):
        wlt, wla = _lora_weights(rwkv_w_up[l], rwkv_a_up[l])
        wk, wv = _pack_kv(mla_w_kv_up[l])
        lw = {
            "norm_g": norm_g[l][None], "w_in": w_in_p[l], "mu": mu_p[l],
            "rwkv": {
                "wlt": wlt, "wla": wla, "w0": rwkv_w0[l].reshape(1, 2 * BRANCH_W),
                "a0": rwkv_a0[l].reshape(1, 2 * BRANCH_W), "k_k": rwkv_k_k[l][None],
                "k_a": rwkv_k_a[l][None], "r_k": rwkv_r_k[l].reshape(1, BRANCH_W),
                "ln_g": rwkv_ln_g[l][None], "ln_b": rwkv_ln_b[l][None],
            },
            "mla": {
                "q_norm": mla_q_norm[l][None], "kv_norm": mla_kv_norm[l][None],
                "wq": _pack_q(mla_w_q_up[l]).astype(BF16), "wk": wk.astype(BF16),
                "wv": wv.astype(BF16), "pk": pk,
            },
            "gmlp": {
                "ln_g": gmlp_ln_g[l][None], "ln_b": gmlp_ln_b[l][None],
                "w_s": gmlp_w_s[l].astype(BF16),
                "b_s": jnp.repeat(gmlp_b_s[l].T, BRANCH_W // G_C, axis=1),
            },
            "dft": dft,
            "w_merge": w_merge[l].astype(BF16), "b_merge": b_merge[l][None],
            "w_branch": w_branch[l].astype(BF16), "w_out": w_out[l].astype(BF16),
        }
        final = l == DEPTH - 1
        ss_c = jnp.broadcast_to(mod[l, 0][None], (n_ctx_tiles, 3, D_MODEL))
        ss_l = mod[l, 1:1 + nb_l]
        xc, sf, sb, ckv, kr = _trunk_layer(xc, ss_c, lw, nb_c, seq_c, None, None, final_g, final)
        sf_list.append(sf)
        sb_list.append(sb)
        ckv_list.append(ckv.reshape(nb_c, seq_c, KV_LORA))
        kr_list.append(kr[:, :ROPE].reshape(nb_c, seq_c, ROPE))
        ctx = (state_rwkv_fwd[:, l], state_rwkv_bwd[:, l], cache_mla_ckv[:, l], cache_mla_krope[:, l])
        xl = _trunk_layer(xl, ss_l, lw, nb_l, seq_l, rope_tabs, ctx, final_g, final)[0]

    return (xc.reshape(nb_c, seq_c, D_MODEL), xl.reshape(nb_l, seq_l, D_MODEL),
            jnp.stack(sf_list, axis=1), jnp.stack(sb_list, axis=1),
            jnp.stack(ckv_list, axis=1), jnp.stack(kr_list, axis=1))
```

```python
import functools
import math

import jax
import jax.numpy as jnp
import numpy as np
from jax import lax
from jax.experimental import pallas as pl
from jax.experimental.pallas import tpu as pltpu

F32 = jnp.float32
BF16 = jnp.bfloat16

D_MODEL = 1024
DEPTH = 2
GRID_W = 64
BRANCH_W = 512
H_A = 8
N_A = 64
LORA = 64
RWKV_GN_EPS = 64e-5
H_B = 8
NOPE = 64
ROPE = 32
VDIM = 64
Q_LORA = 256
KV_LORA = 128
ROPE_BASE = 10000.0
Q_PRESCALE = (NOPE + ROPE) ** -0.5 * math.log2(math.e)
G_C = 4
CHUNK = 128
G_D = 4
NORM_EPS = 1e-6
SHIFT_W = 1728

LANES = 128
VMEM_LIMIT = 52 * 1024 * 1024

ZW = 6144
Z_SHIFT_PAD = 2048
COL_R, COL_K, COL_V = 0, 1, 2
COL_LORA = 6
COL_GA, COL_ZB, COL_GB, COL_U, COL_VC, COL_GC, COL_F, COL_GD = 4, 5, 6, 7, 8, 9, 10, 11
FRONT_TM = 2048
FRONT_TN = 512
N_SHIFT_TILES = Z_SHIFT_PAD // FRONT_TN
SCAN_C = 64
SCAN_NB = 2
ATTN_GROUP = 4


def _mm(a, b):
    return jnp.dot(a.astype(BF16), b.astype(BF16), preferred_element_type=F32)


def _mm_nt(a, b):
    return lax.dot_general(a.astype(BF16), b.astype(BF16), (((1,), (1,)), ((), ())),
                           preferred_element_type=F32)


def _split3(x):
    hi = x.astype(BF16)
    r1 = x - hi.astype(F32)
    mid = r1.astype(BF16)
    lo = (r1 - mid.astype(F32)).astype(BF16)
    return hi, mid, lo


def _mm_exact_lhs01(m01, x):
    hi, mid, lo = _split3(x)
    d = functools.partial(jnp.dot, preferred_element_type=F32)
    return d(m01, hi) + d(m01, mid) + d(m01, lo)


def _silu(x):
    return x * jax.nn.sigmoid(x)


def _rms(x, g):
    return x * lax.rsqrt(jnp.mean(x * x, axis=-1, keepdims=True) + NORM_EPS) * g


def _cparams(sem, vmem=VMEM_LIMIT):
    return pltpu.CompilerParams(dimension_semantics=sem, vmem_limit_bytes=vmem)


def _mod_kernel(c_ref, w_ref, b_ref, o_ref):
    o_ref[0] = _mm(_silu(c_ref[...]), w_ref[0]) + b_ref[0]


def _modulation(cond8, w_ada, b_ada):
    return pl.pallas_call(
        _mod_kernel,
        grid=(DEPTH, 3),
        in_specs=[
            pl.BlockSpec((8, D_MODEL), lambda l, j: (0, 0)),
            pl.BlockSpec((1, D_MODEL, D_MODEL), lambda l, j: (l, 0, j)),
            pl.BlockSpec((1, 1, D_MODEL), lambda l, j: (l, 0, j)),
        ],
        out_specs=pl.BlockSpec((1, 8, D_MODEL), lambda l, j: (l, 0, j)),
        out_shape=jax.ShapeDtypeStruct((DEPTH, 8, 3 * D_MODEL), F32),
        compiler_params=_cparams(("arbitrary", "arbitrary")),
        name="modulation",
    )(cond8, w_ada, b_ada.reshape(DEPTH, 1, 3 * D_MODEL))


def _front_kernel(x_ref, ss_ref, g_ref, w_ref, mu_ref, z_ref, h_scr, *, seq):
    j = pl.program_id(1)

    @pl.when(j == 0)
    def _():
        h = _rms(x_ref[...], g_ref[...]) * (1.0 + ss_ref[0, 1:2, :]) + ss_ref[0, 0:1, :]
        h_scr[...] = h.astype(BF16)

    z = jnp.dot(h_scr[...], w_ref[...], preferred_element_type=F32)

    @pl.when(j < N_SHIFT_TILES)
    def _():
        t = lax.broadcasted_iota(jnp.int32, z.shape, 0) & (seq - 1)
        prev = jnp.where(t == 0, 0.0, pltpu.roll(z, 1, axis=0))
        nxt = jnp.where(t == seq - 1, 0.0, pltpu.roll(z, FRONT_TM - 1, axis=0))
        z_ref[...] = z + (0.5 * (prev + nxt) - z) * mu_ref[...]

    @pl.when(j >= N_SHIFT_TILES)
    def _():
        z_ref[...] = z


def _front(x2d, ss, norm_g, w_in_p, mu_p, seq):
    n = x2d.shape[0]
    return pl.pallas_call(
        functools.partial(_front_kernel, seq=seq),
        grid=(n // FRONT_TM, ZW // FRONT_TN),
        in_specs=[
            pl.BlockSpec((FRONT_TM, D_MODEL), lambda i, j: (i, 0)),
            pl.BlockSpec((1, 2, D_MODEL), lambda i, j: (i, 0, 0)),
            pl.BlockSpec((1, D_MODEL), lambda i, j: (0, 0)),
            pl.BlockSpec((D_MODEL, FRONT_TN), lambda i, j: (0, j)),
            pl.BlockSpec((1, FRONT_TN), lambda i, j: (0, jnp.minimum(j, N_SHIFT_TILES - 1))),
        ],
        out_specs=pl.BlockSpec((FRONT_TM, FRONT_TN), lambda i, j: (i, j)),
        out_shape=jax.ShapeDtypeStruct((n, ZW), F32),
        scratch_shapes=[pltpu.VMEM((FRONT_TM, D_MODEL), BF16)],
        compiler_params=_cparams(("arbitrary", "arbitrary")),
        name="front",
    )(x2d, ss, norm_g, w_in_p, mu_p)


def _scan_chunks(dirs, s_ref):
    c = SCAN_C
    n2 = 2 * c
    ri = lax.broadcasted_iota(jnp.int32, (c, c), 0)
    ci = lax.broadcasted_iota(jnp.int32, (c, c), 1)
    row = lax.broadcasted_iota(jnp.int32, (n2, n2), 0)
    col = lax.broadcasted_iota(jnp.int32, (n2, n2), 1)
    same64 = (row >> 6) == (col >> 6)
    same16 = (row >> 4) == (col >> 4)
    tl = row & (c - 1)
    il = col & (c - 1)
    eye = (row == col).astype(F32)
    lane_a = lax.broadcasted_iota(jnp.int32, (c, LANES), 1) < N_A

    def stack2(x):
        return jnp.concatenate([jnp.where(lane_a, x, 0.0), jnp.where(lane_a, 0.0, x)], axis=0)

    lhs, rhs, v2, bk, gtot, strict, incl, key = [], [], [], [], [], [], [], []
    for r, kk, v, lw, kt, bv, rev, j, d in dirs:
        tri = ((ri <= ci) if rev else (ri >= ci)).astype(BF16)
        cs = _mm_exact_lhs01(tri, lw)
        tot = cs[0:1] if rev else cs[c - 1:c]
        g_tot = jnp.exp(tot)
        a_t = -kk * jnp.exp(cs - lw)
        r_t = r * jnp.exp(cs)
        g_inv = jnp.exp(-cs)
        b_t = bv * g_inv
        k_t = kt * g_inv
        g_rem = jnp.exp(tot - cs)
        b_h = bv * g_rem
        k_h = kt * g_rem
        st = same64 & ((il > tl) if rev else (il < tl))
        inc = same64 & ((il >= tl) if rev else (il <= tl))
        for p in range(H_A // 2):
            sl = slice(p * LANES, (p + 1) * LANES)
            lhs.append(jnp.concatenate([stack2(a_t[:, sl]), stack2(r_t[:, sl])], axis=0))
            rhs.append(jnp.concatenate([b_t[:, sl], k_t[:, sl]], axis=0))
            v2.append(stack2(v[:, sl]))
            bk.append(jnp.concatenate([stack2(b_h[:, sl]), stack2(k_h[:, sl])], axis=0))
            gtot.append(g_tot[:, sl])
            strict.append(st)
            incl.append(inc)
            key.append((j, d, p))
    ch = range(len(key))

    s = [s_ref[key[i]] for i in ch]
    big = [_mm_nt(lhs[i], jnp.concatenate([rhs[i], s[i]], axis=0)) for i in ch]
    lab, lak, mrbk = [], [], []
    for i in ch:
        x = big[i][:n2, :n2]
        xr = pltpu.roll(x, c, axis=1)
        y = big[i][n2:, :n2]
        yr = pltpu.roll(y, c, axis=1)
        pick = lambda top, bot: jnp.concatenate([top[:c], bot[c:]], axis=0)
        lab.append(jnp.where(strict[i], pick(x, xr), 0.0))
        lak.append(jnp.where(strict[i], pick(xr, x), 0.0))
        mrbk.append(jnp.concatenate([jnp.where(incl[i], pick(y, yr), 0.0),
                                     jnp.where(incl[i], pick(yr, y), 0.0)], axis=1))
    dg = [jnp.where(same16, lab[i], 0.0) for i in ch]
    off = [lab[i] - dg[i] for i in ch]
    pinv = [eye + dg[i] for i in ch]
    pw = [_mm(dg[i], dg[i]) for i in ch]
    for _ in range(2):
        t = [_mm(pw[i], jnp.concatenate([pw[i], pinv[i]], axis=1)) for i in ch]
        pinv = [pinv[i] + t[i][:, n2:] for i in ch]
        pw = [t[i][:, :n2] for i in ch]
    pinv = [pinv[i] + _mm(pw[i], pinv[i]) for i in ch]
    f = [_mm(pinv[i], off[i]) for i in ch]
    t = [_mm(f[i], jnp.concatenate([f[i], pinv[i]], axis=1)) for i in ch]
    g = [pinv[i] + t[i][:, n2:] for i in ch]
    tinv = [g[i] + _mm(t[i][:, :n2], g[i]) for i in ch]
    lv = [_mm(lak[i], v2[i]) for i in ch]
    u2 = [_mm(tinv[i], big[i][:n2, n2:] + lv[i]) for i in ch]
    uv = [jnp.concatenate([u2[i], v2[i]], axis=0) for i in ch]
    y2 = [big[i][n2:, n2:] + _mm(mrbk[i], uv[i]) for i in ch]
    for i in ch:
        s_ref[key[i]] = s[i] * gtot[i] + _mm(uv[i].T, bk[i])
    npair = H_A // 2
    return [jnp.concatenate([y2[g * npair + p][:c] + y2[g * npair + p][c:] for p in range(npair)], axis=1)
            for g in range(len(dirs))]


def _head_sums(x):
    lane_a = lax.broadcasted_iota(jnp.int32, (x.shape[0], LANES), 1) < N_A
    outs = []
    for p in range(H_A // 2):
        xs = x[:, p * LANES:(p + 1) * LANES]
        sa = jnp.sum(jnp.where(lane_a, xs, 0.0), axis=-1, keepdims=True)
        sb = jnp.sum(jnp.where(lane_a, 0.0, xs), axis=-1, keepdims=True)
        outs.append(jnp.where(lane_a, sa, sb))
    return jnp.concatenate(outs, axis=1)


def _rwkv_prepare(k, lora, d, w):
    wlt_ref, wla_ref, w0_ref, a0_ref, kk_ref, ka_ref, rk_ref = w
    hs = slice(d * BRANCH_W, (d + 1) * BRANCH_W)
    pre = w0_ref[:, hs] + _mm(jnp.tanh(lora), wlt_ref[:, hs])
    lw = -math.exp(-0.5) * jax.nn.sigmoid(pre)
    kkraw = k * kk_ref[...]
    kk = kkraw * lax.rsqrt(jnp.maximum(_head_sums(kkraw * kkraw), 1e-24))
    k_a = ka_ref[...]
    if d:
        a_b = jax.nn.sigmoid(a0_ref[:, hs] + _mm(lora, wla_ref[:, hs]))
        return kk, lw, k * (1.0 + (a_b - 1.0) * k_a), kk * a_b
    a2 = jax.nn.sigmoid(a0_ref[...] + _mm(lora, wla_ref[...]))
    ktf = k * (1.0 + (a2[:, :BRANCH_W] - 1.0) * k_a)
    ktb = k * (1.0 + (a2[:, BRANCH_W:] - 1.0) * k_a)
    return kk, lw, ktf, kk * a2[:, hs], (ktf + ktb) * rk_ref[...]


def _rwkv_kernel(*refs, has_init):
    if has_init:
        s0_ref, refs = refs[0], refs[1:]
    cur = (refs[0:5], refs[5:10])
    nxt = (refs[10:12], refs[12:14])
    w = refs[14:21]
    lng_ref, lnb_ref = refs[21:23]
    o_ref, sf_ref, sb_ref, s_scr, ysum_scr, kts_scr, p_scr = refs[23:30]
    c = pl.program_id(1)
    nc = pl.num_programs(1)

    @pl.when(c == 0)
    def _():
        if has_init:
            s_scr[...] = s0_ref[...]
        else:
            s_scr[...] = jnp.zeros_like(s_scr)

    @pl.when((c == 0) & (pl.program_id(0) == 0))
    def _():
        for j in range(SCAN_NB):
            for d in range(2):
                for i, a in enumerate(_rwkv_prepare(cur[d][3][j], cur[d][4][j], d, w)):
                    p_scr[j, d, i] = a

    dirs = []
    for j in range(SCAN_NB):
        for d in range(2):
            kk, lw, kt, bv = [p_scr[j, d, i] for i in range(4)]
            dirs.append((cur[d][0][j], kk, cur[d][1][j], lw, kt, bv, d == 1, j, d))
    kts = [p_scr[j, 0, 4] for j in range(SCAN_NB)]
    ys = _scan_chunks(dirs, s_scr)

    for j in range(SCAN_NB):
        for d in range(2):
            for i, a in enumerate(_rwkv_prepare(nxt[d][0][j], nxt[d][1][j], d, w)):
                p_scr[j, d, i] = a

    rows = (pl.multiple_of(c * SCAN_C, SCAN_C), pl.multiple_of((nc - 1 - c) * SCAN_C, SCAN_C))

    @pl.when(c < nc // 2)
    def _():
        for j in range(SCAN_NB):
            kts_scr[j, pl.ds(rows[0], SCAN_C), :] = kts[j]
            for d in range(2):
                ysum_scr[j, pl.ds(rows[d], SCAN_C), :] = ys[2 * j + d]

    @pl.when(c >= nc // 2)
    def _():
        for j in range(SCAN_NB):
            for d in range(2):
                r, v, g = cur[d][0][j], cur[d][1][j], cur[d][2][j]
                bonus = _head_sums(r * (kts_scr[j, pl.ds(rows[1], SCAN_C), :] if d else kts[j])) * v
                o = ysum_scr[j, pl.ds(rows[d], SCAN_C), :] + ys[2 * j + d]
                dlt = o - _head_sums(o) * (1.0 / N_A)
                var = _head_sums(dlt * dlt) * (1.0 / N_A)
                y = dlt * lax.rsqrt(var + RWKV_GN_EPS) * lng_ref[...] + lnb_ref[...] + bonus
                o_ref[j, pl.ds(rows[d], SCAN_C), :] = (y * _silu(g)).astype(BF16)

    @pl.when(c == nc - 1)
    def _():
        for j in range(SCAN_NB):
            for d, st_ref in enumerate((sf_ref, sb_ref)):
                for p in range(H_A // 2):
                    s = s_scr[j, d, p]
                    st_ref[j, 2 * p] = s[:N_A, :N_A]
                    st_ref[j, 2 * p + 1] = pltpu.roll(s, N_A, axis=1)[N_A:, :N_A]


def _rwkv(z, wp, s0_bd, nb, seq):
    n = z.shape[0]
    nc = seq // SCAN_C
    assert nc % 2 == 0 and nb % SCAN_NB == 0
    z3 = z.reshape(nb, seq, ZW)
    chunk = ((lambda c: c), (lambda c: nc - 1 - c))
    spec = lambda idx, col, w=BRANCH_W: pl.BlockSpec((SCAN_NB, SCAN_C, w), lambda b, c: (b, idx(c), col))
    in_specs = []
    for d in range(2):
        in_specs += [spec(chunk[d], COL_R), spec(chunk[d], COL_V), spec(chunk[d], COL_GA),
                     spec(chunk[d], COL_K), spec(chunk[d], COL_LORA, 256)]

    def spec_next(d, col, w=BRANCH_W):
        def index(b, c):
            wrap = c == nc - 1
            nb_i = jnp.minimum(b + wrap.astype(jnp.int32), nb // SCAN_NB - 1)
            return (nb_i, chunk[d](jnp.where(wrap, 0, c + 1)), col)
        return pl.BlockSpec((SCAN_NB, SCAN_C, w), index)

    for d in range(2):
        in_specs += [spec_next(d, COL_K), spec_next(d, COL_LORA, 256)]
    full = lambda a: pl.BlockSpec(a.shape, lambda b, c: (0,) * a.ndim)
    params = (wp["wlt"], wp["wla"], wp["w0"], wp["a0"], wp["k_k"], wp["k_a"], wp["r_k"], wp["ln_g"], wp["ln_b"])
    in_specs += [full(p) for p in params]
    args = [z3] * 14 + list(params)
    sspec = pl.BlockSpec((SCAN_NB, 2, H_A // 2, LANES, LANES), lambda b, c: (b, 0, 0, 0, 0))
    has_init = s0_bd is not None
    if has_init:
        in_specs = [sspec] + in_specs
        args = [s0_bd] + args
    fspec = pl.BlockSpec((SCAN_NB, H_A, N_A, N_A), lambda b, c: (b, 0, 0, 0))
    o_a, sf, sb = pl.pallas_call(
        functools.partial(_rwkv_kernel, has_init=has_init),
        grid=(nb // SCAN_NB, nc),
        in_specs=in_specs,
        out_specs=[pl.BlockSpec((SCAN_NB, seq, BRANCH_W), lambda b, c: (b, 0, 0)), fspec, fspec],
        out_shape=[jax.ShapeDtypeStruct((nb, seq, BRANCH_W), BF16),
                   jax.ShapeDtypeStruct((nb, H_A, N_A, N_A), F32),
                   jax.ShapeDtypeStruct((nb, H_A, N_A, N_A), F32)],
        scratch_shapes=[pltpu.VMEM((SCAN_NB, 2, H_A // 2, LANES, LANES), F32),
                        pltpu.VMEM((SCAN_NB, seq, BRANCH_W), F32),
                        pltpu.VMEM((SCAN_NB, seq, BRANCH_W), F32),
                        pltpu.VMEM((SCAN_NB, 2, 5, SCAN_C, BRANCH_W), F32)],
        compiler_params=_cparams(("arbitrary", "arbitrary")),
        name="rwkv",
    )(*args)
    return o_a.reshape(n, BRANCH_W), sf, sb


def _swap_pairs(x):
    w = x.shape[1]
    even = (lax.broadcasted_iota(jnp.int32, x.shape, 1) & 1) == 0
    return jnp.where(even, pltpu.roll(x, w - 1, axis=1), pltpu.roll(x, 1, axis=1))


def _mla_prep_kernel(*refs, rope):
    zb_ref, gq_ref, gkv_ref, wq_ref, wk_ref, wvt_ref, pk_ref = refs[:7]
    if rope:
        cq_ref, sq_ref, ck_ref, sk_ref = refs[7:11]
        refs = refs[11:]
    else:
        refs = refs[7:]
    q_o, k_o, vt_o, ckv_o, kr_o = refs
    zb = zb_ref[...]
    q = _mm(_rms(zb[:, :Q_LORA], gq_ref[...]), wq_ref[...])
    ckv = _rms(zb[:, Q_LORA:Q_LORA + KV_LORA], gkv_ref[...])
    kr = zb[:, Q_LORA + KV_LORA:]
    if rope:
        cq = jnp.concatenate([cq_ref[...]] * H_B, axis=1)
        sq = jnp.concatenate([sq_ref[...]] * H_B, axis=1)
        q = q * cq + _swap_pairs(q) * sq
        kr = kr * ck_ref[...] + _swap_pairs(kr) * sk_ref[...]
    q_o[...] = (q * Q_PRESCALE).astype(BF16)
    k_o[...] = (_mm(ckv, wk_ref[...]) + _mm(kr, pk_ref[...])).astype(BF16)
    vt_o[...] = _mm_nt(wvt_ref[...], ckv).astype(BF16)
    ckv_o[...] = ckv
    kr_o[...] = kr[:, :ROPE]


def _mla_prep(z, mp, rope_tabs, seq, tm=1024):
    n = z.shape[0]
    full = lambda a: pl.BlockSpec(a.shape, lambda i: (0,) * a.ndim)
    params = (mp["q_norm"], mp["kv_norm"], mp["wq"], mp["wk"], mp["wvt"], mp["pk"])
    in_specs = [pl.BlockSpec((tm, BRANCH_W), lambda i: (i, COL_ZB))] + [full(p) for p in params]
    args = [z, *params]
    rope = rope_tabs is not None
    if rope:
        per = seq // tm
        in_specs += [pl.BlockSpec((tm, LANES), lambda i: (i % per, 0))] * 4
        args += list(rope_tabs)
    rb = lambda w: pl.BlockSpec((tm, w), lambda i: (i, 0))
    return pl.pallas_call(
        functools.partial(_mla_prep_kernel, rope=rope),
        grid=(n // tm,),
        in_specs=in_specs,
        out_specs=[rb(H_B * LANES), rb(H_B * LANES), pl.BlockSpec((H_B * VDIM, tm), lambda i: (0, i)),
                   rb(KV_LORA), rb(ROPE)],
        out_shape=[jax.ShapeDtypeStruct((n, H_B * LANES), BF16),
                   jax.ShapeDtypeStruct((n, H_B * LANES), BF16),
                   jax.ShapeDtypeStruct((H_B * VDIM, n), BF16),
                   jax.ShapeDtypeStruct((n, KV_LORA), F32),
                   jax.ShapeDtypeStruct((n, ROPE), F32)],
        compiler_params=_cparams(("arbitrary",)),
        name="mla_prep",
    )(*args)


def _kv_expand_kernel(ckv_ref, kr_ref, wk_ref, wvt_ref, pk_ref, k_o, vt_o):
    ckv = ckv_ref[...]
    k_o[...] = (_mm(ckv, wk_ref[...]) + _mm(kr_ref[...], pk_ref[...])).astype(BF16)
    vt_o[...] = _mm_nt(wvt_ref[...], ckv).astype(BF16)


def _kv_expand(ckv, kr128, mp):
    m = ckv.shape[0]
    full = lambda a: pl.BlockSpec(a.shape, lambda i: (0,) * a.ndim)
    args = (ckv, kr128, mp["wk"], mp["wvt"], mp["pk"])
    return pl.pallas_call(
        _kv_expand_kernel,
        grid=(1,),
        in_specs=[full(a) for a in args],
        out_specs=[pl.BlockSpec((m, H_B * LANES), lambda i: (0, 0)),
                   pl.BlockSpec((H_B * VDIM, m), lambda i: (0, 0))],
        out_shape=[jax.ShapeDtypeStruct((m, H_B * LANES), BF16),
                   jax.ShapeDtypeStruct((H_B * VDIM, m), BF16)],
        compiler_params=_cparams(("arbitrary",)),
        name="mla_ctx_kv",
    )(*args)


def _attn_kernel(*refs, has_ctx):
    if has_ctx:
        q_ref, k_ref, vt_ref, kc_ref, vct_ref, g_ref, o_ref = refs
    else:
        q_ref, k_ref, vt_ref, g_ref, o_ref = refs
    nt = (((1,), (1,)), ((), ()))
    outs = []
    for h0 in range(0, H_B, ATTN_GROUP):
        heads = range(h0, h0 + ATTN_GROUP)
        hs = [slice(h * LANES, (h + 1) * LANES) for h in heads]
        vh = [slice(h * VDIM, (h + 1) * VDIM) for h in heads]
        gi = range(ATTN_GROUP)
        s = [lax.dot_general(k_ref[:, hs[i]], q_ref[:, hs[i]], nt, preferred_element_type=F32) for i in gi]
        m = [jnp.max(s[i], axis=0, keepdims=True) for i in gi]
        if has_ctx:
            sc = [lax.dot_general(kc_ref[:, hs[i]], q_ref[:, hs[i]], nt, preferred_element_type=F32) for i in gi]
            m = [jnp.maximum(m[i], jnp.max(sc[i], axis=0, keepdims=True)) for i in gi]
        e = [jnp.exp2(s[i] - m[i]) for i in gi]
        den = [jnp.sum(e[i], axis=0, keepdims=True) for i in gi]
        o = [jnp.dot(vt_ref[vh[i], :], e[i].astype(BF16), preferred_element_type=F32) for i in gi]
        if has_ctx:
            ec = [jnp.exp2(sc[i] - m[i]) for i in gi]
            den = [den[i] + jnp.sum(ec[i], axis=0, keepdims=True) for i in gi]
            o = [o[i] + jnp.dot(vct_ref[vh[i], :], ec[i].astype(BF16), preferred_element_type=F32) for i in gi]
        outs += [o[i] / den[i] for i in gi]
    o_ref[...] = (jnp.concatenate(outs, axis=0).T * _silu(g_ref[...])).astype(BF16)


def _attention(z, q, k, vt, kc, vct, nb, seq, tq=256):
    n = z.shape[0]
    nq = seq // tq
    has_ctx = kc is not None
    in_specs = [pl.BlockSpec((tq, H_B * LANES), lambda b, i: (b * nq + i, 0)),
                pl.BlockSpec((seq, H_B * LANES), lambda b, i: (b, 0)),
                pl.BlockSpec((H_B * VDIM, seq), lambda b, i: (0, b))]
    args = [q, k, vt]
    if has_ctx:
        past = kc.shape[0] // nb
        in_specs += [pl.BlockSpec((past, H_B * LANES), lambda b, i: (b, 0)),
                     pl.BlockSpec((H_B * VDIM, past), lambda b, i: (0, b))]
        args += [kc, vct]
    in_specs.append(pl.BlockSpec((tq, BRANCH_W), lambda b, i: (b * nq + i, COL_GB)))
    args.append(z)
    return pl.pallas_call(
        functools.partial(_attn_kernel, has_ctx=has_ctx),
        grid=(nb, nq),
        in_specs=in_specs,
        out_specs=pl.BlockSpec((tq, BRANCH_W), lambda b, i: (b * nq + i, 0)),
        out_shape=jax.ShapeDtypeStruct((n, BRANCH_W), BF16),
        compiler_params=_cparams(("arbitrary", "arbitrary")),
        name="mla_attention",
    )(*args)


def _gmlp_kernel(u_ref, vc_ref, g_ref, lng_ref, lnb_ref, ws_ref, bs_ref, o_ref):
    tm = u_ref.shape[0]
    u = jax.nn.gelu(u_ref[...])
    x = jax.nn.gelu(vc_ref[...])
    mu = jnp.mean(x, axis=-1, keepdims=True)
    xc = x - mu
    var = jnp.mean(xc * xc, axis=-1, keepdims=True)
    vn = (xc * lax.rsqrt(var + 1e-5) * lng_ref[...] + lnb_ref[...]).astype(BF16)
    rows = []
    for c in range(tm // CHUNK):
        cols = []
        for g in range(G_C):
            blk = vn[c * CHUNK:(c + 1) * CHUNK, g * LANES:(g + 1) * LANES]
            cols.append(jnp.dot(ws_ref[g], blk, preferred_element_type=F32))
        rows.append(jnp.concatenate(cols, axis=1) + bs_ref[...])
    mixed = jnp.concatenate(rows, axis=0)
    o_ref[...] = (u * mixed * _silu(g_ref[...])).astype(BF16)


def _gmlp(z, gp, tm=512):
    n = z.shape[0]
    zb = lambda c: pl.BlockSpec((tm, BRANCH_W), lambda i, c=c: (i, c))
    full = lambda a: pl.BlockSpec(a.shape, lambda i: (0,) * a.ndim)
    params = (gp["ln_g"], gp["ln_b"], gp["w_s"], gp["b_s"])
    return pl.pallas_call(
        _gmlp_kernel,
        grid=(n // tm,),
        in_specs=[zb(COL_U), zb(COL_VC), zb(COL_GC)] + [full(p) for p in params],
        out_specs=pl.BlockSpec((tm, BRANCH_W), lambda i: (i, 0)),
        out_shape=jax.ShapeDtypeStruct((n, BRANCH_W), BF16),
        compiler_params=_cparams(("arbitrary",)),
        name="gmlp",
    )(z, z, z, *params)


def _fnet_kernel(f_ref, g_ref, cs_ref, dft_ref, o_ref, xcs_scr, *, seq):
    r = pl.program_id(1)
    nbs = f_ref.shape[0]

    @pl.when(r == 0)
    def _():
        for b in range(nbs):
            for g in range(G_D):
                xg = _mm(f_ref[b, :, g * LANES:(g + 1) * LANES], cs_ref[...])
                xcs_scr[b, 0:seq, g * LANES:(g + 1) * LANES] = xg[:, :LANES].astype(BF16)
                xcs_scr[b, seq:2 * seq, g * LANES:(g + 1) * LANES] = xg[:, LANES:].astype(BF16)

    for b in range(nbs):
        y = jnp.dot(dft_ref[...], xcs_scr[b], preferred_element_type=F32)
        o_ref[b] = (y * (1.0 / math.sqrt(seq * LANES)) * _silu(g_ref[b])).astype(BF16)


def _fnet(z, cs128, dft, nb, seq, tr=256):
    n = z.shape[0]
    nr = seq // tr
    nbs = max(1, min(nb, FRONT_TM // seq))
    z3 = z.reshape(nb, seq, ZW)
    out = pl.pallas_call(
        functools.partial(_fnet_kernel, seq=seq),
        grid=(nb // nbs, nr),
        in_specs=[pl.BlockSpec((nbs, seq, BRANCH_W), lambda b, r: (b, 0, COL_F)),
                  pl.BlockSpec((nbs, tr, BRANCH_W), lambda b, r: (b, r, COL_GD)),
                  pl.BlockSpec((LANES, 2 * LANES), lambda b, r: (0, 0)),
                  pl.BlockSpec((tr, 2 * seq), lambda b, r: (r, 0))],
        out_specs=pl.BlockSpec((nbs, tr, BRANCH_W), lambda b, r: (b, r, 0)),
        out_shape=jax.ShapeDtypeStruct((nb, seq, BRANCH_W), BF16),
        scratch_shapes=[pltpu.VMEM((nbs, 2 * seq, BRANCH_W), BF16)],
        compiler_params=_cparams(("arbitrary", "arbitrary")),
        name="fnet",
    )(z3, z3, cs128, dft)
    return out.reshape(n, BRANCH_W)


def _back_kernel(x_ref, ss_ref, g_ref, oa_ref, ob_ref, oc_ref, od_ref, wm_ref, bm_ref,
                 wb_ref, wo_ref, fg_ref, y_ref, h_scr, acc_scr, *, final):
    nb = pl.program_id(1)

    @pl.when(nb == 0)
    def _():
        h = _rms(x_ref[...], g_ref[...]) * (1.0 + ss_ref[0, 1:2, :]) + ss_ref[0, 0:1, :]
        h_scr[...] = h.astype(BF16)
        acc_scr[...] = jnp.zeros_like(acc_scr)

    gates = jax.nn.sigmoid(jnp.dot(h_scr[...], wm_ref[...], preferred_element_type=F32)
                           + bm_ref[...])
    for i, o_ref in enumerate((oa_ref, ob_ref, oc_ref, od_ref)):
        @pl.when(nb == i)
        def _(o_ref=o_ref):
            acc_scr[...] += gates * jnp.dot(o_ref[...], wb_ref[0], preferred_element_type=F32)

    @pl.when(nb == 3)
    def _():
        xn = x_ref[...] + ss_ref[0, 2:3, :] * _mm(acc_scr[...], wo_ref[...])
        if final:
            xn = _rms(xn, fg_ref[...])
        y_ref[...] = xn


def _back(x2d, ss, norm_g, o_a, o_b, o_c, o_d, lw, final_g, final, tm=1024):
    n = x2d.shape[0]
    rows_per_ss = FRONT_TM // tm
    ob = pl.BlockSpec((tm, BRANCH_W), lambda i, j: (i, 0))
    return pl.pallas_call(
        functools.partial(_back_kernel, final=final),
        grid=(n // tm, 4),
        in_specs=[pl.BlockSpec((tm, D_MODEL), lambda i, j: (i, 0)),
                  pl.BlockSpec((1, 3, D_MODEL), lambda i, j: (i // rows_per_ss, 0, 0)),
                  pl.BlockSpec((1, D_MODEL), lambda i, j: (0, 0)),
                  ob, ob, ob, ob,
                  pl.BlockSpec((D_MODEL, D_MODEL), lambda i, j: (0, j)),
                  pl.BlockSpec((1, D_MODEL), lambda i, j: (0, j)),
                  pl.BlockSpec((1, BRANCH_W, D_MODEL), lambda i, j: (j, 0, 0)),
                  pl.BlockSpec((D_MODEL, D_MODEL), lambda i, j: (0, 0)),
                  pl.BlockSpec((1, D_MODEL), lambda i, j: (0, 0))],
        out_specs=pl.BlockSpec((tm, D_MODEL), lambda i, j: (i, 0)),
        out_shape=jax.ShapeDtypeStruct((n, D_MODEL), F32),
        scratch_shapes=[pltpu.VMEM((tm, D_MODEL), BF16), pltpu.VMEM((tm, D_MODEL), F32)],
        compiler_params=_cparams(("arbitrary", "arbitrary")),
        name="back",
    )(x2d, ss, norm_g, o_a, o_b, o_c, o_d, lw["w_merge"], lw["b_merge"], lw["w_branch"],
      lw["w_out"], final_g)


def _pad_cols(w):
    z = lambda k: jnp.zeros(w.shape[:-1] + (k,), w.dtype)
    return jnp.concatenate([w[..., :SHIFT_W], z(Z_SHIFT_PAD - SHIFT_W), w[..., SHIFT_W:2240],
                            w[..., 2240:2656], z(3072 - 2560 - 416), w[..., 2656:]], axis=-1)


def _rope_place():
    p = np.zeros((LANES, H_B * LANES), np.float32)
    for h in range(H_B):
        for j in range(ROPE):
            p[j, h * LANES + NOPE + j] = 1.0
    return jnp.asarray(p, BF16)


def _pack_q(w):
    w = w.reshape(Q_LORA, H_B, NOPE + ROPE)
    return jnp.pad(w, ((0, 0), (0, 0), (0, LANES - NOPE - ROPE))).reshape(Q_LORA, H_B * LANES)


def _pack_kv(w):
    w = w.reshape(KV_LORA, H_B, NOPE + VDIM)
    wk = jnp.pad(w[..., :NOPE], ((0, 0), (0, 0), (0, LANES - NOPE))).reshape(KV_LORA, H_B * LANES)
    return wk, w[..., NOPE:].reshape(KV_LORA, H_B * VDIM)


def _rope_tables(n_tokens):
    rows = n_tokens // GRID_W
    row = jnp.repeat(jnp.arange(rows, dtype=F32), GRID_W)
    col = jnp.tile(jnp.arange(GRID_W, dtype=F32), rows)
    n_freq = ROPE // 4
    inv = ROPE_BASE ** (-jnp.arange(n_freq, dtype=F32) / n_freq)
    ang = jnp.concatenate([row[:, None] * inv, col[:, None] * inv], axis=-1)
    cos = jnp.repeat(jnp.cos(ang), 2, axis=-1)
    sin = jnp.repeat(jnp.sin(ang), 2, axis=-1) * jnp.tile(jnp.asarray([-1.0, 1.0], F32), ROPE // 2)
    ones = lambda k: jnp.ones((n_tokens, k), F32)
    zeros = lambda k: jnp.zeros((n_tokens, k), F32)
    cq = jnp.concatenate([ones(NOPE), cos, ones(LANES - NOPE - ROPE)], axis=-1)
    sq = jnp.concatenate([zeros(NOPE), sin, zeros(LANES - NOPE - ROPE)], axis=-1)
    ck = jnp.concatenate([cos, ones(LANES - ROPE)], axis=-1)
    sk = jnp.concatenate([sin, zeros(LANES - ROPE)], axis=-1)
    return cq, sq, ck, sk


def _dft_tables(seq):
    k = np.arange(LANES)
    a = 2.0 * np.pi * ((k[:, None] * k[None, :]) % LANES) / LANES
    cs128 = np.concatenate([np.cos(a), np.sin(a)], axis=1)
    t = np.arange(seq)
    b = 2.0 * np.pi * ((t[:, None] * t[None, :]) % seq) / seq
    dft = np.concatenate([np.cos(b), -np.sin(b)], axis=1)
    return (jnp.asarray(cs128.astype(np.float32)).astype(BF16),
            jnp.asarray(dft.astype(np.float32)).astype(BF16))


def _lora_weights(w_up, a_up):
    z = jnp.zeros((LORA, BRANCH_W), F32)
    pad = jnp.zeros((256 - 3 * LORA, 2 * BRANCH_W), F32)
    wlt = jnp.concatenate([jnp.concatenate([w_up[0], z], 1), jnp.concatenate([z, w_up[1]], 1),
                           jnp.concatenate([z, z], 1), pad], axis=0)
    wla = jnp.concatenate([jnp.concatenate([z, z], 1), jnp.concatenate([z, z], 1),
                           jnp.concatenate([a_up[0], a_up[1]], 1), pad], axis=0)
    return wlt.astype(BF16), wla.astype(BF16)


def _to_blockdiag(s):
    b = s.shape[0]
    s = s.reshape(b, H_A // 2, 2, N_A, N_A)
    z = jnp.zeros_like(s[:, :, 0])
    top = jnp.concatenate([s[:, :, 0], z], axis=-1)
    bot = jnp.concatenate([z, s[:, :, 1]], axis=-1)
    return jnp.concatenate([top, bot], axis=-2)


def _trunk_layer(x2d, ss, lw, nb, seq, rope_tabs, ctx, final_g, final):
    z = _front(x2d, ss[:, :2], lw["norm_g"], lw["w_in"], lw["mu"], seq)
    s0 = None
    if ctx is not None:
        s0 = jnp.stack([_to_blockdiag(ctx[0]), _to_blockdiag(ctx[1])], axis=1)
    o_a, sf, sb = _rwkv(z, lw["rwkv"], s0, nb, seq)
    q, k, v, ckv, kr = _mla_prep(z, lw["mla"], rope_tabs, seq)
    kc = vc = None
    if ctx is not None:
        past = ctx[2].shape[1]
        kr_c = jnp.pad(ctx[3].reshape(nb * past, ROPE), ((0, 0), (0, LANES - ROPE)))
        kc, vc = _kv_expand(ctx[2].reshape(nb * past, KV_LORA), kr_c, lw["mla"])
    o_b = _attention(z, q, k, v, kc, vc, nb, seq)
    o_c = _gmlp(z, lw["gmlp"])
    o_d = _fnet(z, *lw["dft"][seq], nb, seq)
    x_new = _back(x2d, ss, lw["norm_g"], o_a, o_b, o_c, o_d, lw, final_g, final)
    return x_new, sf, sb, ckv, kr


def kernel(x_prompt, x_sample, state_rwkv_fwd, state_rwkv_bwd, cache_mla_ckv, cache_mla_krope, c, c_ctx, norm_g, w_ada, b_ada, w_in, shift_mu, rwkv_w0, rwkv_w_up, rwkv_a0, rwkv_a_up, rwkv_k_k, rwkv_k_a, rwkv_r_k, rwkv_ln_g, rwkv_ln_b, mla_q_norm, mla_w_q_up, mla_kv_norm, mla_w_kv_up, gmlp_ln_g, gmlp_ln_b, gmlp_w_s, gmlp_b_s, w_branch, w_merge, b_merge, w_out, final_norm_g):
    nb_c, seq_c, _ = x_prompt.shape
    nb_l, seq_l, _ = x_sample.shape
    assert (nb_c * seq_c) % FRONT_TM == 0 and FRONT_TM % seq_c == 0 and seq_l == FRONT_TM

    cond8 = jnp.concatenate([c_ctx[None], c, jnp.zeros((8 - 1 - nb_l, D_MODEL), F32)], axis=0)
    mod = _modulation(cond8, w_ada, b_ada).reshape(DEPTH, 8, 3, D_MODEL)
    n_ctx_tiles = nb_c * seq_c // FRONT_TM

    pk = _rope_place()
    rope_tabs = _rope_tables(seq_l)
    dft = {s: _dft_tables(s) for s in {seq_c, seq_l}}
    mu_p = jnp.pad(shift_mu, ((0, 0), (0, Z_SHIFT_PAD - SHIFT_W)))[:, None]
    final_g = final_norm_g[None]

    xc = x_prompt.reshape(nb_c * seq_c, D_MODEL)
    xl = x_sample.reshape(nb_l * seq_l, D_MODEL)
    sf_list, sb_list, ckv_list, kr_list = [], [], [], []
    for l in range(DEPTH):
        wlt, wla = _lora_weights(rwkv_w_up[l], rwkv_a_up[l])
        wk, wv = _pack_kv(mla_w_kv_up[l])
        lw = {
            "norm_g": norm_g[l][None], "w_in": _pad_cols(w_in[l]).astype(BF16), "mu": mu_p[l],
            "rwkv": {
                "wlt": wlt, "wla": wla, "w0": rwkv_w0[l].reshape(1, 2 * BRANCH_W),
                "a0": rwkv_a0[l].reshape(1, 2 * BRANCH_W), "k_k": rwkv_k_k[l][None],
                "k_a": rwkv_k_a[l][None], "r_k": rwkv_r_k[l].reshape(1, BRANCH_W),
                "ln_g": rwkv_ln_g[l][None], "ln_b": rwkv_ln_b[l][None],
            },
            "mla": {
                "q_norm": mla_q_norm[l][None], "kv_norm": mla_kv_norm[l][None],
                "wq": _pack_q(mla_w_q_up[l]).astype(BF16), "wk": wk.astype(BF16),
                "wvt": wv.T.astype(BF16), "pk": pk,
            },
            "gmlp": {
                "ln_g": gmlp_ln_g[l][None], "ln_b": gmlp_ln_b[l][None],
                "w_s": gmlp_w_s[l].astype(BF16),
                "b_s": jnp.repeat(gmlp_b_s[l].T, BRANCH_W // G_C, axis=1),
            },
            "dft": dft,
            "w_merge": w_merge[l].astype(BF16), "b_merge": b_merge[l][None],
            "w_branch": w_branch[l].astype(BF16), "w_out": w_out[l].astype(BF16),
        }
        final = l == DEPTH - 1
        ss_c = jnp.broadcast_to(mod[l, 0][None], (n_ctx_tiles, 3, D_MODEL))
        ss_l = mod[l, 1:1 + nb_l]
        xc, sf, sb, ckv, kr = _trunk_layer(xc, ss_c, lw, nb_c, seq_c, None, None, final_g, final)
        sf_list.append(sf)
        sb_list.append(sb)
        ckv_list.append(ckv.reshape(nb_c, seq_c, KV_LORA))
        kr_list.append(kr.reshape(nb_c, seq_c, ROPE))
        ctx = (state_rwkv_fwd[:, l], state_rwkv_bwd[:, l], cache_mla_ckv[:, l], cache_mla_krope[:, l])
        xl = _trunk_layer(xl, ss_l, lw, nb_l, seq_l, rope_tabs, ctx, final_g, final)[0]

    return (xc.reshape(nb_c, seq_c, D_MODEL), xl.reshape(nb_l, seq_l, D_MODEL),
            jnp.stack(sf_list, axis=1), jnp.stack(sb_list, axis=1),
            jnp.stack(ckv_list, axis=1), jnp.stack(kr_list, axis=1))
```

```python
import functools
import math

import jax
import jax.numpy as jnp
import numpy as np
from jax import lax
from jax.experimental import pallas as pl
from jax.experimental.pallas import tpu as pltpu

F32 = jnp.float32
BF16 = jnp.bfloat16

D_MODEL = 1024
DEPTH = 2
GRID_W = 64
BRANCH_W = 512
H_A = 8
N_A = 64
LORA = 64
RWKV_GN_EPS = 64e-5
H_B = 8
NOPE = 64
ROPE = 32
VDIM = 64
Q_LORA = 256
KV_LORA = 128
ROPE_BASE = 10000.0
Q_PRESCALE = (NOPE + ROPE) ** -0.5 * math.log2(math.e)
G_C = 4
CHUNK = 128
G_D = 4
NORM_EPS = 1e-6
SHIFT_W = 1728

LANES = 128
VMEM_LIMIT = 52 * 1024 * 1024

ZW = 6144
Z_SHIFT_PAD = 2048
COL_R, COL_K, COL_V = 0, 1, 2
COL_LORA = 6
COL_GA, COL_ZB, COL_GB, COL_U, COL_VC, COL_GC, COL_F, COL_GD = 4, 5, 6, 7, 8, 9, 10, 11
FRONT_TM = 2048
FRONT_TN = 1024
FRONT_COLS = 256
N_SHIFT_TILES = Z_SHIFT_PAD // FRONT_TN
SCAN_C = 64
SCAN_NB = 2
ATTN_GROUP = 4
ATTN_ROWS = 1024
BACK_ROWS = 256


def _mm(a, b):
    return jnp.dot(a.astype(BF16), b.astype(BF16), preferred_element_type=F32)


def _mm_nt(a, b):
    return lax.dot_general(a.astype(BF16), b.astype(BF16), (((1,), (1,)), ((), ())),
                           preferred_element_type=F32)


def _split3(x):
    hi = x.astype(BF16)
    r1 = x - hi.astype(F32)
    mid = r1.astype(BF16)
    lo = (r1 - mid.astype(F32)).astype(BF16)
    return hi, mid, lo


def _mm_exact_lhs01(m01, x):
    hi, mid, lo = _split3(x)
    d = functools.partial(jnp.dot, preferred_element_type=F32)
    return d(m01, hi) + d(m01, mid) + d(m01, lo)


def _silu(x):
    return x * jax.nn.sigmoid(x)


def _rms(x, g):
    return x * lax.rsqrt(jnp.mean(x * x, axis=-1, keepdims=True) + NORM_EPS) * g


def _cparams(sem, vmem=VMEM_LIMIT):
    return pltpu.CompilerParams(dimension_semantics=sem, vmem_limit_bytes=vmem)


def _mod_kernel(c_ref, w_ref, b_ref, o_ref):
    o_ref[0] = _mm(_silu(c_ref[...]), w_ref[0]) + b_ref[0]


def _modulation(cond8, w_ada, b_ada):
    return pl.pallas_call(
        _mod_kernel,
        grid=(DEPTH, 3),
        in_specs=[
            pl.BlockSpec((8, D_MODEL), lambda l, j: (0, 0)),
            pl.BlockSpec((1, D_MODEL, D_MODEL), lambda l, j: (l, 0, j)),
            pl.BlockSpec((1, 1, D_MODEL), lambda l, j: (l, 0, j)),
        ],
        out_specs=pl.BlockSpec((1, 8, D_MODEL), lambda l, j: (l, 0, j)),
        out_shape=jax.ShapeDtypeStruct((DEPTH, 8, 3 * D_MODEL), F32),
        compiler_params=_cparams(("arbitrary", "arbitrary")),
        name="modulation",
    )(cond8, w_ada, b_ada.reshape(DEPTH, 1, 3 * D_MODEL))


def _front_kernel(x_ref, ss_ref, g_ref, w_ref, mu_ref, z_ref, h_scr, *, seq):
    j = pl.program_id(1)

    @pl.when(j == 0)
    def _():
        h = _rms(x_ref[...], g_ref[...]) * (1.0 + ss_ref[0, 1:2, :]) + ss_ref[0, 0:1, :]
        h_scr[...] = h.astype(BF16)

    cols = [slice(c, c + FRONT_COLS) for c in range(0, FRONT_TN, FRONT_COLS)]

    @pl.when(j < N_SHIFT_TILES)
    def _():
        first = lax.broadcasted_iota(jnp.int32, (8, FRONT_COLS), 0) == 0
        last = lax.broadcasted_iota(jnp.int32, (8, FRONT_COLS), 0) == 7
        for cs in cols:
            z = jnp.dot(h_scr[...], w_ref[:, cs], preferred_element_type=F32)
            hmu = 0.5 * mu_ref[:, cs]
            omu = 1.0 - mu_ref[:, cs]
            prev = pltpu.roll(z, 1, axis=0)
            nxt = pltpu.roll(z, FRONT_TM - 1, axis=0)
            z_ref[:, cs] = z * omu + (prev + nxt) * hmu
            for s0 in range(0, FRONT_TM, seq):
                a = slice(s0, s0 + 8)
                z_ref[a, cs] = z[a] * omu + (jnp.where(first, 0.0, prev[a]) + nxt[a]) * hmu
                b = slice(s0 + seq - 8, s0 + seq)
                z_ref[b, cs] = z[b] * omu + (prev[b] + jnp.where(last, 0.0, nxt[b])) * hmu

    @pl.when(j >= N_SHIFT_TILES)
    def _():
        for cs in cols:
            z_ref[:, cs] = jnp.dot(h_scr[...], w_ref[:, cs], preferred_element_type=F32)


def _front(x2d, ss, norm_g, w_in_p, mu_p, seq):
    n = x2d.shape[0]
    return pl.pallas_call(
        functools.partial(_front_kernel, seq=seq),
        grid=(n // FRONT_TM, ZW // FRONT_TN),
        in_specs=[
            pl.BlockSpec((FRONT_TM, D_MODEL), lambda i, j: (i, 0)),
            pl.BlockSpec((1, 2, D_MODEL), lambda i, j: (i, 0, 0)),
            pl.BlockSpec((1, D_MODEL), lambda i, j: (0, 0)),
            pl.BlockSpec((D_MODEL, FRONT_TN), lambda i, j: (0, j)),
            pl.BlockSpec((1, FRONT_TN), lambda i, j: (0, jnp.minimum(j, N_SHIFT_TILES - 1))),
        ],
        out_specs=pl.BlockSpec((FRONT_TM, FRONT_TN), lambda i, j: (i, j)),
        out_shape=jax.ShapeDtypeStruct((n, ZW), F32),
        scratch_shapes=[pltpu.VMEM((FRONT_TM, D_MODEL), BF16)],
        compiler_params=_cparams(("arbitrary", "arbitrary")),
        name="front",
    )(x2d, ss, norm_g, w_in_p, mu_p)


def _scan_chunks(dirs, s_ref):
    c = SCAN_C
    n2 = 2 * c
    ri = lax.broadcasted_iota(jnp.int32, (c, c), 0)
    ci = lax.broadcasted_iota(jnp.int32, (c, c), 1)
    row = lax.broadcasted_iota(jnp.int32, (n2, n2), 0)
    col = lax.broadcasted_iota(jnp.int32, (n2, n2), 1)
    same64 = (row >> 6) == (col >> 6)
    same16 = (row >> 4) == (col >> 4)
    tl = row & (c - 1)
    il = col & (c - 1)
    eye = (row == col).astype(F32)
    lane_a = lax.broadcasted_iota(jnp.int32, (c, LANES), 1) < N_A

    def stack2(x):
        return jnp.concatenate([jnp.where(lane_a, x, 0.0), jnp.where(lane_a, 0.0, x)], axis=0)

    lhs, rhs, v2, bk, gtot, strict, incl, key = [], [], [], [], [], [], [], []
    for r, kk, v, lw, kt, bv, rev, j, d in dirs:
        tri = ((ri <= ci) if rev else (ri >= ci)).astype(BF16)
        cs = _mm_exact_lhs01(tri, lw)
        tot = cs[0:1] if rev else cs[c - 1:c]
        g_tot = jnp.exp(tot)
        a_t = -kk * jnp.exp(cs - lw)
        r_t = r * jnp.exp(cs)
        g_inv = jnp.exp(-cs)
        b_t = bv * g_inv
        k_t = kt * g_inv
        g_rem = jnp.exp(tot - cs)
        b_h = bv * g_rem
        k_h = kt * g_rem
        st = same64 & ((il > tl) if rev else (il < tl))
        inc = same64 & ((il >= tl) if rev else (il <= tl))
        for p in range(H_A // 2):
            sl = slice(p * LANES, (p + 1) * LANES)
            lhs.append(jnp.concatenate([stack2(a_t[:, sl]), stack2(r_t[:, sl])], axis=0))
            rhs.append(jnp.concatenate([b_t[:, sl], k_t[:, sl]], axis=0))
            v2.append(stack2(v[:, sl]))
            bk.append(jnp.concatenate([stack2(b_h[:, sl]), stack2(k_h[:, sl])], axis=0))
            gtot.append(g_tot[:, sl])
            strict.append(st)
            incl.append(inc)
            key.append((j, d, p))
    ch = range(len(key))

    s = [s_ref[key[i]] for i in ch]
    big = [_mm_nt(lhs[i], jnp.concatenate([rhs[i], s[i]], axis=0)) for i in ch]
    lab, lak, mrbk = [], [], []
    for i in ch:
        x = big[i][:n2, :n2]
        xr = pltpu.roll(x, c, axis=1)
        y = big[i][n2:, :n2]
        yr = pltpu.roll(y, c, axis=1)
        pick = lambda top, bot: jnp.concatenate([top[:c], bot[c:]], axis=0)
        lab.append(jnp.where(strict[i], pick(x, xr), 0.0))
        lak.append(jnp.where(strict[i], pick(xr, x), 0.0))
        mrbk.append(jnp.concatenate([jnp.where(incl[i], pick(y, yr), 0.0),
                                     jnp.where(incl[i], pick(yr, y), 0.0)], axis=1))
    dg = [jnp.where(same16, lab[i], 0.0) for i in ch]
    off = [lab[i] - dg[i] for i in ch]
    pinv = [eye + dg[i] for i in ch]
    pw = [_mm(dg[i], dg[i]) for i in ch]
    for _ in range(2):
        t = [_mm(pw[i], jnp.concatenate([pw[i], pinv[i]], axis=1)) for i in ch]
        pinv = [pinv[i] + t[i][:, n2:] for i in ch]
        pw = [t[i][:, :n2] for i in ch]
    pinv = [pinv[i] + _mm(pw[i], pinv[i]) for i in ch]
    f = [_mm(pinv[i], off[i]) for i in ch]
    t = [_mm(f[i], jnp.concatenate([f[i], pinv[i]], axis=1)) for i in ch]
    g = [pinv[i] + t[i][:, n2:] for i in ch]
    tinv = [g[i] + _mm(t[i][:, :n2], g[i]) for i in ch]
    lv = [_mm(lak[i], v2[i]) for i in ch]
    u2 = [_mm(tinv[i], big[i][:n2, n2:] + lv[i]) for i in ch]
    uv = [jnp.concatenate([u2[i], v2[i]], axis=0) for i in ch]
    y2 = [big[i][n2:, n2:] + _mm(mrbk[i], uv[i]) for i in ch]
    for i in ch:
        s_ref[key[i]] = s[i] * gtot[i] + _mm(uv[i].T, bk[i])
    npair = H_A // 2
    return [jnp.concatenate([y2[g * npair + p][:c] + y2[g * npair + p][c:] for p in range(npair)], axis=1)
            for g in range(len(dirs))]


def _head_sums(x):
    lane_a = lax.broadcasted_iota(jnp.int32, (x.shape[0], LANES), 1) < N_A
    outs = []
    for p in range(H_A // 2):
        xs = x[:, p * LANES:(p + 1) * LANES]
        sa = jnp.sum(jnp.where(lane_a, xs, 0.0), axis=-1, keepdims=True)
        sb = jnp.sum(jnp.where(lane_a, 0.0, xs), axis=-1, keepdims=True)
        outs.append(jnp.where(lane_a, sa, sb))
    return jnp.concatenate(outs, axis=1)


def _rwkv_prepare(k, lora, d, w):
    wlt_ref, wla_ref, w0_ref, a0_ref, kk_ref, ka_ref, rk_ref = w
    hs = slice(d * BRANCH_W, (d + 1) * BRANCH_W)
    pre = w0_ref[:, hs] + _mm(jnp.tanh(lora), wlt_ref[:, hs])
    lw = -math.exp(-0.5) * jax.nn.sigmoid(pre)
    kkraw = k * kk_ref[...]
    kk = kkraw * lax.rsqrt(jnp.maximum(_head_sums(kkraw * kkraw), 1e-24))
    k_a = ka_ref[...]
    if d:
        a_b = jax.nn.sigmoid(a0_ref[:, hs] + _mm(lora, wla_ref[:, hs]))
        return kk, lw, k * (1.0 + (a_b - 1.0) * k_a), kk * a_b
    a2 = jax.nn.sigmoid(a0_ref[...] + _mm(lora, wla_ref[...]))
    ktf = k * (1.0 + (a2[:, :BRANCH_W] - 1.0) * k_a)
    ktb = k * (1.0 + (a2[:, BRANCH_W:] - 1.0) * k_a)
    return kk, lw, ktf, kk * a2[:, hs], (ktf + ktb) * rk_ref[...]


def _rwkv_kernel(*refs, has_init):
    if has_init:
        s0_ref, refs = refs[0], refs[1:]
    cur = (refs[0:5], refs[5:10])
    nxt = (refs[10:12], refs[12:14])
    w = refs[14:21]
    lng_ref, lnb_ref = refs[21:23]
    o_ref, sf_ref, sb_ref, s_scr, ysum_scr, kts_scr, p_scr = refs[23:30]
    c = pl.program_id(1)
    nc = pl.num_programs(1)

    @pl.when(c == 0)
    def _():
        if has_init:
            s_scr[...] = s0_ref[...]
        else:
            s_scr[...] = jnp.zeros_like(s_scr)

    @pl.when((c == 0) & (pl.program_id(0) == 0))
    def _():
        for j in range(SCAN_NB):
            for d in range(2):
                for i, a in enumerate(_rwkv_prepare(cur[d][3][j], cur[d][4][j], d, w)):
                    p_scr[j, d, i] = a

    dirs = []
    for j in range(SCAN_NB):
        for d in range(2):
            kk, lw, kt, bv = [p_scr[j, d, i] for i in range(4)]
            dirs.append((cur[d][0][j], kk, cur[d][1][j], lw, kt, bv, d == 1, j, d))
    kts = [p_scr[j, 0, 4] for j in range(SCAN_NB)]
    ys = _scan_chunks(dirs, s_scr)

    for j in range(SCAN_NB):
        for d in range(2):
            for i, a in enumerate(_rwkv_prepare(nxt[d][0][j], nxt[d][1][j], d, w)):
                p_scr[j, d, i] = a

    rows = (pl.multiple_of(c * SCAN_C, SCAN_C), pl.multiple_of((nc - 1 - c) * SCAN_C, SCAN_C))

    @pl.when(c < nc // 2)
    def _():
        for j in range(SCAN_NB):
            kts_scr[j, pl.ds(rows[0], SCAN_C), :] = kts[j]
            for d in range(2):
                ysum_scr[j, pl.ds(rows[d], SCAN_C), :] = ys[2 * j + d]

    @pl.when(c >= nc // 2)
    def _():
        for j in range(SCAN_NB):
            for d in range(2):
                r, v, g = cur[d][0][j], cur[d][1][j], cur[d][2][j]
                bonus = _head_sums(r * (kts_scr[j, pl.ds(rows[1], SCAN_C), :] if d else kts[j])) * v
                o = ysum_scr[j, pl.ds(rows[d], SCAN_C), :] + ys[2 * j + d]
                dlt = o - _head_sums(o) * (1.0 / N_A)
                var = _head_sums(dlt * dlt) * (1.0 / N_A)
                y = dlt * lax.rsqrt(var + RWKV_GN_EPS) * lng_ref[...] + lnb_ref[...] + bonus
                o_ref[j, pl.ds(rows[d], SCAN_C), :] = (y * _silu(g)).astype(BF16)

    @pl.when(c == nc - 1)
    def _():
        for j in range(SCAN_NB):
            for d, st_ref in enumerate((sf_ref, sb_ref)):
                for p in range(H_A // 2):
                    s = s_scr[j, d, p]
                    st_ref[j, 2 * p] = s[:N_A, :N_A]
                    st_ref[j, 2 * p + 1] = pltpu.roll(s, N_A, axis=1)[N_A:, :N_A]


def _rwkv(z, wp, s0_bd, nb, seq):
    n = z.shape[0]
    nc = seq // SCAN_C
    assert nc % 2 == 0 and nb % SCAN_NB == 0
    z3 = z.reshape(nb, seq, ZW)
    chunk = ((lambda c: c), (lambda c: nc - 1 - c))
    spec = lambda idx, col, w=BRANCH_W: pl.BlockSpec((SCAN_NB, SCAN_C, w), lambda b, c: (b, idx(c), col))
    in_specs = []
    for d in range(2):
        in_specs += [spec(chunk[d], COL_R), spec(chunk[d], COL_V), spec(chunk[d], COL_GA),
                     spec(chunk[d], COL_K), spec(chunk[d], COL_LORA, 256)]

    def spec_next(d, col, w=BRANCH_W):
        def index(b, c):
            wrap = c == nc - 1
            nb_i = jnp.minimum(b + wrap.astype(jnp.int32), nb // SCAN_NB - 1)
            return (nb_i, chunk[d](jnp.where(wrap, 0, c + 1)), col)
        return pl.BlockSpec((SCAN_NB, SCAN_C, w), index)

    for d in range(2):
        in_specs += [spec_next(d, COL_K), spec_next(d, COL_LORA, 256)]
    full = lambda a: pl.BlockSpec(a.shape, lambda b, c: (0,) * a.ndim)
    params = (wp["wlt"], wp["wla"], wp["w0"], wp["a0"], wp["k_k"], wp["k_a"], wp["r_k"], wp["ln_g"], wp["ln_b"])
    in_specs += [full(p) for p in params]
    args = [z3] * 14 + list(params)
    sspec = pl.BlockSpec((SCAN_NB, 2, H_A // 2, LANES, LANES), lambda b, c: (b, 0, 0, 0, 0))
    has_init = s0_bd is not None
    if has_init:
        in_specs = [sspec] + in_specs
        args = [s0_bd] + args
    fspec = pl.BlockSpec((SCAN_NB, H_A, N_A, N_A), lambda b, c: (b, 0, 0, 0))
    o_a, sf, sb = pl.pallas_call(
        functools.partial(_rwkv_kernel, has_init=has_init),
        grid=(nb // SCAN_NB, nc),
        in_specs=in_specs,
        out_specs=[pl.BlockSpec((SCAN_NB, seq, BRANCH_W), lambda b, c: (b, 0, 0)), fspec, fspec],
        out_shape=[jax.ShapeDtypeStruct((nb, seq, BRANCH_W), BF16),
                   jax.ShapeDtypeStruct((nb, H_A, N_A, N_A), F32),
                   jax.ShapeDtypeStruct((nb, H_A, N_A, N_A), F32)],
        scratch_shapes=[pltpu.VMEM((SCAN_NB, 2, H_A // 2, LANES, LANES), F32),
                        pltpu.VMEM((SCAN_NB, seq, BRANCH_W), F32),
                        pltpu.VMEM((SCAN_NB, seq, BRANCH_W), F32),
                        pltpu.VMEM((SCAN_NB, 2, 5, SCAN_C, BRANCH_W), F32)],
        compiler_params=_cparams(("arbitrary", "arbitrary")),
        name="rwkv",
    )(*args)
    return o_a.reshape(n, BRANCH_W), sf, sb


def _swap_pairs(x):
    w = x.shape[1]
    even = (lax.broadcasted_iota(jnp.int32, x.shape, 1) & 1) == 0
    return jnp.where(even, pltpu.roll(x, w - 1, axis=1), pltpu.roll(x, 1, axis=1))


def _mla_prep_kernel(*refs, rope):
    zb_ref, gq_ref, gkv_ref, wq_ref, wk_ref, wvt_ref, pk_ref = refs[:7]
    if rope:
        cq_ref, sq_ref, ck_ref, sk_ref = refs[7:11]
        refs = refs[11:]
    else:
        refs = refs[7:]
    q_o, k_o, vt_o, ckv_o, kr_o = refs
    zb = zb_ref[...]
    q = _mm(_rms(zb[:, :Q_LORA], gq_ref[...]), wq_ref[...])
    ckv = _rms(zb[:, Q_LORA:Q_LORA + KV_LORA], gkv_ref[...])
    kr = zb[:, Q_LORA + KV_LORA:]
    if rope:
        cq = jnp.concatenate([cq_ref[...]] * H_B, axis=1)
        sq = jnp.concatenate([sq_ref[...]] * H_B, axis=1)
        q = q * cq + _swap_pairs(q) * sq
        kr = kr * ck_ref[...] + _swap_pairs(kr) * sk_ref[...]
    q_o[...] = (q * Q_PRESCALE).astype(BF16)
    k_o[...] = (_mm(ckv, wk_ref[...]) + _mm(kr, pk_ref[...])).astype(BF16)
    vt_o[...] = _mm_nt(wvt_ref[...], ckv).astype(BF16)
    ckv_o[...] = ckv
    kr_o[...] = kr[:, :ROPE]


def _mla_prep(z, mp, rope_tabs, seq, tm=1024):
    n = z.shape[0]
    full = lambda a: pl.BlockSpec(a.shape, lambda i: (0,) * a.ndim)
    params = (mp["q_norm"], mp["kv_norm"], mp["wq"], mp["wk"], mp["wvt"], mp["pk"])
    in_specs = [pl.BlockSpec((tm, BRANCH_W), lambda i: (i, COL_ZB))] + [full(p) for p in params]
    args = [z, *params]
    rope = rope_tabs is not None
    if rope:
        per = seq // tm
        in_specs += [pl.BlockSpec((tm, LANES), lambda i: (i % per, 0))] * 4
        args += list(rope_tabs)
    rb = lambda w: pl.BlockSpec((tm, w), lambda i: (i, 0))
    return pl.pallas_call(
        functools.partial(_mla_prep_kernel, rope=rope),
        grid=(n // tm,),
        in_specs=in_specs,
        out_specs=[rb(H_B * LANES), rb(H_B * LANES), pl.BlockSpec((H_B * VDIM, tm), lambda i: (0, i)),
                   rb(KV_LORA), rb(ROPE)],
        out_shape=[jax.ShapeDtypeStruct((n, H_B * LANES), BF16),
                   jax.ShapeDtypeStruct((n, H_B * LANES), BF16),
                   jax.ShapeDtypeStruct((H_B * VDIM, n), BF16),
                   jax.ShapeDtypeStruct((n, KV_LORA), F32),
                   jax.ShapeDtypeStruct((n, ROPE), F32)],
        compiler_params=_cparams(("arbitrary",)),
        name="mla_prep",
    )(*args)


def _kv_expand_kernel(ckv_ref, kr_ref, wk_ref, wvt_ref, pk_ref, k_o, vt_o):
    ckv = ckv_ref[...]
    k_o[...] = (_mm(ckv, wk_ref[...]) + _mm(kr_ref[...], pk_ref[...])).astype(BF16)
    vt_o[...] = _mm_nt(wvt_ref[...], ckv).astype(BF16)


def _kv_expand(ckv, kr128, mp):
    m = ckv.shape[0]
    full = lambda a: pl.BlockSpec(a.shape, lambda i: (0,) * a.ndim)
    args = (ckv, kr128, mp["wk"], mp["wvt"], mp["pk"])
    return pl.pallas_call(
        _kv_expand_kernel,
        grid=(1,),
        in_specs=[full(a) for a in args],
        out_specs=[pl.BlockSpec((m, H_B * LANES), lambda i: (0, 0)),
                   pl.BlockSpec((H_B * VDIM, m), lambda i: (0, 0))],
        out_shape=[jax.ShapeDtypeStruct((m, H_B * LANES), BF16),
                   jax.ShapeDtypeStruct((H_B * VDIM, m), BF16)],
        compiler_params=_cparams(("arbitrary",)),
        name="mla_ctx_kv",
    )(*args)


def _attn_kernel(*refs, has_ctx):
    if has_ctx:
        q_ref, k_ref, vt_ref, kc_ref, vct_ref, g_ref, o_ref = refs
    else:
        q_ref, k_ref, vt_ref, g_ref, o_ref = refs
    nbs, seq = k_ref.shape[0], k_ref.shape[1]
    nt = (((1,), (1,)), ((), ()))
    for b in range(nbs):
        keys = slice(b * seq, (b + 1) * seq)
        outs = []
        for h0 in range(0, H_B, ATTN_GROUP):
            heads = range(h0, h0 + ATTN_GROUP)
            hs = [slice(h * LANES, (h + 1) * LANES) for h in heads]
            vh = [slice(h * VDIM, (h + 1) * VDIM) for h in heads]
            gi = range(ATTN_GROUP)
            s = [lax.dot_general(k_ref[b, :, hs[i]], q_ref[b, :, hs[i]], nt, preferred_element_type=F32)
                 for i in gi]
            m = [jnp.max(s[i], axis=0, keepdims=True) for i in gi]
            if has_ctx:
                sc = [lax.dot_general(kc_ref[:, hs[i]], q_ref[b, :, hs[i]], nt, preferred_element_type=F32)
                      for i in gi]
                m = [jnp.maximum(m[i], jnp.max(sc[i], axis=0, keepdims=True)) for i in gi]
            e = [jnp.exp2(s[i] - m[i]) for i in gi]
            den = [jnp.sum(e[i], axis=0, keepdims=True) for i in gi]
            o = [jnp.dot(vt_ref[vh[i], keys], e[i].astype(BF16), preferred_element_type=F32) for i in gi]
            if has_ctx:
                ec = [jnp.exp2(sc[i] - m[i]) for i in gi]
                den = [den[i] + jnp.sum(ec[i], axis=0, keepdims=True) for i in gi]
                o = [o[i] + jnp.dot(vct_ref[vh[i], :], ec[i].astype(BF16), preferred_element_type=F32)
                     for i in gi]
            outs += [o[i] / den[i] for i in gi]
        o_ref[b] = (jnp.concatenate(outs, axis=0).T * _silu(g_ref[b])).astype(BF16)


def _attention(z, q, k, vt, kc, vct, nb, seq, tq=256):
    n = z.shape[0]
    nq = seq // tq
    has_ctx = kc is not None
    nbs = 1 if has_ctx else max(1, min(nb, ATTN_ROWS // seq))
    z3 = z.reshape(nb, seq, ZW)
    in_specs = [pl.BlockSpec((nbs, tq, H_B * LANES), lambda b, i: (b, i, 0)),
                pl.BlockSpec((nbs, seq, H_B * LANES), lambda b, i: (b, 0, 0)),
                pl.BlockSpec((H_B * VDIM, nbs * seq), lambda b, i: (0, b))]
    args = [q.reshape(nb, seq, H_B * LANES), k.reshape(nb, seq, H_B * LANES), vt]
    if has_ctx:
        past = kc.shape[0] // nb
        in_specs += [pl.BlockSpec((past, H_B * LANES), lambda b, i: (b, 0)),
                     pl.BlockSpec((H_B * VDIM, past), lambda b, i: (0, b))]
        args += [kc, vct]
    in_specs.append(pl.BlockSpec((nbs, tq, BRANCH_W), lambda b, i: (b, i, COL_GB)))
    args.append(z3)
    out = pl.pallas_call(
        functools.partial(_attn_kernel, has_ctx=has_ctx),
        grid=(nb // nbs, nq),
        in_specs=in_specs,
        out_specs=pl.BlockSpec((nbs, tq, BRANCH_W), lambda b, i: (b, i, 0)),
        out_shape=jax.ShapeDtypeStruct((nb, seq, BRANCH_W), BF16),
        compiler_params=_cparams(("arbitrary", "arbitrary")),
        name="mla_attention",
    )(*args)
    return out.reshape(n, BRANCH_W)


def _gmlp_kernel(u_ref, vc_ref, g_ref, lng_ref, lnb_ref, ws_ref, bs_ref, o_ref):
    tm = u_ref.shape[0]
    u = jax.nn.gelu(u_ref[...])
    x = jax.nn.gelu(vc_ref[...])
    mu = jnp.mean(x, axis=-1, keepdims=True)
    xc = x - mu
    var = jnp.mean(xc * xc, axis=-1, keepdims=True)
    vn = (xc * lax.rsqrt(var + 1e-5) * lng_ref[...] + lnb_ref[...]).astype(BF16)
    rows = []
    for c in range(tm // CHUNK):
        cols = []
        for g in range(G_C):
            blk = vn[c * CHUNK:(c + 1) * CHUNK, g * LANES:(g + 1) * LANES]
            cols.append(jnp.dot(ws_ref[g], blk, preferred_element_type=F32))
        rows.append(jnp.concatenate(cols, axis=1) + bs_ref[...])
    mixed = jnp.concatenate(rows, axis=0)
    o_ref[...] = (u * mixed * _silu(g_ref[...])).astype(BF16)


def _gmlp(z, gp, tm=512):
    n = z.shape[0]
    zb = lambda c: pl.BlockSpec((tm, BRANCH_W), lambda i, c=c: (i, c))
    full = lambda a: pl.BlockSpec(a.shape, lambda i: (0,) * a.ndim)
    params = (gp["ln_g"], gp["ln_b"], gp["w_s"], gp["b_s"])
    return pl.pallas_call(
        _gmlp_kernel,
        grid=(n // tm,),
        in_specs=[zb(COL_U), zb(COL_VC), zb(COL_GC)] + [full(p) for p in params],
        out_specs=pl.BlockSpec((tm, BRANCH_W), lambda i: (i, 0)),
        out_shape=jax.ShapeDtypeStruct((n, BRANCH_W), BF16),
        compiler_params=_cparams(("arbitrary",)),
        name="gmlp",
    )(z, z, z, *params)


def _fnet_kernel(f_ref, g_ref, cs_ref, dft_ref, o_ref, xcs_scr, *, seq):
    r = pl.program_id(1)
    nbs = f_ref.shape[0]

    @pl.when(r == 0)
    def _():
        for b in range(nbs):
            for g in range(G_D):
                xg = _mm(f_ref[b, :, g * LANES:(g + 1) * LANES], cs_ref[...])
                xcs_scr[b, 0:seq, g * LANES:(g + 1) * LANES] = xg[:, :LANES].astype(BF16)
                xcs_scr[b, seq:2 * seq, g * LANES:(g + 1) * LANES] = xg[:, LANES:].astype(BF16)

    for b in range(nbs):
        y = jnp.dot(dft_ref[...], xcs_scr[b], preferred_element_type=F32)
        o_ref[b] = (y * (1.0 / math.sqrt(seq * LANES)) * _silu(g_ref[b])).astype(BF16)


def _fnet(z, cs128, dft, nb, seq, tr=256):
    n = z.shape[0]
    nr = seq // tr
    nbs = max(1, min(nb, FRONT_TM // seq))
    z3 = z.reshape(nb, seq, ZW)
    out = pl.pallas_call(
        functools.partial(_fnet_kernel, seq=seq),
        grid=(nb // nbs, nr),
        in_specs=[pl.BlockSpec((nbs, seq, BRANCH_W), lambda b, r: (b, 0, COL_F)),
                  pl.BlockSpec((nbs, tr, BRANCH_W), lambda b, r: (b, r, COL_GD)),
                  pl.BlockSpec((LANES, 2 * LANES), lambda b, r: (0, 0)),
                  pl.BlockSpec((tr, 2 * seq), lambda b, r: (r, 0))],
        out_specs=pl.BlockSpec((nbs, tr, BRANCH_W), lambda b, r: (b, r, 0)),
        out_shape=jax.ShapeDtypeStruct((nb, seq, BRANCH_W), BF16),
        scratch_shapes=[pltpu.VMEM((nbs, 2 * seq, BRANCH_W), BF16)],
        compiler_params=_cparams(("arbitrary", "arbitrary")),
        name="fnet",
    )(z3, z3, cs128, dft)
    return out.reshape(n, BRANCH_W)


def _back_kernel(x_ref, ss_ref, g_ref, oa_ref, ob_ref, oc_ref, od_ref, wm_ref, bm_ref,
                 wb_ref, wo_ref, fg_ref, y_ref, h_scr, acc_scr, *, final):
    nb = pl.program_id(1)
    tm = x_ref.shape[0]
    chunks = [pl.ds(r, BACK_ROWS) for r in range(0, tm, BACK_ROWS)]

    @pl.when(nb == 0)
    def _():
        h = _rms(x_ref[...], g_ref[...]) * (1.0 + ss_ref[0, 1:2, :]) + ss_ref[0, 0:1, :]
        h_scr[...] = h.astype(BF16)

    for i, o_ref in enumerate((oa_ref, ob_ref, oc_ref, od_ref)):
        @pl.when(nb == i)
        def _(o_ref=o_ref, i=i):
            for rows in chunks:
                gates = jax.nn.sigmoid(jnp.dot(h_scr[rows, :], wm_ref[...], preferred_element_type=F32)
                                       + bm_ref[...])
                upd = gates * jnp.dot(o_ref[rows, :], wb_ref[0], preferred_element_type=F32)
                if i == 0:
                    acc_scr[rows, :] = upd
                else:
                    acc_scr[rows, :] += upd

    @pl.when(nb == 3)
    def _():
        for rows in chunks:
            xn = x_ref[rows, :] + ss_ref[0, 2:3, :] * _mm(acc_scr[rows, :], wo_ref[...])
            if final:
                xn = _rms(xn, fg_ref[...])
            y_ref[rows, :] = xn


def _back(x2d, ss, norm_g, o_a, o_b, o_c, o_d, lw, final_g, final, tm=1024):
    n = x2d.shape[0]
    rows_per_ss = FRONT_TM // tm
    ob = pl.BlockSpec((tm, BRANCH_W), lambda i, j: (i, 0))
    return pl.pallas_call(
        functools.partial(_back_kernel, final=final),
        grid=(n // tm, 4),
        in_specs=[pl.BlockSpec((tm, D_MODEL), lambda i, j: (i, 0)),
                  pl.BlockSpec((1, 3, D_MODEL), lambda i, j: (i // rows_per_ss, 0, 0)),
                  pl.BlockSpec((1, D_MODEL), lambda i, j: (0, 0)),
                  ob, ob, ob, ob,
                  pl.BlockSpec((D_MODEL, D_MODEL), lambda i, j: (0, j)),
                  pl.BlockSpec((1, D_MODEL), lambda i, j: (0, j)),
                  pl.BlockSpec((1, BRANCH_W, D_MODEL), lambda i, j: (j, 0, 0)),
                  pl.BlockSpec((D_MODEL, D_MODEL), lambda i, j: (0, 0)),
                  pl.BlockSpec((1, D_MODEL), lambda i, j: (0, 0))],
        out_specs=pl.BlockSpec((tm, D_MODEL), lambda i, j: (i, 0)),
        out_shape=jax.ShapeDtypeStruct((n, D_MODEL), F32),
        scratch_shapes=[pltpu.VMEM((tm, D_MODEL), BF16), pltpu.VMEM((tm, D_MODEL), F32)],
        compiler_params=_cparams(("arbitrary", "arbitrary")),
        name="back",
    )(x2d, ss, norm_g, o_a, o_b, o_c, o_d, lw["w_merge"], lw["b_merge"], lw["w_branch"],
      lw["w_out"], final_g)


def _pad_cols(w):
    z = lambda k: jnp.zeros(w.shape[:-1] + (k,), w.dtype)
    return jnp.concatenate([w[..., :SHIFT_W], z(Z_SHIFT_PAD - SHIFT_W), w[..., SHIFT_W:2240],
                            w[..., 2240:2656], z(3072 - 2560 - 416), w[..., 2656:]], axis=-1)


def _rope_place():
    p = np.zeros((LANES, H_B * LANES), np.float32)
    for h in range(H_B):
        for j in range(ROPE):
            p[j, h * LANES + NOPE + j] = 1.0
    return jnp.asarray(p, BF16)


def _pack_q(w):
    w = w.reshape(Q_LORA, H_B, NOPE + ROPE)
    return jnp.pad(w, ((0, 0), (0, 0), (0, LANES - NOPE - ROPE))).reshape(Q_LORA, H_B * LANES)


def _pack_kv(w):
    w = w.reshape(KV_LORA, H_B, NOPE + VDIM)
    wk = jnp.pad(w[..., :NOPE], ((0, 0), (0, 0), (0, LANES - NOPE))).reshape(KV_LORA, H_B * LANES)
    return wk, w[..., NOPE:].reshape(KV_LORA, H_B * VDIM)


def _rope_tables(n_tokens):
    rows = n_tokens // GRID_W
    row = jnp.repeat(jnp.arange(rows, dtype=F32), GRID_W)
    col = jnp.tile(jnp.arange(GRID_W, dtype=F32), rows)
    n_freq = ROPE // 4
    inv = ROPE_BASE ** (-jnp.arange(n_freq, dtype=F32) / n_freq)
    ang = jnp.concatenate([row[:, None] * inv, col[:, None] * inv], axis=-1)
    cos = jnp.repeat(jnp.cos(ang), 2, axis=-1)
    sin = jnp.repeat(jnp.sin(ang), 2, axis=-1) * jnp.tile(jnp.asarray([-1.0, 1.0], F32), ROPE // 2)
    ones = lambda k: jnp.ones((n_tokens, k), F32)
    zeros = lambda k: jnp.zeros((n_tokens, k), F32)
    cq = jnp.concatenate([ones(NOPE), cos, ones(LANES - NOPE - ROPE)], axis=-1)
    sq = jnp.concatenate([zeros(NOPE), sin, zeros(LANES - NOPE - ROPE)], axis=-1)
    ck = jnp.concatenate([cos, ones(LANES - ROPE)], axis=-1)
    sk = jnp.concatenate([sin, zeros(LANES - ROPE)], axis=-1)
    return cq, sq, ck, sk


def _dft_tables(seq):
    k = np.arange(LANES)
    a = 2.0 * np.pi * ((k[:, None] * k[None, :]) % LANES) / LANES
    cs128 = np.concatenate([np.cos(a), np.sin(a)], axis=1)
    t = np.arange(seq)
    b = 2.0 * np.pi * ((t[:, None] * t[None, :]) % seq) / seq
    dft = np.concatenate([np.cos(b), -np.sin(b)], axis=1)
    return (jnp.asarray(cs128.astype(np.float32)).astype(BF16),
            jnp.asarray(dft.astype(np.float32)).astype(BF16))


def _lora_weights(w_up, a_up):
    z = jnp.zeros((LORA, BRANCH_W), F32)
    pad = jnp.zeros((256 - 3 * LORA, 2 * BRANCH_W), F32)
    wlt = jnp.concatenate([jnp.concatenate([w_up[0], z], 1), jnp.concatenate([z, w_up[1]], 1),
                           jnp.concatenate([z, z], 1), pad], axis=0)
    wla = jnp.concatenate([jnp.concatenate([z, z], 1), jnp.concatenate([z, z], 1),
                           jnp.concatenate([a_up[0], a_up[1]], 1), pad], axis=0)
    return wlt.astype(BF16), wla.astype(BF16)


def _to_blockdiag(s):
    b = s.shape[0]
    s = s.reshape(b, H_A // 2, 2, N_A, N_A)
    z = jnp.zeros_like(s[:, :, 0])
    top = jnp.concatenate([s[:, :, 0], z], axis=-1)
    bot = jnp.concatenate([z, s[:, :, 1]], axis=-1)
    return jnp.concatenate([top, bot], axis=-2)


def _trunk_layer(x2d, ss, lw, nb, seq, rope_tabs, ctx, final_g, final):
    z = _front(x2d, ss[:, :2], lw["norm_g"], lw["w_in"], lw["mu"], seq)
    s0 = None
    if ctx is not None:
        s0 = jnp.stack([_to_blockdiag(ctx[0]), _to_blockdiag(ctx[1])], axis=1)
    o_a, sf, sb = _rwkv(z, lw["rwkv"], s0, nb, seq)
    q, k, v, ckv, kr = _mla_prep(z, lw["mla"], rope_tabs, seq)
    kc = vc = None
    if ctx is not None:
        past = ctx[2].shape[1]
        kr_c = jnp.pad(ctx[3].reshape(nb * past, ROPE), ((0, 0), (0, LANES - ROPE)))
        kc, vc = _kv_expand(ctx[2].reshape(nb * past, KV_LORA), kr_c, lw["mla"])
    o_b = _attention(z, q, k, v, kc, vc, nb, seq)
    o_c = _gmlp(z, lw["gmlp"])
    o_d = _fnet(z, *lw["dft"][seq], nb, seq)
    x_new = _back(x2d, ss, lw["norm_g"], o_a, o_b, o_c, o_d, lw, final_g, final)
    return x_new, sf, sb, ckv, kr


def kernel(x_prompt, x_sample, state_rwkv_fwd, state_rwkv_bwd, cache_mla_ckv, cache_mla_krope, c, c_ctx, norm_g, w_ada, b_ada, w_in, shift_mu, rwkv_w0, rwkv_w_up, rwkv_a0, rwkv_a_up, rwkv_k_k, rwkv_k_a, rwkv_r_k, rwkv_ln_g, rwkv_ln_b, mla_q_norm, mla_w_q_up, mla_kv_norm, mla_w_kv_up, gmlp_ln_g, gmlp_ln_b, gmlp_w_s, gmlp_b_s, w_branch, w_merge, b_merge, w_out, final_norm_g):
    nb_c, seq_c, _ = x_prompt.shape
    nb_l, seq_l, _ = x_sample.shape
    assert (nb_c * seq_c) % FRONT_TM == 0 and FRONT_TM % seq_c == 0 and seq_l == FRONT_TM

    cond8 = jnp.concatenate([c_ctx[None], c, jnp.zeros((8 - 1 - nb_l, D_MODEL), F32)], axis=0)
    mod = _modulation(cond8, w_ada, b_ada).reshape(DEPTH, 8, 3, D_MODEL)
    n_ctx_tiles = nb_c * seq_c // FRONT_TM

    pk = _rope_place()
    rope_tabs = _rope_tables(seq_l)
    dft = {s: _dft_tables(s) for s in {seq_c, seq_l}}
    mu_p = jnp.pad(shift_mu, ((0, 0), (0, Z_SHIFT_PAD - SHIFT_W)))[:, None]
    final_g = final_norm_g[None]

    xc = x_prompt.reshape(nb_c * seq_c, D_MODEL)
    xl = x_sample.reshape(nb_l * seq_l, D_MODEL)
    sf_list, sb_list, ckv_list, kr_list = [], [], [], []
    for l in range(DEPTH):
        wlt, wla = _lora_weights(rwkv_w_up[l], rwkv_a_up[l])
        wk, wv = _pack_kv(mla_w_kv_up[l])
        lw = {
            "norm_g": norm_g[l][None], "w_in": _pad_cols(w_in[l]).astype(BF16), "mu": mu_p[l],
            "rwkv": {
                "wlt": wlt, "wla": wla, "w0": rwkv_w0[l].reshape(1, 2 * BRANCH_W),
                "a0": rwkv_a0[l].reshape(1, 2 * BRANCH_W), "k_k": rwkv_k_k[l][None],
                "k_a": rwkv_k_a[l][None], "r_k": rwkv_r_k[l].reshape(1, BRANCH_W),
                "ln_g": rwkv_ln_g[l][None], "ln_b": rwkv_ln_b[l][None],
            },
            "mla": {
                "q_norm": mla_q_norm[l][None], "kv_norm": mla_kv_norm[l][None],
                "wq": _pack_q(mla_w_q_up[l]).astype(BF16), "wk": wk.astype(BF16),
                "wvt": wv.T.astype(BF16), "pk": pk,
            },
            "gmlp": {
                "ln_g": gmlp_ln_g[l][None], "ln_b": gmlp_ln_b[l][None],
                "w_s": gmlp_w_s[l].astype(BF16),
                "b_s": jnp.repeat(gmlp_b_s[l].T, BRANCH_W // G_C, axis=1),
            },
            "dft": dft,
            "w_merge": w_merge[l].astype(BF16), "b_merge": b_merge[l][None],
            "w_branch": w_branch[l].astype(BF16), "w_out": w_out[l].astype(BF16),
        }
        final = l == DEPTH - 1
        ss_c = jnp.broadcast_to(mod[l, 0][None], (n_ctx_tiles, 3, D_MODEL))
        ss_l = mod[l, 1:1 + nb_l]
        xc, sf, sb, ckv, kr = _trunk_layer(xc, ss_c, lw, nb_c, seq_c, None, None, final_g, final)
        sf_list.append(sf)
        sb_list.append(sb)
        ckv_list.append(ckv.reshape(nb_c, seq_c, KV_LORA))
        kr_list.append(kr.reshape(nb_c, seq_c, ROPE))
        ctx = (state_rwkv_fwd[:, l], state_rwkv_bwd[:, l], cache_mla_ckv[:, l], cache_mla_krope[:, l])
        xl = _trunk_layer(xl, ss_l, lw, nb_l, seq_l, rope_tabs, ctx, final_g, final)[0]

    return (xc.reshape(nb_c, seq_c, D_MODEL), xl.reshape(nb_l, seq_l, D_MODEL),
            jnp.stack(sf_list, axis=1), jnp.stack(sb_list, axis=1),
            jnp.stack(ckv_list, axis=1), jnp.stack(kr_list, axis=1))
```

```python
import functools
import math

import jax
import jax.numpy as jnp
import numpy as np
from jax import lax
from jax.experimental import pallas as pl
from jax.experimental.pallas import tpu as pltpu

F32 = jnp.float32
BF16 = jnp.bfloat16

D_MODEL = 1024
DEPTH = 2
GRID_W = 64
BRANCH_W = 512
H_A = 8
N_A = 64
LORA = 64
RWKV_GN_EPS = 64e-5
H_B = 8
NOPE = 64
ROPE = 32
VDIM = 64
Q_LORA = 256
KV_LORA = 128
ROPE_BASE = 10000.0
Q_PRESCALE = (NOPE + ROPE) ** -0.5 * math.log2(math.e)
G_C = 4
CHUNK = 128
G_D = 4
NORM_EPS = 1e-6
SHIFT_W = 1728

LANES = 128
VMEM_LIMIT = 52 * 1024 * 1024

ZW = 6144
Z_SHIFT_PAD = 2048
COL_R, COL_K, COL_V = 0, 1, 2
COL_LORA = 6
COL_GA, COL_ZB, COL_GB, COL_U, COL_VC, COL_GC, COL_F, COL_GD = 4, 5, 6, 7, 8, 9, 10, 11
FRONT_TM = 2048
FRONT_TN = 1024
FRONT_COLS = 256
N_SHIFT_TILES = Z_SHIFT_PAD // FRONT_TN
SCAN_C = 64
SCAN_NB = 2
ATTN_GROUP = 4
ATTN_TQ = 512
FNET_ROWS = 1024
ATTN_ROWS = 1024
BACK_ROWS = 256


def _mm(a, b):
    return jnp.dot(a.astype(BF16), b.astype(BF16), preferred_element_type=F32)


def _mm_nt(a, b):
    return lax.dot_general(a.astype(BF16), b.astype(BF16), (((1,), (1,)), ((), ())),
                           preferred_element_type=F32)


def _split3(x):
    hi = x.astype(BF16)
    r1 = x - hi.astype(F32)
    mid = r1.astype(BF16)
    lo = (r1 - mid.astype(F32)).astype(BF16)
    return hi, mid, lo


def _mm_exact_lhs01(m01, x):
    hi, mid, lo = _split3(x)
    d = functools.partial(jnp.dot, preferred_element_type=F32)
    return d(m01, hi) + d(m01, mid) + d(m01, lo)


def _silu(x):
    return x * jax.nn.sigmoid(x)


def _rms(x, g):
    return x * lax.rsqrt(jnp.mean(x * x, axis=-1, keepdims=True) + NORM_EPS) * g


def _cparams(sem, vmem=VMEM_LIMIT):
    return pltpu.CompilerParams(dimension_semantics=sem, vmem_limit_bytes=vmem)


def _mod_kernel(c_ref, w_ref, b_ref, o_ref):
    o_ref[0] = _mm(_silu(c_ref[...]), w_ref[0]) + b_ref[0]


def _modulation(cond8, w_ada, b_ada):
    return pl.pallas_call(
        _mod_kernel,
        grid=(DEPTH, 3),
        in_specs=[
            pl.BlockSpec((8, D_MODEL), lambda l, j: (0, 0)),
            pl.BlockSpec((1, D_MODEL, D_MODEL), lambda l, j: (l, 0, j)),
            pl.BlockSpec((1, 1, D_MODEL), lambda l, j: (l, 0, j)),
        ],
        out_specs=pl.BlockSpec((1, 8, D_MODEL), lambda l, j: (l, 0, j)),
        out_shape=jax.ShapeDtypeStruct((DEPTH, 8, 3 * D_MODEL), F32),
        compiler_params=_cparams(("arbitrary", "arbitrary")),
        name="modulation",
    )(cond8, w_ada, b_ada.reshape(DEPTH, 1, 3 * D_MODEL))


def _front_kernel(x_ref, ss_ref, g_ref, w_ref, mu_ref, z_ref, h_ref, *, seq):
    j = pl.program_id(1)

    @pl.when(j == 0)
    def _():
        h = _rms(x_ref[...], g_ref[...]) * (1.0 + ss_ref[0, 1:2, :]) + ss_ref[0, 0:1, :]
        h_ref[...] = h.astype(BF16)

    cols = [slice(c, c + FRONT_COLS) for c in range(0, FRONT_TN, FRONT_COLS)]

    @pl.when(j < N_SHIFT_TILES)
    def _():
        first = lax.broadcasted_iota(jnp.int32, (8, FRONT_COLS), 0) == 0
        last = lax.broadcasted_iota(jnp.int32, (8, FRONT_COLS), 0) == 7
        for cs in cols:
            z = jnp.dot(h_ref[...], w_ref[:, cs], preferred_element_type=F32)
            hmu = 0.5 * mu_ref[:, cs]
            omu = 1.0 - mu_ref[:, cs]
            prev = pltpu.roll(z, 1, axis=0)
            nxt = pltpu.roll(z, FRONT_TM - 1, axis=0)
            z_ref[:, cs] = z * omu + (prev + nxt) * hmu
            for s0 in range(0, FRONT_TM, seq):
                a = slice(s0, s0 + 8)
                z_ref[a, cs] = z[a] * omu + (jnp.where(first, 0.0, prev[a]) + nxt[a]) * hmu
                b = slice(s0 + seq - 8, s0 + seq)
                z_ref[b, cs] = z[b] * omu + (prev[b] + jnp.where(last, 0.0, nxt[b])) * hmu

    @pl.when(j >= N_SHIFT_TILES)
    def _():
        for cs in cols:
            z_ref[:, cs] = jnp.dot(h_ref[...], w_ref[:, cs], preferred_element_type=F32)


def _front(x2d, ss, norm_g, w_in_p, mu_p, seq):
    n = x2d.shape[0]
    return pl.pallas_call(
        functools.partial(_front_kernel, seq=seq),
        grid=(n // FRONT_TM, ZW // FRONT_TN),
        in_specs=[
            pl.BlockSpec((FRONT_TM, D_MODEL), lambda i, j: (i, 0)),
            pl.BlockSpec((1, 2, D_MODEL), lambda i, j: (i, 0, 0)),
            pl.BlockSpec((1, D_MODEL), lambda i, j: (0, 0)),
            pl.BlockSpec((D_MODEL, FRONT_TN), lambda i, j: (0, j)),
            pl.BlockSpec((1, FRONT_TN), lambda i, j: (0, jnp.minimum(j, N_SHIFT_TILES - 1))),
        ],
        out_specs=[pl.BlockSpec((FRONT_TM, FRONT_TN), lambda i, j: (i, j)),
                   pl.BlockSpec((FRONT_TM, D_MODEL), lambda i, j: (i, 0))],
        out_shape=[jax.ShapeDtypeStruct((n, ZW), F32), jax.ShapeDtypeStruct((n, D_MODEL), BF16)],
        compiler_params=_cparams(("arbitrary", "arbitrary")),
        name="front",
    )(x2d, ss, norm_g, w_in_p, mu_p)


def _scan_chunks(dirs, s_ref):
    c = SCAN_C
    n2 = 2 * c
    ri = lax.broadcasted_iota(jnp.int32, (c, c), 0)
    ci = lax.broadcasted_iota(jnp.int32, (c, c), 1)
    row = lax.broadcasted_iota(jnp.int32, (n2, n2), 0)
    col = lax.broadcasted_iota(jnp.int32, (n2, n2), 1)
    same64 = (row >> 6) == (col >> 6)
    same16 = (row >> 4) == (col >> 4)
    tl = row & (c - 1)
    il = col & (c - 1)
    eye = (row == col).astype(F32)
    lane_a = lax.broadcasted_iota(jnp.int32, (c, LANES), 1) < N_A

    def stack2(x):
        return jnp.concatenate([jnp.where(lane_a, x, 0.0), jnp.where(lane_a, 0.0, x)], axis=0)

    lhs, rhs, v2, bk, gtot, strict, incl, key = [], [], [], [], [], [], [], []
    for r, kk, v, lw, kt, bv, rev, j, d in dirs:
        tri = ((ri <= ci) if rev else (ri >= ci)).astype(BF16)
        cs = _mm_exact_lhs01(tri, lw)
        tot = cs[0:1] if rev else cs[c - 1:c]
        g_tot = jnp.exp(tot)
        a_t = -kk * jnp.exp(cs - lw)
        r_t = r * jnp.exp(cs)
        g_inv = jnp.exp(-cs)
        b_t = bv * g_inv
        k_t = kt * g_inv
        g_rem = jnp.exp(tot - cs)
        b_h = bv * g_rem
        k_h = kt * g_rem
        st = same64 & ((il > tl) if rev else (il < tl))
        inc = same64 & ((il >= tl) if rev else (il <= tl))
        for p in range(H_A // 2):
            sl = slice(p * LANES, (p + 1) * LANES)
            lhs.append(jnp.concatenate([stack2(a_t[:, sl]), stack2(r_t[:, sl])], axis=0))
            rhs.append(jnp.concatenate([b_t[:, sl], k_t[:, sl]], axis=0))
            v2.append(stack2(v[:, sl]))
            bk.append(jnp.concatenate([stack2(b_h[:, sl]), stack2(k_h[:, sl])], axis=0))
            gtot.append(g_tot[:, sl])
            strict.append(st)
            incl.append(inc)
            key.append((j, d, p))
    ch = range(len(key))

    s = [s_ref[key[i]] for i in ch]
    big = [_mm_nt(lhs[i], jnp.concatenate([rhs[i], s[i]], axis=0)) for i in ch]
    lab, lak, mrbk = [], [], []
    for i in ch:
        x = big[i][:n2, :n2]
        xr = pltpu.roll(x, c, axis=1)
        y = big[i][n2:, :n2]
        yr = pltpu.roll(y, c, axis=1)
        pick = lambda top, bot: jnp.concatenate([top[:c], bot[c:]], axis=0)
        lab.append(jnp.where(strict[i], pick(x, xr), 0.0))
        lak.append(jnp.where(strict[i], pick(xr, x), 0.0))
        mrbk.append(jnp.concatenate([jnp.where(incl[i], pick(y, yr), 0.0),
                                     jnp.where(incl[i], pick(yr, y), 0.0)], axis=1))
    dg = [jnp.where(same16, lab[i], 0.0) for i in ch]
    off = [lab[i] - dg[i] for i in ch]
    pinv = [eye + dg[i] for i in ch]
    pw = [_mm(dg[i], dg[i]) for i in ch]
    for _ in range(2):
        t = [_mm(pw[i], jnp.concatenate([pw[i], pinv[i]], axis=1)) for i in ch]
        pinv = [pinv[i] + t[i][:, n2:] for i in ch]
        pw = [t[i][:, :n2] for i in ch]
    pinv = [pinv[i] + _mm(pw[i], pinv[i]) for i in ch]
    f = [_mm(pinv[i], off[i]) for i in ch]
    t = [_mm(f[i], jnp.concatenate([f[i], pinv[i]], axis=1)) for i in ch]
    g = [pinv[i] + t[i][:, n2:] for i in ch]
    tinv = [g[i] + _mm(t[i][:, :n2], g[i]) for i in ch]
    lv = [_mm(lak[i], v2[i]) for i in ch]
    u2 = [_mm(tinv[i], big[i][:n2, n2:] + lv[i]) for i in ch]
    uv = [jnp.concatenate([u2[i], v2[i]], axis=0) for i in ch]
    y2 = [big[i][n2:, n2:] + _mm(mrbk[i], uv[i]) for i in ch]
    for i in ch:
        s_ref[key[i]] = s[i] * gtot[i] + _mm(uv[i].T, bk[i])
    npair = H_A // 2
    return [jnp.concatenate([y2[g * npair + p][:c] + y2[g * npair + p][c:] for p in range(npair)], axis=1)
            for g in range(len(dirs))]


def _head_sums(x):
    lane_a = lax.broadcasted_iota(jnp.int32, (x.shape[0], LANES), 1) < N_A
    outs = []
    for p in range(H_A // 2):
        xs = x[:, p * LANES:(p + 1) * LANES]
        sa = jnp.sum(jnp.where(lane_a, xs, 0.0), axis=-1, keepdims=True)
        sb = jnp.sum(jnp.where(lane_a, 0.0, xs), axis=-1, keepdims=True)
        outs.append(jnp.where(lane_a, sa, sb))
    return jnp.concatenate(outs, axis=1)


def _rwkv_prepare(k, lora, d, w):
    wlt_ref, wla_ref, w0_ref, a0_ref, kk_ref, ka_ref, rk_ref = w
    hs = slice(d * BRANCH_W, (d + 1) * BRANCH_W)
    pre = w0_ref[:, hs] + _mm(jnp.tanh(lora), wlt_ref[:, hs])
    lw = -math.exp(-0.5) * jax.nn.sigmoid(pre)
    kkraw = k * kk_ref[...]
    kk = kkraw * lax.rsqrt(jnp.maximum(_head_sums(kkraw * kkraw), 1e-24))
    k_a = ka_ref[...]
    if d:
        a_b = jax.nn.sigmoid(a0_ref[:, hs] + _mm(lora, wla_ref[:, hs]))
        return kk, lw, k * (1.0 + (a_b - 1.0) * k_a), kk * a_b
    a2 = jax.nn.sigmoid(a0_ref[...] + _mm(lora, wla_ref[...]))
    ktf = k * (1.0 + (a2[:, :BRANCH_W] - 1.0) * k_a)
    ktb = k * (1.0 + (a2[:, BRANCH_W:] - 1.0) * k_a)
    return kk, lw, ktf, kk * a2[:, hs], (ktf + ktb) * rk_ref[...]


def _rwkv_kernel(*refs, has_init):
    if has_init:
        s0_ref, refs = refs[0], refs[1:]
    cur = (refs[0:5], refs[5:10])
    nxt = (refs[10:12], refs[12:14])
    w = refs[14:21]
    lng_ref, lnb_ref = refs[21:23]
    o_ref, sf_ref, sb_ref, s_scr, ysum_scr, kts_scr, p_scr = refs[23:30]
    c = pl.program_id(1)
    nc = pl.num_programs(1)

    @pl.when(c == 0)
    def _():
        if has_init:
            s_scr[...] = s0_ref[...]
        else:
            s_scr[...] = jnp.zeros_like(s_scr)

    @pl.when((c == 0) & (pl.program_id(0) == 0))
    def _():
        for j in range(SCAN_NB):
            for d in range(2):
                for i, a in enumerate(_rwkv_prepare(cur[d][3][j], cur[d][4][j], d, w)):
                    p_scr[j, d, i] = a

    dirs = []
    for j in range(SCAN_NB):
        for d in range(2):
            kk, lw, kt, bv = [p_scr[j, d, i] for i in range(4)]
            dirs.append((cur[d][0][j], kk, cur[d][1][j], lw, kt, bv, d == 1, j, d))
    kts = [p_scr[j, 0, 4] for j in range(SCAN_NB)]
    ys = _scan_chunks(dirs, s_scr)

    for j in range(SCAN_NB):
        for d in range(2):
            for i, a in enumerate(_rwkv_prepare(nxt[d][0][j], nxt[d][1][j], d, w)):
                p_scr[j, d, i] = a

    rows = (pl.multiple_of(c * SCAN_C, SCAN_C), pl.multiple_of((nc - 1 - c) * SCAN_C, SCAN_C))

    @pl.when(c < nc // 2)
    def _():
        for j in range(SCAN_NB):
            kts_scr[j, pl.ds(rows[0], SCAN_C), :] = kts[j]
            for d in range(2):
                ysum_scr[j, pl.ds(rows[d], SCAN_C), :] = ys[2 * j + d]

    @pl.when(c >= nc // 2)
    def _():
        for j in range(SCAN_NB):
            for d in range(2):
                r, v, g = cur[d][0][j], cur[d][1][j], cur[d][2][j]
                bonus = _head_sums(r * (kts_scr[j, pl.ds(rows[1], SCAN_C), :] if d else kts[j])) * v
                o = ysum_scr[j, pl.ds(rows[d], SCAN_C), :] + ys[2 * j + d]
                dlt = o - _head_sums(o) * (1.0 / N_A)
                var = _head_sums(dlt * dlt) * (1.0 / N_A)
                y = dlt * lax.rsqrt(var + RWKV_GN_EPS) * lng_ref[...] + lnb_ref[...] + bonus
                o_ref[j, pl.ds(rows[d], SCAN_C), :] = (y * _silu(g)).astype(BF16)

    @pl.when(c == nc - 1)
    def _():
        for j in range(SCAN_NB):
            for d, st_ref in enumerate((sf_ref, sb_ref)):
                for p in range(H_A // 2):
                    s = s_scr[j, d, p]
                    st_ref[j, 2 * p] = s[:N_A, :N_A]
                    st_ref[j, 2 * p + 1] = pltpu.roll(s, N_A, axis=1)[N_A:, :N_A]


def _rwkv(z, wp, s0_bd, nb, seq):
    n = z.shape[0]
    nc = seq // SCAN_C
    assert nc % 2 == 0 and nb % SCAN_NB == 0
    z3 = z.reshape(nb, seq, ZW)
    chunk = ((lambda c: c), (lambda c: nc - 1 - c))
    spec = lambda idx, col, w=BRANCH_W: pl.BlockSpec((SCAN_NB, SCAN_C, w), lambda b, c: (b, idx(c), col))
    in_specs = []
    for d in range(2):
        in_specs += [spec(chunk[d], COL_R), spec(chunk[d], COL_V), spec(chunk[d], COL_GA),
                     spec(chunk[d], COL_K), spec(chunk[d], COL_LORA, 256)]

    def spec_next(d, col, w=BRANCH_W):
        def index(b, c):
            wrap = c == nc - 1
            nb_i = jnp.minimum(b + wrap.astype(jnp.int32), nb // SCAN_NB - 1)
            return (nb_i, chunk[d](jnp.where(wrap, 0, c + 1)), col)
        return pl.BlockSpec((SCAN_NB, SCAN_C, w), index)

    for d in range(2):
        in_specs += [spec_next(d, COL_K), spec_next(d, COL_LORA, 256)]
    full = lambda a: pl.BlockSpec(a.shape, lambda b, c: (0,) * a.ndim)
    params = (wp["wlt"], wp["wla"], wp["w0"], wp["a0"], wp["k_k"], wp["k_a"], wp["r_k"], wp["ln_g"], wp["ln_b"])
    in_specs += [full(p) for p in params]
    args = [z3] * 14 + list(params)
    sspec = pl.BlockSpec((SCAN_NB, 2, H_A // 2, LANES, LANES), lambda b, c: (b, 0, 0, 0, 0))
    has_init = s0_bd is not None
    if has_init:
        in_specs = [sspec] + in_specs
        args = [s0_bd] + args
    fspec = pl.BlockSpec((SCAN_NB, H_A, N_A, N_A), lambda b, c: (b, 0, 0, 0))
    o_a, sf, sb = pl.pallas_call(
        functools.partial(_rwkv_kernel, has_init=has_init),
        grid=(nb // SCAN_NB, nc),
        in_specs=in_specs,
        out_specs=[pl.BlockSpec((SCAN_NB, seq, BRANCH_W), lambda b, c: (b, 0, 0)), fspec, fspec],
        out_shape=[jax.ShapeDtypeStruct((nb, seq, BRANCH_W), BF16),
                   jax.ShapeDtypeStruct((nb, H_A, N_A, N_A), F32),
                   jax.ShapeDtypeStruct((nb, H_A, N_A, N_A), F32)],
        scratch_shapes=[pltpu.VMEM((SCAN_NB, 2, H_A // 2, LANES, LANES), F32),
                        pltpu.VMEM((SCAN_NB, seq, BRANCH_W), F32),
                        pltpu.VMEM((SCAN_NB, seq, BRANCH_W), F32),
                        pltpu.VMEM((SCAN_NB, 2, 5, SCAN_C, BRANCH_W), F32)],
        compiler_params=_cparams(("arbitrary", "arbitrary")),
        name="rwkv",
    )(*args)
    return o_a.reshape(n, BRANCH_W), sf, sb


def _swap_pairs(x):
    w = x.shape[1]
    even = (lax.broadcasted_iota(jnp.int32, x.shape, 1) & 1) == 0
    return jnp.where(even, pltpu.roll(x, w - 1, axis=1), pltpu.roll(x, 1, axis=1))


def _mla_prep_kernel(*refs, rope):
    zb_ref, gq_ref, gkv_ref, wq_ref, wk_ref, wvt_ref, pk_ref = refs[:7]
    if rope:
        cq_ref, sq_ref, ck_ref, sk_ref = refs[7:11]
        refs = refs[11:]
    else:
        refs = refs[7:]
    q_o, k_o, vt_o, ckv_o, kr_o = refs
    zb = zb_ref[...]
    q = _mm(_rms(zb[:, :Q_LORA], gq_ref[...]), wq_ref[...])
    ckv = _rms(zb[:, Q_LORA:Q_LORA + KV_LORA], gkv_ref[...])
    kr = zb[:, Q_LORA + KV_LORA:]
    if rope:
        cq = jnp.concatenate([cq_ref[...]] * H_B, axis=1)
        sq = jnp.concatenate([sq_ref[...]] * H_B, axis=1)
        q = q * cq + _swap_pairs(q) * sq
        kr = kr * ck_ref[...] + _swap_pairs(kr) * sk_ref[...]
    q_o[...] = (q * Q_PRESCALE).astype(BF16)
    k_o[...] = (_mm(ckv, wk_ref[...]) + _mm(kr, pk_ref[...])).astype(BF16)
    vt_o[...] = _mm_nt(wvt_ref[...], ckv).astype(BF16)
    ckv_o[...] = ckv
    kr_o[...] = kr[:, :ROPE]


def _mla_prep(z, mp, rope_tabs, seq, tm=1024):
    n = z.shape[0]
    full = lambda a: pl.BlockSpec(a.shape, lambda i: (0,) * a.ndim)
    params = (mp["q_norm"], mp["kv_norm"], mp["wq"], mp["wk"], mp["wvt"], mp["pk"])
    in_specs = [pl.BlockSpec((tm, BRANCH_W), lambda i: (i, COL_ZB))] + [full(p) for p in params]
    args = [z, *params]
    rope = rope_tabs is not None
    if rope:
        per = seq // tm
        in_specs += [pl.BlockSpec((tm, LANES), lambda i: (i % per, 0))] * 4
        args += list(rope_tabs)
    rb = lambda w: pl.BlockSpec((tm, w), lambda i: (i, 0))
    return pl.pallas_call(
        functools.partial(_mla_prep_kernel, rope=rope),
        grid=(n // tm,),
        in_specs=in_specs,
        out_specs=[rb(H_B * LANES), rb(H_B * LANES), pl.BlockSpec((H_B * VDIM, tm), lambda i: (0, i)),
                   rb(KV_LORA), rb(ROPE)],
        out_shape=[jax.ShapeDtypeStruct((n, H_B * LANES), BF16),
                   jax.ShapeDtypeStruct((n, H_B * LANES), BF16),
                   jax.ShapeDtypeStruct((H_B * VDIM, n), BF16),
                   jax.ShapeDtypeStruct((n, KV_LORA), F32),
                   jax.ShapeDtypeStruct((n, ROPE), F32)],
        compiler_params=_cparams(("arbitrary",)),
        name="mla_prep",
    )(*args)


def _kv_expand_kernel(ckv_ref, kr_ref, wk_ref, wvt_ref, pk_ref, k_o, vt_o):
    ckv = ckv_ref[...]
    k_o[...] = (_mm(ckv, wk_ref[...]) + _mm(kr_ref[...], pk_ref[...])).astype(BF16)
    vt_o[...] = _mm_nt(wvt_ref[...], ckv).astype(BF16)


def _kv_expand(ckv, kr128, mp):
    m = ckv.shape[0]
    full = lambda a: pl.BlockSpec(a.shape, lambda i: (0,) * a.ndim)
    args = (ckv, kr128, mp["wk"], mp["wvt"], mp["pk"])
    return pl.pallas_call(
        _kv_expand_kernel,
        grid=(1,),
        in_specs=[full(a) for a in args],
        out_specs=[pl.BlockSpec((m, H_B * LANES), lambda i: (0, 0)),
                   pl.BlockSpec((H_B * VDIM, m), lambda i: (0, 0))],
        out_shape=[jax.ShapeDtypeStruct((m, H_B * LANES), BF16),
                   jax.ShapeDtypeStruct((H_B * VDIM, m), BF16)],
        compiler_params=_cparams(("arbitrary",)),
        name="mla_ctx_kv",
    )(*args)


def _attn_kernel(*refs, has_ctx):
    if has_ctx:
        q_ref, k_ref, vt_ref, kc_ref, vct_ref, g_ref, o_ref = refs
    else:
        q_ref, k_ref, vt_ref, g_ref, o_ref = refs
    nbs, seq = k_ref.shape[0], k_ref.shape[1]
    nt = (((1,), (1,)), ((), ()))
    for b in range(nbs):
        keys = slice(b * seq, (b + 1) * seq)
        outs = []
        for h0 in range(0, H_B, ATTN_GROUP):
            heads = range(h0, h0 + ATTN_GROUP)
            hs = [slice(h * LANES, (h + 1) * LANES) for h in heads]
            vh = [slice(h * VDIM, (h + 1) * VDIM) for h in heads]
            gi = range(ATTN_GROUP)
            s = [lax.dot_general(k_ref[b, :, hs[i]], q_ref[b, :, hs[i]], nt, preferred_element_type=F32)
                 for i in gi]
            m = [jnp.max(s[i], axis=0, keepdims=True) for i in gi]
            if has_ctx:
                sc = [lax.dot_general(kc_ref[:, hs[i]], q_ref[b, :, hs[i]], nt, preferred_element_type=F32)
                      for i in gi]
                m = [jnp.maximum(m[i], jnp.max(sc[i], axis=0, keepdims=True)) for i in gi]
            e = [jnp.exp2(s[i] - m[i]) for i in gi]
            den = [jnp.sum(e[i], axis=0, keepdims=True) for i in gi]
            o = [jnp.dot(vt_ref[vh[i], keys], e[i].astype(BF16), preferred_element_type=F32) for i in gi]
            if has_ctx:
                ec = [jnp.exp2(sc[i] - m[i]) for i in gi]
                den = [den[i] + jnp.sum(ec[i], axis=0, keepdims=True) for i in gi]
                o = [o[i] + jnp.dot(vct_ref[vh[i], :], ec[i].astype(BF16), preferred_element_type=F32)
                     for i in gi]
            outs += [o[i] / den[i] for i in gi]
        o_ref[b] = (jnp.concatenate(outs, axis=0).T * _silu(g_ref[b])).astype(BF16)


def _attention(z, q, k, vt, kc, vct, nb, seq):
    n = z.shape[0]
    tq = min(seq, ATTN_TQ)
    nq = seq // tq
    has_ctx = kc is not None
    nbs = 1 if has_ctx else max(1, min(nb, ATTN_ROWS // seq))
    z3 = z.reshape(nb, seq, ZW)
    in_specs = [pl.BlockSpec((nbs, tq, H_B * LANES), lambda b, i: (b, i, 0)),
                pl.BlockSpec((nbs, seq, H_B * LANES), lambda b, i: (b, 0, 0)),
                pl.BlockSpec((H_B * VDIM, nbs * seq), lambda b, i: (0, b))]
    args = [q.reshape(nb, seq, H_B * LANES), k.reshape(nb, seq, H_B * LANES), vt]
    if has_ctx:
        past = kc.shape[0] // nb
        in_specs += [pl.BlockSpec((past, H_B * LANES), lambda b, i: (b, 0)),
                     pl.BlockSpec((H_B * VDIM, past), lambda b, i: (0, b))]
        args += [kc, vct]
    in_specs.append(pl.BlockSpec((nbs, tq, BRANCH_W), lambda b, i: (b, i, COL_GB)))
    args.append(z3)
    out = pl.pallas_call(
        functools.partial(_attn_kernel, has_ctx=has_ctx),
        grid=(nb // nbs, nq),
        in_specs=in_specs,
        out_specs=pl.BlockSpec((nbs, tq, BRANCH_W), lambda b, i: (b, i, 0)),
        out_shape=jax.ShapeDtypeStruct((nb, seq, BRANCH_W), BF16),
        compiler_params=_cparams(("arbitrary", "arbitrary")),
        name="mla_attention",
    )(*args)
    return out.reshape(n, BRANCH_W)


def _gmlp_kernel(u_ref, vc_ref, g_ref, lng_ref, lnb_ref, ws_ref, bs_ref, o_ref):
    tm = u_ref.shape[0]
    u = jax.nn.gelu(u_ref[...])
    x = jax.nn.gelu(vc_ref[...])
    mu = jnp.mean(x, axis=-1, keepdims=True)
    xc = x - mu
    var = jnp.mean(xc * xc, axis=-1, keepdims=True)
    vn = (xc * lax.rsqrt(var + 1e-5) * lng_ref[...] + lnb_ref[...]).astype(BF16)
    rows = []
    for c in range(tm // CHUNK):
        cols = []
        for g in range(G_C):
            blk = vn[c * CHUNK:(c + 1) * CHUNK, g * LANES:(g + 1) * LANES]
            cols.append(jnp.dot(ws_ref[g], blk, preferred_element_type=F32))
        rows.append(jnp.concatenate(cols, axis=1) + bs_ref[...])
    mixed = jnp.concatenate(rows, axis=0)
    o_ref[...] = (u * mixed * _silu(g_ref[...])).astype(BF16)


def _gmlp(z, gp, tm=512):
    n = z.shape[0]
    zb = lambda c: pl.BlockSpec((tm, BRANCH_W), lambda i, c=c: (i, c))
    full = lambda a: pl.BlockSpec(a.shape, lambda i: (0,) * a.ndim)
    params = (gp["ln_g"], gp["ln_b"], gp["w_s"], gp["b_s"])
    return pl.pallas_call(
        _gmlp_kernel,
        grid=(n // tm,),
        in_specs=[zb(COL_U), zb(COL_VC), zb(COL_GC)] + [full(p) for p in params],
        out_specs=pl.BlockSpec((tm, BRANCH_W), lambda i: (i, 0)),
        out_shape=jax.ShapeDtypeStruct((n, BRANCH_W), BF16),
        compiler_params=_cparams(("arbitrary",)),
        name="gmlp",
    )(z, z, z, *params)


def _fnet_kernel(f_ref, g_ref, cs_ref, dft_ref, o_ref, xcs_scr, *, seq):
    r = pl.program_id(1)
    nbs = f_ref.shape[0]

    @pl.when(r == 0)
    def _():
        for b in range(nbs):
            for g in range(G_D):
                xg = _mm(f_ref[b, :, g * LANES:(g + 1) * LANES], cs_ref[...])
                xcs_scr[b, 0:seq, g * LANES:(g + 1) * LANES] = xg[:, :LANES].astype(BF16)
                xcs_scr[b, seq:2 * seq, g * LANES:(g + 1) * LANES] = xg[:, LANES:].astype(BF16)

    for b in range(nbs):
        y = jnp.dot(dft_ref[...], xcs_scr[b], preferred_element_type=F32)
        o_ref[b] = (y * (1.0 / math.sqrt(seq * LANES)) * _silu(g_ref[b])).astype(BF16)


def _fnet(z, cs128, dft, nb, seq):
    n = z.shape[0]
    tr = min(seq, FNET_ROWS)
    nr = seq // tr
    nbs = max(1, min(nb, FRONT_TM // seq))
    z3 = z.reshape(nb, seq, ZW)
    out = pl.pallas_call(
        functools.partial(_fnet_kernel, seq=seq),
        grid=(nb // nbs, nr),
        in_specs=[pl.BlockSpec((nbs, seq, BRANCH_W), lambda b, r: (b, 0, COL_F)),
                  pl.BlockSpec((nbs, tr, BRANCH_W), lambda b, r: (b, r, COL_GD)),
                  pl.BlockSpec((LANES, 2 * LANES), lambda b, r: (0, 0)),
                  pl.BlockSpec((tr, 2 * seq), lambda b, r: (r, 0))],
        out_specs=pl.BlockSpec((nbs, tr, BRANCH_W), lambda b, r: (b, r, 0)),
        out_shape=jax.ShapeDtypeStruct((nb, seq, BRANCH_W), BF16),
        scratch_shapes=[pltpu.VMEM((nbs, 2 * seq, BRANCH_W), BF16)],
        compiler_params=_cparams(("arbitrary", "arbitrary")),
        name="fnet",
    )(z3, z3, cs128, dft)
    return out.reshape(n, BRANCH_W)


def _back_kernel(x_ref, h_ref, gate_ref, oa_ref, ob_ref, oc_ref, od_ref, wm_ref, bm_ref,
                 wb_ref, wo_ref, fg_ref, y_ref, acc_scr, *, final):
    nb = pl.program_id(1)
    tm = x_ref.shape[0]
    chunks = [pl.ds(r, BACK_ROWS) for r in range(0, tm, BACK_ROWS)]

    for i, o_ref in enumerate((oa_ref, ob_ref, oc_ref, od_ref)):
        @pl.when(nb == i)
        def _(o_ref=o_ref, i=i):
            for rows in chunks:
                gates = jax.nn.sigmoid(jnp.dot(h_ref[rows, :], wm_ref[...], preferred_element_type=F32)
                                       + bm_ref[...])
                upd = gates * jnp.dot(o_ref[rows, :], wb_ref[0], preferred_element_type=F32)
                if i == 0:
                    acc_scr[rows, :] = upd
                else:
                    acc_scr[rows, :] += upd

    @pl.when(nb == 3)
    def _():
        for rows in chunks:
            xn = x_ref[rows, :] + gate_ref[0] * _mm(acc_scr[rows, :], wo_ref[...])
            if final:
                xn = _rms(xn, fg_ref[...])
            y_ref[rows, :] = xn


def _back(x2d, h, gate, o_a, o_b, o_c, o_d, lw, final_g, final, tm=1024):
    n = x2d.shape[0]
    rows_per_gate = FRONT_TM // tm
    ob = pl.BlockSpec((tm, BRANCH_W), lambda i, j: (i, 0))
    return pl.pallas_call(
        functools.partial(_back_kernel, final=final),
        grid=(n // tm, 4),
        in_specs=[pl.BlockSpec((tm, D_MODEL), lambda i, j: (i, 0)),
                  pl.BlockSpec((tm, D_MODEL), lambda i, j: (i, 0)),
                  pl.BlockSpec((1, 1, D_MODEL), lambda i, j: (i // rows_per_gate, 0, 0)),
                  ob, ob, ob, ob,
                  pl.BlockSpec((D_MODEL, D_MODEL), lambda i, j: (0, j)),
                  pl.BlockSpec((1, D_MODEL), lambda i, j: (0, j)),
                  pl.BlockSpec((1, BRANCH_W, D_MODEL), lambda i, j: (j, 0, 0)),
                  pl.BlockSpec((D_MODEL, D_MODEL), lambda i, j: (0, 0)),
                  pl.BlockSpec((1, D_MODEL), lambda i, j: (0, 0))],
        out_specs=pl.BlockSpec((tm, D_MODEL), lambda i, j: (i, 0)),
        out_shape=jax.ShapeDtypeStruct((n, D_MODEL), F32),
        scratch_shapes=[pltpu.VMEM((tm, D_MODEL), F32)],
        compiler_params=_cparams(("arbitrary", "arbitrary")),
        name="back",
    )(x2d, h, gate, o_a, o_b, o_c, o_d, lw["w_merge"], lw["b_merge"], lw["w_branch"],
      lw["w_out"], final_g)


def _pad_cols_kernel(w_ref, o_ref):
    rows = w_ref.shape[0]
    o_ref[:, :SHIFT_W] = w_ref[:, :SHIFT_W].astype(BF16)
    o_ref[:, SHIFT_W:Z_SHIFT_PAD] = jnp.zeros((rows, Z_SHIFT_PAD - SHIFT_W), BF16)
    o_ref[:, Z_SHIFT_PAD:Z_SHIFT_PAD + 928] = w_ref[:, SHIFT_W:2656].astype(BF16)
    o_ref[:, Z_SHIFT_PAD + 928:3072] = jnp.zeros((rows, 3072 - Z_SHIFT_PAD - 928), BF16)
    o_ref[:, 3072:] = w_ref[:, 2656:].astype(BF16)


def _pad_cols(w, tm=256):
    k, n_in = w.shape
    return pl.pallas_call(
        _pad_cols_kernel,
        grid=(k // tm,),
        in_specs=[pl.BlockSpec((tm, n_in), lambda i: (i, 0))],
        out_specs=pl.BlockSpec((tm, ZW), lambda i: (i, 0)),
        out_shape=jax.ShapeDtypeStruct((k, ZW), BF16),
        compiler_params=_cparams(("arbitrary",)),
        name="pad_w_in",
    )(w)


def _rope_place():
    p = np.zeros((LANES, H_B * LANES), np.float32)
    for h in range(H_B):
        for j in range(ROPE):
            p[j, h * LANES + NOPE + j] = 1.0
    return jnp.asarray(p, BF16)


def _pack_q(w):
    w = w.reshape(Q_LORA, H_B, NOPE + ROPE)
    return jnp.pad(w, ((0, 0), (0, 0), (0, LANES - NOPE - ROPE))).reshape(Q_LORA, H_B * LANES)


def _pack_kv(w):
    w = w.reshape(KV_LORA, H_B, NOPE + VDIM)
    wk = jnp.pad(w[..., :NOPE], ((0, 0), (0, 0), (0, LANES - NOPE))).reshape(KV_LORA, H_B * LANES)
    return wk, w[..., NOPE:].reshape(KV_LORA, H_B * VDIM)


def _rope_tables(n_tokens):
    rows = n_tokens // GRID_W
    row = jnp.repeat(jnp.arange(rows, dtype=F32), GRID_W)
    col = jnp.tile(jnp.arange(GRID_W, dtype=F32), rows)
    n_freq = ROPE // 4
    inv = ROPE_BASE ** (-jnp.arange(n_freq, dtype=F32) / n_freq)
    ang = jnp.concatenate([row[:, None] * inv, col[:, None] * inv], axis=-1)
    cos = jnp.repeat(jnp.cos(ang), 2, axis=-1)
    sin = jnp.repeat(jnp.sin(ang), 2, axis=-1) * jnp.tile(jnp.asarray([-1.0, 1.0], F32), ROPE // 2)
    ones = lambda k: jnp.ones((n_tokens, k), F32)
    zeros = lambda k: jnp.zeros((n_tokens, k), F32)
    cq = jnp.concatenate([ones(NOPE), cos, ones(LANES - NOPE - ROPE)], axis=-1)
    sq = jnp.concatenate([zeros(NOPE), sin, zeros(LANES - NOPE - ROPE)], axis=-1)
    ck = jnp.concatenate([cos, ones(LANES - ROPE)], axis=-1)
    sk = jnp.concatenate([sin, zeros(LANES - ROPE)], axis=-1)
    return cq, sq, ck, sk


def _dft_tables(seq):
    k = np.arange(LANES)
    a = 2.0 * np.pi * ((k[:, None] * k[None, :]) % LANES) / LANES
    cs128 = np.concatenate([np.cos(a), np.sin(a)], axis=1)
    t = np.arange(seq)
    b = 2.0 * np.pi * ((t[:, None] * t[None, :]) % seq) / seq
    dft = np.concatenate([np.cos(b), -np.sin(b)], axis=1)
    return (jnp.asarray(cs128.astype(np.float32)).astype(BF16),
            jnp.asarray(dft.astype(np.float32)).astype(BF16))


def _lora_weights(w_up, a_up):
    z = jnp.zeros((LORA, BRANCH_W), F32)
    pad = jnp.zeros((256 - 3 * LORA, 2 * BRANCH_W), F32)
    wlt = jnp.concatenate([jnp.concatenate([w_up[0], z], 1), jnp.concatenate([z, w_up[1]], 1),
                           jnp.concatenate([z, z], 1), pad], axis=0)
    wla = jnp.concatenate([jnp.concatenate([z, z], 1), jnp.concatenate([z, z], 1),
                           jnp.concatenate([a_up[0], a_up[1]], 1), pad], axis=0)
    return wlt.astype(BF16), wla.astype(BF16)


def _to_blockdiag(s):
    b = s.shape[0]
    s = s.reshape(b, H_A // 2, 2, N_A, N_A)
    z = jnp.zeros_like(s[:, :, 0])
    top = jnp.concatenate([s[:, :, 0], z], axis=-1)
    bot = jnp.concatenate([z, s[:, :, 1]], axis=-1)
    return jnp.concatenate([top, bot], axis=-2)


def _trunk_layer(x2d, ss, lw, nb, seq, rope_tabs, ctx, final_g, final):
    z, h = _front(x2d, ss[:, :2], lw["norm_g"], lw["w_in"], lw["mu"], seq)
    s0 = None
    if ctx is not None:
        s0 = jnp.stack([_to_blockdiag(ctx[0]), _to_blockdiag(ctx[1])], axis=1)
    o_a, sf, sb = _rwkv(z, lw["rwkv"], s0, nb, seq)
    q, k, v, ckv, kr = _mla_prep(z, lw["mla"], rope_tabs, seq)
    kc = vc = None
    if ctx is not None:
        past = ctx[2].shape[1]
        kr_c = jnp.pad(ctx[3].reshape(nb * past, ROPE), ((0, 0), (0, LANES - ROPE)))
        kc, vc = _kv_expand(ctx[2].reshape(nb * past, KV_LORA), kr_c, lw["mla"])
    o_b = _attention(z, q, k, v, kc, vc, nb, seq)
    o_c = _gmlp(z, lw["gmlp"])
    o_d = _fnet(z, *lw["dft"][seq], nb, seq)
    x_new = _back(x2d, h, ss[:, 2:], o_a, o_b, o_c, o_d, lw, final_g, final)
    return x_new, sf, sb, ckv, kr


def kernel(x_prompt, x_sample, state_rwkv_fwd, state_rwkv_bwd, cache_mla_ckv, cache_mla_krope, c, c_ctx, norm_g, w_ada, b_ada, w_in, shift_mu, rwkv_w0, rwkv_w_up, rwkv_a0, rwkv_a_up, rwkv_k_k, rwkv_k_a, rwkv_r_k, rwkv_ln_g, rwkv_ln_b, mla_q_norm, mla_w_q_up, mla_kv_norm, mla_w_kv_up, gmlp_ln_g, gmlp_ln_b, gmlp_w_s, gmlp_b_s, w_branch, w_merge, b_merge, w_out, final_norm_g):
    nb_c, seq_c, _ = x_prompt.shape
    nb_l, seq_l, _ = x_sample.shape
    assert (nb_c * seq_c) % FRONT_TM == 0 and FRONT_TM % seq_c == 0 and seq_l == FRONT_TM

    cond8 = jnp.concatenate([c_ctx[None], c, jnp.zeros((8 - 1 - nb_l, D_MODEL), F32)], axis=0)
    mod = _modulation(cond8, w_ada, b_ada).reshape(DEPTH, 8, 3, D_MODEL)
    n_ctx_tiles = nb_c * seq_c // FRONT_TM

    pk = _rope_place()
    rope_tabs = _rope_tables(seq_l)
    dft = {s: _dft_tables(s) for s in {seq_c, seq_l}}
    mu_p = jnp.pad(shift_mu, ((0, 0), (0, Z_SHIFT_PAD - SHIFT_W)))[:, None]
    final_g = final_norm_g[None]

    xc = x_prompt.reshape(nb_c * seq_c, D_MODEL)
    xl = x_sample.reshape(nb_l * seq_l, D_MODEL)
    sf_list, sb_list, ckv_list, kr_list = [], [], [], []
    for l in range(DEPTH):
        wlt, wla = _lora_weights(rwkv_w_up[l], rwkv_a_up[l])
        wk, wv = _pack_kv(mla_w_kv_up[l])
        lw = {
            "norm_g": norm_g[l][None], "w_in": _pad_cols(w_in[l]), "mu": mu_p[l],
            "rwkv": {
                "wlt": wlt, "wla": wla, "w0": rwkv_w0[l].reshape(1, 2 * BRANCH_W),
                "a0": rwkv_a0[l].reshape(1, 2 * BRANCH_W), "k_k": rwkv_k_k[l][None],
                "k_a": rwkv_k_a[l][None], "r_k": rwkv_r_k[l].reshape(1, BRANCH_W),
                "ln_g": rwkv_ln_g[l][None], "ln_b": rwkv_ln_b[l][None],
            },
            "mla": {
                "q_norm": mla_q_norm[l][None], "kv_norm": mla_kv_norm[l][None],
                "wq": _pack_q(mla_w_q_up[l]).astype(BF16), "wk": wk.astype(BF16),
                "wvt": wv.T.astype(BF16), "pk": pk,
            },
            "gmlp": {
                "ln_g": gmlp_ln_g[l][None], "ln_b": gmlp_ln_b[l][None],
                "w_s": gmlp_w_s[l].astype(BF16),
                "b_s": jnp.repeat(gmlp_b_s[l].T, BRANCH_W // G_C, axis=1),
            },
            "dft": dft,
            "w_merge": w_merge[l].astype(BF16), "b_merge": b_merge[l][None],
            "w_branch": w_branch[l].astype(BF16), "w_out": w_out[l].astype(BF16),
        }
        final = l == DEPTH - 1
        ss_c = jnp.broadcast_to(mod[l, 0][None], (n_ctx_tiles, 3, D_MODEL))
        ss_l = mod[l, 1:1 + nb_l]
        xc, sf, sb, ckv, kr = _trunk_layer(xc, ss_c, lw, nb_c, seq_c, None, None, final_g, final)
        sf_list.append(sf)
        sb_list.append(sb)
        ckv_list.append(ckv.reshape(nb_c, seq_c, KV_LORA))
        kr_list.append(kr.reshape(nb_c, seq_c, ROPE))
        ctx = (state_rwkv_fwd[:, l], state_rwkv_bwd[:, l], cache_mla_ckv[:, l], cache_mla_krope[:, l])
        xl = _trunk_layer(xl, ss_l, lw, nb_l, seq_l, rope_tabs, ctx, final_g, final)[0]

    return (xc.reshape(nb_c, seq_c, D_MODEL), xl.reshape(nb_l, seq_l, D_MODEL),
            jnp.stack(sf_list, axis=1), jnp.stack(sb_list, axis=1),
            jnp.stack(ckv_list, axis=1), jnp.stack(kr_list, axis=1))
```

```python
import functools
import math

import jax
import jax.numpy as jnp
import numpy as np
from jax import lax
from jax.experimental import pallas as pl
from jax.experimental.pallas import tpu as pltpu

F32 = jnp.float32
BF16 = jnp.bfloat16

D_MODEL = 1024
DEPTH = 2
GRID_W = 64
BRANCH_W = 512
H_A = 8
N_A = 64
LORA = 64
RWKV_GN_EPS = 64e-5
H_B = 8
NOPE = 64
ROPE = 32
VDIM = 64
Q_LORA = 256
KV_LORA = 128
ROPE_BASE = 10000.0
Q_PRESCALE = (NOPE + ROPE) ** -0.5 * math.log2(math.e)
G_C = 4
CHUNK = 128
G_D = 4
NORM_EPS = 1e-6
SHIFT_W = 1728

LANES = 128
VMEM_LIMIT = 52 * 1024 * 1024

ZW = 6144
Z_SHIFT_PAD = 2048
COL_R, COL_K, COL_V = 0, 1, 2
COL_LORA = 6
COL_GA, COL_ZB, COL_GB, COL_U, COL_VC, COL_GC, COL_F, COL_GD = 4, 5, 6, 7, 8, 9, 10, 11
FRONT_TM = 2048
FRONT_TN = 1024
FRONT_COLS = 256
N_SHIFT_TILES = Z_SHIFT_PAD // FRONT_TN
SCAN_C = 64
SCAN_NB = 2
ATTN_GROUP = 4
ATTN_TQ = 512
FNET_ROWS = 1024
ATTN_ROWS = 1024
BACK_ROWS = 256


def _mm(a, b):
    return jnp.dot(a.astype(BF16), b.astype(BF16), preferred_element_type=F32)


def _mm_nt(a, b):
    return lax.dot_general(a.astype(BF16), b.astype(BF16), (((1,), (1,)), ((), ())),
                           preferred_element_type=F32)


def _mm_split_lhs01(m01, x):
    hi = x.astype(BF16)
    lo = (x - hi.astype(F32)).astype(BF16)
    d = functools.partial(jnp.dot, preferred_element_type=F32)
    return d(m01, hi) + d(m01, lo)


def _silu(x):
    return x * jax.nn.sigmoid(x)


def _rms(x, g):
    return x * lax.rsqrt(jnp.mean(x * x, axis=-1, keepdims=True) + NORM_EPS) * g


def _cparams(sem, vmem=VMEM_LIMIT):
    return pltpu.CompilerParams(dimension_semantics=sem, vmem_limit_bytes=vmem)


def _mod_kernel(c_ref, w_ref, b_ref, o_ref):
    o_ref[0] = _mm(_silu(c_ref[...]), w_ref[0]) + b_ref[0]


def _modulation(cond8, w_ada, b_ada):
    return pl.pallas_call(
        _mod_kernel,
        grid=(DEPTH, 3),
        in_specs=[
            pl.BlockSpec((8, D_MODEL), lambda l, j: (0, 0)),
            pl.BlockSpec((1, D_MODEL, D_MODEL), lambda l, j: (l, 0, j)),
            pl.BlockSpec((1, 1, D_MODEL), lambda l, j: (l, 0, j)),
        ],
        out_specs=pl.BlockSpec((1, 8, D_MODEL), lambda l, j: (l, 0, j)),
        out_shape=jax.ShapeDtypeStruct((DEPTH, 8, 3 * D_MODEL), F32),
        compiler_params=_cparams(("arbitrary", "arbitrary")),
        name="modulation",
    )(cond8, w_ada, b_ada.reshape(DEPTH, 1, 3 * D_MODEL))


def _front_kernel(x_ref, ss_ref, g_ref, w_ref, mu_ref, z_ref, h_ref, *, seq):
    j = pl.program_id(1)

    @pl.when(j == 0)
    def _():
        h = _rms(x_ref[...], g_ref[...]) * (1.0 + ss_ref[0, 1:2, :]) + ss_ref[0, 0:1, :]
        h_ref[...] = h.astype(BF16)

    cols = [slice(c, c + FRONT_COLS) for c in range(0, FRONT_TN, FRONT_COLS)]

    @pl.when(j < N_SHIFT_TILES)
    def _():
        first = lax.broadcasted_iota(jnp.int32, (8, FRONT_COLS), 0) == 0
        last = lax.broadcasted_iota(jnp.int32, (8, FRONT_COLS), 0) == 7
        for cs in cols:
            z = jnp.dot(h_ref[...], w_ref[:, cs], preferred_element_type=F32)
            hmu = 0.5 * mu_ref[:, cs]
            omu = 1.0 - mu_ref[:, cs]
            prev = pltpu.roll(z, 1, axis=0)
            nxt = pltpu.roll(z, FRONT_TM - 1, axis=0)
            z_ref[:, cs] = z * omu + (prev + nxt) * hmu
            for s0 in range(0, FRONT_TM, seq):
                a = slice(s0, s0 + 8)
                z_ref[a, cs] = z[a] * omu + (jnp.where(first, 0.0, prev[a]) + nxt[a]) * hmu
                b = slice(s0 + seq - 8, s0 + seq)
                z_ref[b, cs] = z[b] * omu + (prev[b] + jnp.where(last, 0.0, nxt[b])) * hmu

    @pl.when(j >= N_SHIFT_TILES)
    def _():
        for cs in cols:
            z_ref[:, cs] = jnp.dot(h_ref[...], w_ref[:, cs], preferred_element_type=F32)


def _front(x2d, ss, norm_g, w_in_p, mu_p, seq):
    n = x2d.shape[0]
    return pl.pallas_call(
        functools.partial(_front_kernel, seq=seq),
        grid=(n // FRONT_TM, ZW // FRONT_TN),
        in_specs=[
            pl.BlockSpec((FRONT_TM, D_MODEL), lambda i, j: (i, 0)),
            pl.BlockSpec((1, 2, D_MODEL), lambda i, j: (i, 0, 0)),
            pl.BlockSpec((1, D_MODEL), lambda i, j: (0, 0)),
            pl.BlockSpec((D_MODEL, FRONT_TN), lambda i, j: (0, j)),
            pl.BlockSpec((1, FRONT_TN), lambda i, j: (0, jnp.minimum(j, N_SHIFT_TILES - 1))),
        ],
        out_specs=[pl.BlockSpec((FRONT_TM, FRONT_TN), lambda i, j: (i, j)),
                   pl.BlockSpec((FRONT_TM, D_MODEL), lambda i, j: (i, 0))],
        out_shape=[jax.ShapeDtypeStruct((n, ZW), F32), jax.ShapeDtypeStruct((n, D_MODEL), BF16)],
        compiler_params=_cparams(("arbitrary", "arbitrary")),
        name="front",
    )(x2d, ss, norm_g, w_in_p, mu_p)


def _scan_chunks(dirs, s_ref):
    c = SCAN_C
    n2 = 2 * c
    ri = lax.broadcasted_iota(jnp.int32, (c, c), 0)
    ci = lax.broadcasted_iota(jnp.int32, (c, c), 1)
    row = lax.broadcasted_iota(jnp.int32, (n2, n2), 0)
    col = lax.broadcasted_iota(jnp.int32, (n2, n2), 1)
    same64 = (row >> 6) == (col >> 6)
    same16 = (row >> 4) == (col >> 4)
    tl = row & (c - 1)
    il = col & (c - 1)
    eye = (row == col).astype(F32)
    lane_a = lax.broadcasted_iota(jnp.int32, (c, LANES), 1) < N_A

    def stack2(x):
        return jnp.concatenate([jnp.where(lane_a, x, 0.0), jnp.where(lane_a, 0.0, x)], axis=0)

    lhs, rhs, v2, bk, gtot, strict, incl, key = [], [], [], [], [], [], [], []
    for r, kk, v, lw, kt, bv, rev, j, d in dirs:
        tri = ((ri <= ci) if rev else (ri >= ci)).astype(BF16)
        cs = _mm_split_lhs01(tri, lw)
        tot = cs[0:1] if rev else cs[c - 1:c]
        g_tot = jnp.exp(tot)
        a_t = -kk * jnp.exp(cs - lw)
        r_t = r * jnp.exp(cs)
        g_inv = jnp.exp(-cs)
        b_t = bv * g_inv
        k_t = kt * g_inv
        g_rem = jnp.exp(tot - cs)
        b_h = bv * g_rem
        k_h = kt * g_rem
        st = same64 & ((il > tl) if rev else (il < tl))
        inc = same64 & ((il >= tl) if rev else (il <= tl))
        for p in range(H_A // 2):
            sl = slice(p * LANES, (p + 1) * LANES)
            lhs.append(jnp.concatenate([stack2(a_t[:, sl]), stack2(r_t[:, sl])], axis=0))
            rhs.append(jnp.concatenate([b_t[:, sl], k_t[:, sl]], axis=0))
            v2.append(stack2(v[:, sl]))
            bk.append(jnp.concatenate([stack2(b_h[:, sl]), stack2(k_h[:, sl])], axis=0))
            gtot.append(g_tot[:, sl])
            strict.append(st)
            incl.append(inc)
            key.append((j, d, p))
    ch = range(len(key))

    s = [s_ref[key[i]] for i in ch]
    big = [_mm_nt(lhs[i], jnp.concatenate([rhs[i], s[i]], axis=0)) for i in ch]
    lab, lak, mrbk = [], [], []
    for i in ch:
        x = big[i][:n2, :n2]
        xr = pltpu.roll(x, c, axis=1)
        y = big[i][n2:, :n2]
        yr = pltpu.roll(y, c, axis=1)
        pick = lambda top, bot: jnp.concatenate([top[:c], bot[c:]], axis=0)
        lab.append(jnp.where(strict[i], pick(x, xr), 0.0))
        lak.append(jnp.where(strict[i], pick(xr, x), 0.0))
        mrbk.append(jnp.concatenate([jnp.where(incl[i], pick(y, yr), 0.0),
                                     jnp.where(incl[i], pick(yr, y), 0.0)], axis=1))
    dg = [jnp.where(same16, lab[i], 0.0) for i in ch]
    off = [lab[i] - dg[i] for i in ch]
    pinv = [eye + dg[i] for i in ch]
    pw = [_mm(dg[i], dg[i]) for i in ch]
    for _ in range(2):
        t = [_mm(pw[i], jnp.concatenate([pw[i], pinv[i]], axis=1)) for i in ch]
        pinv = [pinv[i] + t[i][:, n2:] for i in ch]
        pw = [t[i][:, :n2] for i in ch]
    pinv = [pinv[i] + _mm(pw[i], pinv[i]) for i in ch]
    f = [_mm(pinv[i], off[i]) for i in ch]
    t = [_mm(f[i], jnp.concatenate([f[i], pinv[i]], axis=1)) for i in ch]
    g = [pinv[i] + t[i][:, n2:] for i in ch]
    tinv = [g[i] + _mm(t[i][:, :n2], g[i]) for i in ch]
    lv = [_mm(lak[i], v2[i]) for i in ch]
    u2 = [_mm(tinv[i], big[i][:n2, n2:] + lv[i]) for i in ch]
    uv = [jnp.concatenate([u2[i], v2[i]], axis=0) for i in ch]
    y2 = [big[i][n2:, n2:] + _mm(mrbk[i], uv[i]) for i in ch]
    for i in ch:
        s_ref[key[i]] = s[i] * gtot[i] + _mm(uv[i].T, bk[i])
    npair = H_A // 2
    return [jnp.concatenate([y2[g * npair + p][:c] + y2[g * npair + p][c:] for p in range(npair)], axis=1)
            for g in range(len(dirs))]


def _head_sums(x):
    lane_a = lax.broadcasted_iota(jnp.int32, (x.shape[0], LANES), 1) < N_A
    outs = []
    for p in range(H_A // 2):
        xs = x[:, p * LANES:(p + 1) * LANES]
        sa = jnp.sum(jnp.where(lane_a, xs, 0.0), axis=-1, keepdims=True)
        sb = jnp.sum(jnp.where(lane_a, 0.0, xs), axis=-1, keepdims=True)
        outs.append(jnp.where(lane_a, sa, sb))
    return jnp.concatenate(outs, axis=1)


def _rwkv_prepare(k, lora, d, w):
    wlt_ref, wla_ref, w0_ref, a0_ref, kk_ref, ka_ref, rk_ref = w
    hs = slice(d * BRANCH_W, (d + 1) * BRANCH_W)
    lo_w, lo_a = lora[:, :LANES], lora[:, LANES:]
    pre = w0_ref[:, hs] + _mm(jnp.tanh(lo_w), wlt_ref[:LANES, hs])
    lw = -math.exp(-0.5) * jax.nn.sigmoid(pre)
    kkraw = k * kk_ref[...]
    kk = kkraw * lax.rsqrt(jnp.maximum(_head_sums(kkraw * kkraw), 1e-24))
    k_a = ka_ref[...]
    if d:
        a_b = jax.nn.sigmoid(a0_ref[:, hs] + _mm(lo_a, wla_ref[LANES:, hs]))
        return kk, lw, k * (1.0 + (a_b - 1.0) * k_a), kk * a_b
    a2 = jax.nn.sigmoid(a0_ref[...] + _mm(lo_a, wla_ref[LANES:, :]))
    ktf = k * (1.0 + (a2[:, :BRANCH_W] - 1.0) * k_a)
    ktb = k * (1.0 + (a2[:, BRANCH_W:] - 1.0) * k_a)
    return kk, lw, ktf, kk * a2[:, hs], (ktf + ktb) * rk_ref[...]


def _rwkv_kernel(*refs, has_init):
    if has_init:
        s0_ref, refs = refs[0], refs[1:]
    cur = (refs[0:5], refs[5:10])
    nxt = (refs[10:12], refs[12:14])
    w = refs[14:21]
    lng_ref, lnb_ref = refs[21:23]
    o_ref, sf_ref, sb_ref, s_scr, ysum_scr, kts_scr, p_scr = refs[23:30]
    c = pl.program_id(1)
    nc = pl.num_programs(1)

    @pl.when(c == 0)
    def _():
        if has_init:
            s_scr[...] = s0_ref[...]
        else:
            s_scr[...] = jnp.zeros_like(s_scr)

    @pl.when((c == 0) & (pl.program_id(0) == 0))
    def _():
        for j in range(SCAN_NB):
            for d in range(2):
                for i, a in enumerate(_rwkv_prepare(cur[d][3][j], cur[d][4][j], d, w)):
                    p_scr[j, d, i] = a

    dirs = []
    for j in range(SCAN_NB):
        for d in range(2):
            kk, lw, kt, bv = [p_scr[j, d, i] for i in range(4)]
            dirs.append((cur[d][0][j], kk, cur[d][1][j], lw, kt, bv, d == 1, j, d))
    kts = [p_scr[j, 0, 4] for j in range(SCAN_NB)]
    ys = _scan_chunks(dirs, s_scr)

    for j in range(SCAN_NB):
        for d in range(2):
            for i, a in enumerate(_rwkv_prepare(nxt[d][0][j], nxt[d][1][j], d, w)):
                p_scr[j, d, i] = a

    rows = (pl.multiple_of(c * SCAN_C, SCAN_C), pl.multiple_of((nc - 1 - c) * SCAN_C, SCAN_C))

    @pl.when(c < nc // 2)
    def _():
        for j in range(SCAN_NB):
            kts_scr[j, pl.ds(rows[0], SCAN_C), :] = kts[j]
            for d in range(2):
                ysum_scr[j, pl.ds(rows[d], SCAN_C), :] = ys[2 * j + d]

    @pl.when(c >= nc // 2)
    def _():
        for j in range(SCAN_NB):
            for d in range(2):
                r, v, g = cur[d][0][j], cur[d][1][j], cur[d][2][j]
                bonus = _head_sums(r * (kts_scr[j, pl.ds(rows[1], SCAN_C), :] if d else kts[j])) * v
                o = ysum_scr[j, pl.ds(rows[d], SCAN_C), :] + ys[2 * j + d]
                dlt = o - _head_sums(o) * (1.0 / N_A)
                var = _head_sums(dlt * dlt) * (1.0 / N_A)
                y = dlt * lax.rsqrt(var + RWKV_GN_EPS) * lng_ref[...] + lnb_ref[...] + bonus
                o_ref[j, pl.ds(rows[d], SCAN_C), :] = (y * _silu(g)).astype(BF16)

    @pl.when(c == nc - 1)
    def _():
        for j in range(SCAN_NB):
            for d, st_ref in enumerate((sf_ref, sb_ref)):
                for p in range(H_A // 2):
                    s = s_scr[j, d, p]
                    st_ref[j, 2 * p] = s[:N_A, :N_A]
                    st_ref[j, 2 * p + 1] = pltpu.roll(s, N_A, axis=1)[N_A:, :N_A]


def _rwkv(z, wp, s0_bd, nb, seq):
    n = z.shape[0]
    nc = seq // SCAN_C
    assert nc % 2 == 0 and nb % SCAN_NB == 0
    z3 = z.reshape(nb, seq, ZW)
    chunk = ((lambda c: c), (lambda c: nc - 1 - c))
    spec = lambda idx, col, w=BRANCH_W: pl.BlockSpec((SCAN_NB, SCAN_C, w), lambda b, c: (b, idx(c), col))
    in_specs = []
    for d in range(2):
        in_specs += [spec(chunk[d], COL_R), spec(chunk[d], COL_V), spec(chunk[d], COL_GA),
                     spec(chunk[d], COL_K), spec(chunk[d], COL_LORA, 256)]

    def spec_next(d, col, w=BRANCH_W):
        def index(b, c):
            wrap = c == nc - 1
            nb_i = jnp.minimum(b + wrap.astype(jnp.int32), nb // SCAN_NB - 1)
            return (nb_i, chunk[d](jnp.where(wrap, 0, c + 1)), col)
        return pl.BlockSpec((SCAN_NB, SCAN_C, w), index)

    for d in range(2):
        in_specs += [spec_next(d, COL_K), spec_next(d, COL_LORA, 256)]
    full = lambda a: pl.BlockSpec(a.shape, lambda b, c: (0,) * a.ndim)
    params = (wp["wlt"], wp["wla"], wp["w0"], wp["a0"], wp["k_k"], wp["k_a"], wp["r_k"], wp["ln_g"], wp["ln_b"])
    in_specs += [full(p) for p in params]
    args = [z3] * 14 + list(params)
    sspec = pl.BlockSpec((SCAN_NB, 2, H_A // 2, LANES, LANES), lambda b, c: (b, 0, 0, 0, 0))
    has_init = s0_bd is not None
    if has_init:
        in_specs = [sspec] + in_specs
        args = [s0_bd] + args
    fspec = pl.BlockSpec((SCAN_NB, H_A, N_A, N_A), lambda b, c: (b, 0, 0, 0))
    o_a, sf, sb = pl.pallas_call(
        functools.partial(_rwkv_kernel, has_init=has_init),
        grid=(nb // SCAN_NB, nc),
        in_specs=in_specs,
        out_specs=[pl.BlockSpec((SCAN_NB, seq, BRANCH_W), lambda b, c: (b, 0, 0)), fspec, fspec],
        out_shape=[jax.ShapeDtypeStruct((nb, seq, BRANCH_W), BF16),
                   jax.ShapeDtypeStruct((nb, H_A, N_A, N_A), F32),
                   jax.ShapeDtypeStruct((nb, H_A, N_A, N_A), F32)],
        scratch_shapes=[pltpu.VMEM((SCAN_NB, 2, H_A // 2, LANES, LANES), F32),
                        pltpu.VMEM((SCAN_NB, seq, BRANCH_W), F32),
                        pltpu.VMEM((SCAN_NB, seq, BRANCH_W), F32),
                        pltpu.VMEM((SCAN_NB, 2, 5, SCAN_C, BRANCH_W), F32)],
        compiler_params=_cparams(("arbitrary", "arbitrary")),
        name="rwkv",
    )(*args)
    return o_a.reshape(n, BRANCH_W), sf, sb


def _swap_pairs(x):
    w = x.shape[1]
    even = (lax.broadcasted_iota(jnp.int32, x.shape, 1) & 1) == 0
    return jnp.where(even, pltpu.roll(x, w - 1, axis=1), pltpu.roll(x, 1, axis=1))


def _mla_prep_kernel(*refs, rope):
    zb_ref, gq_ref, gkv_ref, wq_ref, wk_ref, wvt_ref, pk_ref = refs[:7]
    if rope:
        cq_ref, sq_ref, ck_ref, sk_ref = refs[7:11]
        refs = refs[11:]
    else:
        refs = refs[7:]
    q_o, k_o, vt_o, ckv_o, kr_o = refs
    zb = zb_ref[...]
    q = _mm(_rms(zb[:, :Q_LORA], gq_ref[...]), wq_ref[...])
    ckv = _rms(zb[:, Q_LORA:Q_LORA + KV_LORA], gkv_ref[...])
    kr = zb[:, Q_LORA + KV_LORA:]
    if rope:
        cq = jnp.concatenate([cq_ref[...]] * H_B, axis=1)
        sq = jnp.concatenate([sq_ref[...]] * H_B, axis=1)
        q = q * cq + _swap_pairs(q) * sq
        kr = kr * ck_ref[...] + _swap_pairs(kr) * sk_ref[...]
    q_o[...] = (q * Q_PRESCALE).astype(BF16)
    k_o[...] = (_mm(ckv, wk_ref[...]) + _mm(kr, pk_ref[...])).astype(BF16)
    vt_o[...] = _mm_nt(wvt_ref[...], ckv).astype(BF16)
    ckv_o[...] = ckv
    kr_o[...] = kr[:, :ROPE]


def _mla_prep(z, mp, rope_tabs, seq, tm=1024):
    n = z.shape[0]
    full = lambda a: pl.BlockSpec(a.shape, lambda i: (0,) * a.ndim)
    params = (mp["q_norm"], mp["kv_norm"], mp["wq"], mp["wk"], mp["wvt"], mp["pk"])
    in_specs = [pl.BlockSpec((tm, BRANCH_W), lambda i: (i, COL_ZB))] + [full(p) for p in params]
    args = [z, *params]
    rope = rope_tabs is not None
    if rope:
        per = seq // tm
        in_specs += [pl.BlockSpec((tm, LANES), lambda i: (i % per, 0))] * 4
        args += list(rope_tabs)
    rb = lambda w: pl.BlockSpec((tm, w), lambda i: (i, 0))
    return pl.pallas_call(
        functools.partial(_mla_prep_kernel, rope=rope),
        grid=(n // tm,),
        in_specs=in_specs,
        out_specs=[rb(H_B * LANES), rb(H_B * LANES), pl.BlockSpec((H_B * VDIM, tm), lambda i: (0, i)),
                   rb(KV_LORA), rb(ROPE)],
        out_shape=[jax.ShapeDtypeStruct((n, H_B * LANES), BF16),
                   jax.ShapeDtypeStruct((n, H_B * LANES), BF16),
                   jax.ShapeDtypeStruct((H_B * VDIM, n), BF16),
                   jax.ShapeDtypeStruct((n, KV_LORA), F32),
                   jax.ShapeDtypeStruct((n, ROPE), F32)],
        compiler_params=_cparams(("arbitrary",)),
        name="mla_prep",
    )(*args)


def _kv_expand_kernel(ckv_ref, kr_ref, wk_ref, wvt_ref, pk_ref, k_o, vt_o):
    ckv = ckv_ref[...]
    k_o[...] = (_mm(ckv, wk_ref[...]) + _mm(kr_ref[...], pk_ref[...])).astype(BF16)
    vt_o[...] = _mm_nt(wvt_ref[...], ckv).astype(BF16)


def _kv_expand(ckv, kr128, mp):
    m = ckv.shape[0]
    full = lambda a: pl.BlockSpec(a.shape, lambda i: (0,) * a.ndim)
    args = (ckv, kr128, mp["wk"], mp["wvt"], mp["pk"])
    return pl.pallas_call(
        _kv_expand_kernel,
        grid=(1,),
        in_specs=[full(a) for a in args],
        out_specs=[pl.BlockSpec((m, H_B * LANES), lambda i: (0, 0)),
                   pl.BlockSpec((H_B * VDIM, m), lambda i: (0, 0))],
        out_shape=[jax.ShapeDtypeStruct((m, H_B * LANES), BF16),
                   jax.ShapeDtypeStruct((H_B * VDIM, m), BF16)],
        compiler_params=_cparams(("arbitrary",)),
        name="mla_ctx_kv",
    )(*args)


def _attn_kernel(*refs, has_ctx):
    if has_ctx:
        q_ref, k_ref, vt_ref, kc_ref, vct_ref, g_ref, o_ref = refs
    else:
        q_ref, k_ref, vt_ref, g_ref, o_ref = refs
    nbs, seq = k_ref.shape[0], k_ref.shape[1]
    nt = (((1,), (1,)), ((), ()))
    for b in range(nbs):
        keys = slice(b * seq, (b + 1) * seq)
        outs = []
        for h0 in range(0, H_B, ATTN_GROUP):
            heads = range(h0, h0 + ATTN_GROUP)
            hs = [slice(h * LANES, (h + 1) * LANES) for h in heads]
            vh = [slice(h * VDIM, (h + 1) * VDIM) for h in heads]
            gi = range(ATTN_GROUP)
            s = [lax.dot_general(k_ref[b, :, hs[i]], q_ref[b, :, hs[i]], nt, preferred_element_type=F32)
                 for i in gi]
            m = [jnp.max(s[i], axis=0, keepdims=True) for i in gi]
            if has_ctx:
                sc = [lax.dot_general(kc_ref[:, hs[i]], q_ref[b, :, hs[i]], nt, preferred_element_type=F32)
                      for i in gi]
                m = [jnp.maximum(m[i], jnp.max(sc[i], axis=0, keepdims=True)) for i in gi]
            e = [jnp.exp2(s[i] - m[i]) for i in gi]
            den = [jnp.sum(e[i], axis=0, keepdims=True) for i in gi]
            o = [jnp.dot(vt_ref[vh[i], keys], e[i].astype(BF16), preferred_element_type=F32) for i in gi]
            if has_ctx:
                ec = [jnp.exp2(sc[i] - m[i]) for i in gi]
                den = [den[i] + jnp.sum(ec[i], axis=0, keepdims=True) for i in gi]
                o = [o[i] + jnp.dot(vct_ref[vh[i], :], ec[i].astype(BF16), preferred_element_type=F32)
                     for i in gi]
            outs += [o[i] / den[i] for i in gi]
        o_ref[b] = (jnp.concatenate(outs, axis=0).T * _silu(g_ref[b])).astype(BF16)


def _attention(z, q, k, vt, kc, vct, nb, seq):
    n = z.shape[0]
    tq = min(seq, ATTN_TQ)
    nq = seq // tq
    has_ctx = kc is not None
    nbs = 1 if has_ctx else max(1, min(nb, ATTN_ROWS // seq))
    z3 = z.reshape(nb, seq, ZW)
    in_specs = [pl.BlockSpec((nbs, tq, H_B * LANES), lambda b, i: (b, i, 0)),
                pl.BlockSpec((nbs, seq, H_B * LANES), lambda b, i: (b, 0, 0)),
                pl.BlockSpec((H_B * VDIM, nbs * seq), lambda b, i: (0, b))]
    args = [q.reshape(nb, seq, H_B * LANES), k.reshape(nb, seq, H_B * LANES), vt]
    if has_ctx:
        past = kc.shape[0] // nb
        in_specs += [pl.BlockSpec((past, H_B * LANES), lambda b, i: (b, 0)),
                     pl.BlockSpec((H_B * VDIM, past), lambda b, i: (0, b))]
        args += [kc, vct]
    in_specs.append(pl.BlockSpec((nbs, tq, BRANCH_W), lambda b, i: (b, i, COL_GB)))
    args.append(z3)
    out = pl.pallas_call(
        functools.partial(_attn_kernel, has_ctx=has_ctx),
        grid=(nb // nbs, nq),
        in_specs=in_specs,
        out_specs=pl.BlockSpec((nbs, tq, BRANCH_W), lambda b, i: (b, i, 0)),
        out_shape=jax.ShapeDtypeStruct((nb, seq, BRANCH_W), BF16),
        compiler_params=_cparams(("arbitrary", "arbitrary")),
        name="mla_attention",
    )(*args)
    return out.reshape(n, BRANCH_W)


def _gmlp_kernel(u_ref, vc_ref, g_ref, lng_ref, lnb_ref, ws_ref, bs_ref, o_ref):
    tm = u_ref.shape[0]
    u = jax.nn.gelu(u_ref[...])
    x = jax.nn.gelu(vc_ref[...])
    mu = jnp.mean(x, axis=-1, keepdims=True)
    xc = x - mu
    var = jnp.mean(xc * xc, axis=-1, keepdims=True)
    vn = (xc * lax.rsqrt(var + 1e-5) * lng_ref[...] + lnb_ref[...]).astype(BF16)
    rows = []
    for c in range(tm // CHUNK):
        cols = []
        for g in range(G_C):
            blk = vn[c * CHUNK:(c + 1) * CHUNK, g * LANES:(g + 1) * LANES]
            cols.append(jnp.dot(ws_ref[g], blk, preferred_element_type=F32))
        rows.append(jnp.concatenate(cols, axis=1) + bs_ref[...])
    mixed = jnp.concatenate(rows, axis=0)
    o_ref[...] = (u * mixed * _silu(g_ref[...])).astype(BF16)


def _gmlp(z, gp, tm=512):
    n = z.shape[0]
    zb = lambda c: pl.BlockSpec((tm, BRANCH_W), lambda i, c=c: (i, c))
    full = lambda a: pl.BlockSpec(a.shape, lambda i: (0,) * a.ndim)
    params = (gp["ln_g"], gp["ln_b"], gp["w_s"], gp["b_s"])
    return pl.pallas_call(
        _gmlp_kernel,
        grid=(n // tm,),
        in_specs=[zb(COL_U), zb(COL_VC), zb(COL_GC)] + [full(p) for p in params],
        out_specs=pl.BlockSpec((tm, BRANCH_W), lambda i: (i, 0)),
        out_shape=jax.ShapeDtypeStruct((n, BRANCH_W), BF16),
        compiler_params=_cparams(("arbitrary",)),
        name="gmlp",
    )(z, z, z, *params)


def _fnet_kernel(f_ref, g_ref, cs_ref, dft_ref, o_ref, xcs_scr, *, seq):
    r = pl.program_id(1)
    nbs = f_ref.shape[0]

    @pl.when(r == 0)
    def _():
        for b in range(nbs):
            for g in range(G_D):
                xg = _mm(f_ref[b, :, g * LANES:(g + 1) * LANES], cs_ref[...])
                xcs_scr[b, 0:seq, g * LANES:(g + 1) * LANES] = xg[:, :LANES].astype(BF16)
                xcs_scr[b, seq:2 * seq, g * LANES:(g + 1) * LANES] = xg[:, LANES:].astype(BF16)

    for b in range(nbs):
        y = jnp.dot(dft_ref[...], xcs_scr[b], preferred_element_type=F32)
        o_ref[b] = (y * (1.0 / math.sqrt(seq * LANES)) * _silu(g_ref[b])).astype(BF16)


def _fnet(z, cs128, dft, nb, seq):
    n = z.shape[0]
    tr = min(seq, FNET_ROWS)
    nr = seq // tr
    nbs = max(1, min(nb, FRONT_TM // seq))
    z3 = z.reshape(nb, seq, ZW)
    out = pl.pallas_call(
        functools.partial(_fnet_kernel, seq=seq),
        grid=(nb // nbs, nr),
        in_specs=[pl.BlockSpec((nbs, seq, BRANCH_W), lambda b, r: (b, 0, COL_F)),
                  pl.BlockSpec((nbs, tr, BRANCH_W), lambda b, r: (b, r, COL_GD)),
                  pl.BlockSpec((LANES, 2 * LANES), lambda b, r: (0, 0)),
                  pl.BlockSpec((tr, 2 * seq), lambda b, r: (r, 0))],
        out_specs=pl.BlockSpec((nbs, tr, BRANCH_W), lambda b, r: (b, r, 0)),
        out_shape=jax.ShapeDtypeStruct((nb, seq, BRANCH_W), BF16),
        scratch_shapes=[pltpu.VMEM((nbs, 2 * seq, BRANCH_W), BF16)],
        compiler_params=_cparams(("arbitrary", "arbitrary")),
        name="fnet",
    )(z3, z3, cs128, dft)
    return out.reshape(n, BRANCH_W)


def _back_kernel(x_ref, h_ref, gate_ref, oa_ref, ob_ref, oc_ref, od_ref, wm_ref, bm_ref,
                 wb_ref, wo_ref, fg_ref, y_ref, acc_scr, *, final):
    nb = pl.program_id(1)
    tm = x_ref.shape[0]
    chunks = [pl.ds(r, BACK_ROWS) for r in range(0, tm, BACK_ROWS)]

    for i, o_ref in enumerate((oa_ref, ob_ref, oc_ref, od_ref)):
        @pl.when(nb == i)
        def _(o_ref=o_ref, i=i):
            for rows in chunks:
                gates = jax.nn.sigmoid(jnp.dot(h_ref[rows, :], wm_ref[...], preferred_element_type=F32)
                                       + bm_ref[...])
                upd = gates * jnp.dot(o_ref[rows, :], wb_ref[0], preferred_element_type=F32)
                if i == 0:
                    acc_scr[rows, :] = upd
                else:
                    acc_scr[rows, :] += upd

    @pl.when(nb == 3)
    def _():
        for rows in chunks:
            xn = x_ref[rows, :] + gate_ref[0] * _mm(acc_scr[rows, :], wo_ref[...])
            if final:
                xn = _rms(xn, fg_ref[...])
            y_ref[rows, :] = xn


def _back(x2d, h, gate, o_a, o_b, o_c, o_d, wts, l, final_g, final, tm=1024):
    n = x2d.shape[0]
    rows_per_gate = FRONT_TM // tm
    ob = pl.BlockSpec((tm, BRANCH_W), lambda i, j: (i, 0))
    return pl.pallas_call(
        functools.partial(_back_kernel, final=final),
        grid=(n // tm, 4),
        in_specs=[pl.BlockSpec((tm, D_MODEL), lambda i, j: (i, 0)),
                  pl.BlockSpec((tm, D_MODEL), lambda i, j: (i, 0)),
                  pl.BlockSpec((1, 1, D_MODEL), lambda i, j: (i // rows_per_gate, 0, 0)),
                  ob, ob, ob, ob,
                  pl.BlockSpec((None, D_MODEL, D_MODEL), lambda i, j: (l, 0, j)),
                  pl.BlockSpec((None, 1, D_MODEL), lambda i, j: (l, 0, j)),
                  pl.BlockSpec((None, 1, BRANCH_W, D_MODEL), lambda i, j: (l, j, 0, 0)),
                  pl.BlockSpec((None, D_MODEL, D_MODEL), lambda i, j: (l, 0, 0)),
                  pl.BlockSpec((1, D_MODEL), lambda i, j: (0, 0))],
        out_specs=pl.BlockSpec((tm, D_MODEL), lambda i, j: (i, 0)),
        out_shape=jax.ShapeDtypeStruct((n, D_MODEL), F32),
        scratch_shapes=[pltpu.VMEM((tm, D_MODEL), F32)],
        compiler_params=_cparams(("arbitrary", "arbitrary")),
        name="back",
    )(x2d, h, gate, o_a, o_b, o_c, o_d, wts["w_merge"], wts["b_merge"], wts["w_branch"],
      wts["w_out"], final_g)


def _pad_cols_kernel(w_ref, o_ref):
    rows = w_ref.shape[0]
    o_ref[:, :SHIFT_W] = w_ref[:, :SHIFT_W].astype(BF16)
    o_ref[:, SHIFT_W:Z_SHIFT_PAD] = jnp.zeros((rows, Z_SHIFT_PAD - SHIFT_W), BF16)
    o_ref[:, Z_SHIFT_PAD:Z_SHIFT_PAD + 928] = w_ref[:, SHIFT_W:2656].astype(BF16)
    o_ref[:, Z_SHIFT_PAD + 928:3072] = jnp.zeros((rows, 3072 - Z_SHIFT_PAD - 928), BF16)
    o_ref[:, 3072:] = w_ref[:, 2656:].astype(BF16)


def _pad_cols(w, l, tm=256):
    _, k, n_in = w.shape
    return pl.pallas_call(
        _pad_cols_kernel,
        grid=(k // tm,),
        in_specs=[pl.BlockSpec((None, tm, n_in), lambda i: (l, i, 0))],
        out_specs=pl.BlockSpec((tm, ZW), lambda i: (i, 0)),
        out_shape=jax.ShapeDtypeStruct((k, ZW), BF16),
        compiler_params=_cparams(("arbitrary",)),
        name="pad_w_in",
    )(w)


def _rope_place():
    p = np.zeros((LANES, H_B * LANES), np.float32)
    for h in range(H_B):
        for j in range(ROPE):
            p[j, h * LANES + NOPE + j] = 1.0
    return jnp.asarray(p, BF16)


def _pack_q(w):
    w = w.reshape(Q_LORA, H_B, NOPE + ROPE)
    return jnp.pad(w, ((0, 0), (0, 0), (0, LANES - NOPE - ROPE))).reshape(Q_LORA, H_B * LANES)


def _pack_kv(w):
    w = w.reshape(KV_LORA, H_B, NOPE + VDIM)
    wk = jnp.pad(w[..., :NOPE], ((0, 0), (0, 0), (0, LANES - NOPE))).reshape(KV_LORA, H_B * LANES)
    return wk, w[..., NOPE:].reshape(KV_LORA, H_B * VDIM)


def _rope_tables(n_tokens):
    rows = n_tokens // GRID_W
    row = jnp.repeat(jnp.arange(rows, dtype=F32), GRID_W)
    col = jnp.tile(jnp.arange(GRID_W, dtype=F32), rows)
    n_freq = ROPE // 4
    inv = ROPE_BASE ** (-jnp.arange(n_freq, dtype=F32) / n_freq)
    ang = jnp.concatenate([row[:, None] * inv, col[:, None] * inv], axis=-1)
    cos = jnp.repeat(jnp.cos(ang), 2, axis=-1)
    sin = jnp.repeat(jnp.sin(ang), 2, axis=-1) * jnp.tile(jnp.asarray([-1.0, 1.0], F32), ROPE // 2)
    ones = lambda k: jnp.ones((n_tokens, k), F32)
    zeros = lambda k: jnp.zeros((n_tokens, k), F32)
    cq = jnp.concatenate([ones(NOPE), cos, ones(LANES - NOPE - ROPE)], axis=-1)
    sq = jnp.concatenate([zeros(NOPE), sin, zeros(LANES - NOPE - ROPE)], axis=-1)
    ck = jnp.concatenate([cos, ones(LANES - ROPE)], axis=-1)
    sk = jnp.concatenate([sin, zeros(LANES - ROPE)], axis=-1)
    return cq, sq, ck, sk


def _dft_tables(seq):
    k = np.arange(LANES)
    a = 2.0 * np.pi * ((k[:, None] * k[None, :]) % LANES) / LANES
    cs128 = np.concatenate([np.cos(a), np.sin(a)], axis=1)
    t = np.arange(seq)
    b = 2.0 * np.pi * ((t[:, None] * t[None, :]) % seq) / seq
    dft = np.concatenate([np.cos(b), -np.sin(b)], axis=1)
    return (jnp.asarray(cs128.astype(np.float32)).astype(BF16),
            jnp.asarray(dft.astype(np.float32)).astype(BF16))


def _lora_weights(w_up, a_up):
    z = jnp.zeros((LORA, BRANCH_W), F32)
    pad = jnp.zeros((256 - 3 * LORA, 2 * BRANCH_W), F32)
    wlt = jnp.concatenate([jnp.concatenate([w_up[0], z], 1), jnp.concatenate([z, w_up[1]], 1),
                           jnp.concatenate([z, z], 1), pad], axis=0)
    wla = jnp.concatenate([jnp.concatenate([z, z], 1), jnp.concatenate([z, z], 1),
                           jnp.concatenate([a_up[0], a_up[1]], 1), pad], axis=0)
    return wlt.astype(BF16), wla.astype(BF16)


def _to_blockdiag(s):
    b = s.shape[0]
    s = s.reshape(b, H_A // 2, 2, N_A, N_A)
    z = jnp.zeros_like(s[:, :, 0])
    top = jnp.concatenate([s[:, :, 0], z], axis=-1)
    bot = jnp.concatenate([z, s[:, :, 1]], axis=-1)
    return jnp.concatenate([top, bot], axis=-2)


def _trunk_layer(x2d, ss, lw, wts, l, nb, seq, rope_tabs, ctx, final_g):
    z, h = _front(x2d, ss[:, :2], lw["norm_g"], lw["w_in"], lw["mu"], seq)
    s0 = None
    if ctx is not None:
        s0 = jnp.stack([_to_blockdiag(ctx[0]), _to_blockdiag(ctx[1])], axis=1)
    o_a, sf, sb = _rwkv(z, lw["rwkv"], s0, nb, seq)
    q, k, v, ckv, kr = _mla_prep(z, lw["mla"], rope_tabs, seq)
    kc = vc = None
    if ctx is not None:
        past = ctx[2].shape[1]
        kr_c = jnp.pad(ctx[3].reshape(nb * past, ROPE), ((0, 0), (0, LANES - ROPE)))
        kc, vc = _kv_expand(ctx[2].reshape(nb * past, KV_LORA), kr_c, lw["mla"])
    o_b = _attention(z, q, k, v, kc, vc, nb, seq)
    o_c = _gmlp(z, lw["gmlp"])
    o_d = _fnet(z, *lw["dft"][seq], nb, seq)
    x_new = _back(x2d, h, ss[:, 2:], o_a, o_b, o_c, o_d, wts, l, final_g, l == DEPTH - 1)
    return x_new, sf, sb, ckv, kr


def kernel(x_prompt, x_sample, state_rwkv_fwd, state_rwkv_bwd, cache_mla_ckv, cache_mla_krope, c, c_ctx, norm_g, w_ada, b_ada, w_in, shift_mu, rwkv_w0, rwkv_w_up, rwkv_a0, rwkv_a_up, rwkv_k_k, rwkv_k_a, rwkv_r_k, rwkv_ln_g, rwkv_ln_b, mla_q_norm, mla_w_q_up, mla_kv_norm, mla_w_kv_up, gmlp_ln_g, gmlp_ln_b, gmlp_w_s, gmlp_b_s, w_branch, w_merge, b_merge, w_out, final_norm_g):
    nb_c, seq_c, _ = x_prompt.shape
    nb_l, seq_l, _ = x_sample.shape
    assert (nb_c * seq_c) % FRONT_TM == 0 and FRONT_TM % seq_c == 0 and seq_l == FRONT_TM

    cond8 = jnp.concatenate([c_ctx[None], c, jnp.zeros((8 - 1 - nb_l, D_MODEL), F32)], axis=0)
    mod = _modulation(cond8, w_ada, b_ada).reshape(DEPTH, 8, 3, D_MODEL)
    n_ctx_tiles = nb_c * seq_c // FRONT_TM

    pk = _rope_place()
    rope_tabs = _rope_tables(seq_l)
    dft = {s: _dft_tables(s) for s in {seq_c, seq_l}}
    mu_p = jnp.pad(shift_mu, ((0, 0), (0, Z_SHIFT_PAD - SHIFT_W)))[:, None]
    final_g = final_norm_g[None]
    wts = {"w_merge": w_merge.astype(BF16), "b_merge": b_merge[:, None], "w_branch": w_branch.astype(BF16),
           "w_out": w_out.astype(BF16)}

    xc = x_prompt.reshape(nb_c * seq_c, D_MODEL)
    xl = x_sample.reshape(nb_l * seq_l, D_MODEL)
    sf_list, sb_list, ckv_list, kr_list = [], [], [], []
    for l in range(DEPTH):
        wlt, wla = _lora_weights(rwkv_w_up[l], rwkv_a_up[l])
        wk, wv = _pack_kv(mla_w_kv_up[l])
        lw = {
            "norm_g": norm_g[l][None], "w_in": _pad_cols(w_in, l), "mu": mu_p[l],
            "rwkv": {
                "wlt": wlt, "wla": wla, "w0": rwkv_w0[l].reshape(1, 2 * BRANCH_W),
                "a0": rwkv_a0[l].reshape(1, 2 * BRANCH_W), "k_k": rwkv_k_k[l][None],
                "k_a": rwkv_k_a[l][None], "r_k": rwkv_r_k[l].reshape(1, BRANCH_W),
                "ln_g": rwkv_ln_g[l][None], "ln_b": rwkv_ln_b[l][None],
            },
            "mla": {
                "q_norm": mla_q_norm[l][None], "kv_norm": mla_kv_norm[l][None],
                "wq": _pack_q(mla_w_q_up[l]).astype(BF16), "wk": wk.astype(BF16),
                "wvt": wv.T.astype(BF16), "pk": pk,
            },
            "gmlp": {
                "ln_g": gmlp_ln_g[l][None], "ln_b": gmlp_ln_b[l][None],
                "w_s": gmlp_w_s[l].astype(BF16),
                "b_s": jnp.repeat(gmlp_b_s[l].T, BRANCH_W // G_C, axis=1),
            },
            "dft": dft,
        }
        ss_c = jnp.broadcast_to(mod[l, 0][None], (n_ctx_tiles, 3, D_MODEL))
        ss_l = mod[l, 1:1 + nb_l]
        xc, sf, sb, ckv, kr = _trunk_layer(xc, ss_c, lw, wts, l, nb_c, seq_c, None, None, final_g)
        sf_list.append(sf)
        sb_list.append(sb)
        ckv_list.append(ckv.reshape(nb_c, seq_c, KV_LORA))
        kr_list.append(kr.reshape(nb_c, seq_c, ROPE))
        ctx = (state_rwkv_fwd[:, l], state_rwkv_bwd[:, l], cache_mla_ckv[:, l], cache_mla_krope[:, l])
        xl = _trunk_layer(xl, ss_l, lw, wts, l, nb_l, seq_l, rope_tabs, ctx, final_g)[0]

    return (xc.reshape(nb_c, seq_c, D_MODEL), xl.reshape(nb_l, seq_l, D_MODEL),
            jnp.stack(sf_list, axis=1), jnp.stack(sb_list, axis=1),
            jnp.stack(ckv_list, axis=1), jnp.stack(kr_list, axis=1))
```

```python
import functools
import math

import jax
import jax.numpy as jnp
import numpy as np
from jax import lax
from jax.experimental import pallas as pl
from jax.experimental.pallas import tpu as pltpu

F32 = jnp.float32
BF16 = jnp.bfloat16

D_MODEL = 1024
DEPTH = 2
GRID_W = 64
BRANCH_W = 512
H_A = 8
N_A = 64
LORA = 64
RWKV_GN_EPS = 64e-5
H_B = 8
NOPE = 64
ROPE = 32
VDIM = 64
Q_LORA = 256
KV_LORA = 128
ROPE_BASE = 10000.0
Q_PRESCALE = (NOPE + ROPE) ** -0.5 * math.log2(math.e)
G_C = 4
CHUNK = 128
G_D = 4
NORM_EPS = 1e-6
SHIFT_W = 1728

LANES = 128
VMEM_LIMIT = 52 * 1024 * 1024

ZW = 6144
Z_SHIFT_PAD = 2048
COL_R, COL_K, COL_V = 0, 1, 2
COL_LORA = 6
COL_GA, COL_ZB, COL_GB, COL_U, COL_VC, COL_GC, COL_F, COL_GD = 4, 5, 6, 7, 8, 9, 10, 11
FRONT_TM = 2048
FRONT_TN = 1024
FRONT_COLS = 256
N_SHIFT_TILES = Z_SHIFT_PAD // FRONT_TN
SCAN_C = 64
SCAN_NB = 2
ATTN_GROUP = 4
ATTN_GROUP_LONG = 2
ATTN_TQ = 512
FNET_ROWS = 1024
ATTN_ROWS = 1024
BACK_ROWS = 256


def _mm(a, b):
    return jnp.dot(a.astype(BF16), b.astype(BF16), preferred_element_type=F32)


def _mm_nt(a, b):
    return lax.dot_general(a.astype(BF16), b.astype(BF16), (((1,), (1,)), ((), ())),
                           preferred_element_type=F32)


def _mm_split_lhs01(m01, x):
    hi = x.astype(BF16)
    lo = (x - hi.astype(F32)).astype(BF16)
    d = functools.partial(jnp.dot, preferred_element_type=F32)
    return d(m01, hi) + d(m01, lo)


def _silu(x):
    return x * jax.nn.sigmoid(x)


def _rms(x, g):
    return x * lax.rsqrt(jnp.mean(x * x, axis=-1, keepdims=True) + NORM_EPS) * g


def _cparams(sem, vmem=VMEM_LIMIT):
    return pltpu.CompilerParams(dimension_semantics=sem, vmem_limit_bytes=vmem)


def _mod_kernel(c_ref, w_ref, b_ref, o_ref):
    o_ref[0] = _mm(_silu(c_ref[...]), w_ref[0]) + b_ref[0]


def _modulation(cond8, w_ada, b_ada):
    return pl.pallas_call(
        _mod_kernel,
        grid=(DEPTH, 3),
        in_specs=[
            pl.BlockSpec((8, D_MODEL), lambda l, j: (0, 0)),
            pl.BlockSpec((1, D_MODEL, D_MODEL), lambda l, j: (l, 0, j)),
            pl.BlockSpec((1, 1, D_MODEL), lambda l, j: (l, 0, j)),
        ],
        out_specs=pl.BlockSpec((1, 8, D_MODEL), lambda l, j: (l, 0, j)),
        out_shape=jax.ShapeDtypeStruct((DEPTH, 8, 3 * D_MODEL), F32),
        compiler_params=_cparams(("arbitrary", "arbitrary")),
        name="modulation",
    )(cond8, w_ada, b_ada.reshape(DEPTH, 1, 3 * D_MODEL))


def _front_kernel(x_ref, ss_ref, g_ref, w_ref, mu_ref, z_ref, h_ref, *, seq):
    j = pl.program_id(1)

    @pl.when(j == 0)
    def _():
        h = _rms(x_ref[...], g_ref[...]) * (1.0 + ss_ref[0, 1:2, :]) + ss_ref[0, 0:1, :]
        h_ref[...] = h.astype(BF16)

    cols = [slice(c, c + FRONT_COLS) for c in range(0, FRONT_TN, FRONT_COLS)]

    @pl.when(j < N_SHIFT_TILES)
    def _():
        first = lax.broadcasted_iota(jnp.int32, (8, FRONT_COLS), 0) == 0
        last = lax.broadcasted_iota(jnp.int32, (8, FRONT_COLS), 0) == 7
        for cs in cols:
            z = jnp.dot(h_ref[...], w_ref[:, cs], preferred_element_type=F32)
            hmu = 0.5 * mu_ref[:, cs]
            omu = 1.0 - mu_ref[:, cs]
            prev = pltpu.roll(z, 1, axis=0)
            nxt = pltpu.roll(z, FRONT_TM - 1, axis=0)
            z_ref[:, cs] = z * omu + (prev + nxt) * hmu
            for s0 in range(0, FRONT_TM, seq):
                a = slice(s0, s0 + 8)
                z_ref[a, cs] = z[a] * omu + (jnp.where(first, 0.0, prev[a]) + nxt[a]) * hmu
                b = slice(s0 + seq - 8, s0 + seq)
                z_ref[b, cs] = z[b] * omu + (prev[b] + jnp.where(last, 0.0, nxt[b])) * hmu

    @pl.when(j >= N_SHIFT_TILES)
    def _():
        for cs in cols:
            z_ref[:, cs] = jnp.dot(h_ref[...], w_ref[:, cs], preferred_element_type=F32)


def _front(x2d, ss, norm_g, w_in_p, mu_p, seq):
    n = x2d.shape[0]
    return pl.pallas_call(
        functools.partial(_front_kernel, seq=seq),
        grid=(n // FRONT_TM, ZW // FRONT_TN),
        in_specs=[
            pl.BlockSpec((FRONT_TM, D_MODEL), lambda i, j: (i, 0)),
            pl.BlockSpec((1, 2, D_MODEL), lambda i, j: (i, 0, 0)),
            pl.BlockSpec((1, D_MODEL), lambda i, j: (0, 0)),
            pl.BlockSpec((D_MODEL, FRONT_TN), lambda i, j: (0, j)),
            pl.BlockSpec((1, FRONT_TN), lambda i, j: (0, jnp.minimum(j, N_SHIFT_TILES - 1))),
        ],
        out_specs=[pl.BlockSpec((FRONT_TM, FRONT_TN), lambda i, j: (i, j)),
                   pl.BlockSpec((FRONT_TM, D_MODEL), lambda i, j: (i, 0))],
        out_shape=[jax.ShapeDtypeStruct((n, ZW), F32), jax.ShapeDtypeStruct((n, D_MODEL), BF16)],
        compiler_params=_cparams(("arbitrary", "arbitrary")),
        name="front",
    )(x2d, ss, norm_g, w_in_p, mu_p)


def _scan_chunks(dirs, s_ref):
    c = SCAN_C
    n2 = 2 * c
    ri = lax.broadcasted_iota(jnp.int32, (c, c), 0)
    ci = lax.broadcasted_iota(jnp.int32, (c, c), 1)
    row = lax.broadcasted_iota(jnp.int32, (n2, n2), 0)
    col = lax.broadcasted_iota(jnp.int32, (n2, n2), 1)
    same64 = (row >> 6) == (col >> 6)
    same16 = (row >> 4) == (col >> 4)
    tl = row & (c - 1)
    il = col & (c - 1)
    eye = (row == col).astype(F32)
    lane_a = lax.broadcasted_iota(jnp.int32, (c, LANES), 1) < N_A

    def stack2(x):
        return jnp.concatenate([jnp.where(lane_a, x, 0.0), jnp.where(lane_a, 0.0, x)], axis=0)

    lhs, rhs, v2, bk, gtot, strict, incl, key = [], [], [], [], [], [], [], []
    for r, kk, v, lw, kt, bv, rev, j, d in dirs:
        tri = ((ri <= ci) if rev else (ri >= ci)).astype(BF16)
        cs = _mm_split_lhs01(tri, lw)
        tot = cs[0:1] if rev else cs[c - 1:c]
        g_tot = jnp.exp(tot)
        a_t = -kk * jnp.exp(cs - lw)
        r_t = r * jnp.exp(cs)
        g_inv = jnp.exp(-cs)
        b_t = bv * g_inv
        k_t = kt * g_inv
        g_rem = jnp.exp(tot - cs)
        b_h = bv * g_rem
        k_h = kt * g_rem
        st = same64 & ((il > tl) if rev else (il < tl))
        inc = same64 & ((il >= tl) if rev else (il <= tl))
        for p in range(H_A // 2):
            sl = slice(p * LANES, (p + 1) * LANES)
            lhs.append(jnp.concatenate([stack2(a_t[:, sl]), stack2(r_t[:, sl])], axis=0))
            rhs.append(jnp.concatenate([b_t[:, sl], k_t[:, sl]], axis=0))
            v2.append(stack2(v[:, sl]))
            bk.append(jnp.concatenate([stack2(b_h[:, sl]), stack2(k_h[:, sl])], axis=0))
            gtot.append(g_tot[:, sl])
            strict.append(st)
            incl.append(inc)
            key.append((j, d, p))
    ch = range(len(key))

    s = [s_ref[key[i]] for i in ch]
    big = [_mm_nt(lhs[i], jnp.concatenate([rhs[i], s[i]], axis=0)) for i in ch]
    lab, lak, mrbk = [], [], []
    for i in ch:
        x = big[i][:n2, :n2]
        xr = pltpu.roll(x, c, axis=1)
        y = big[i][n2:, :n2]
        yr = pltpu.roll(y, c, axis=1)
        pick = lambda top, bot: jnp.concatenate([top[:c], bot[c:]], axis=0)
        lab.append(jnp.where(strict[i], pick(x, xr), 0.0))
        lak.append(jnp.where(strict[i], pick(xr, x), 0.0))
        mrbk.append(jnp.concatenate([jnp.where(incl[i], pick(y, yr), 0.0),
                                     jnp.where(incl[i], pick(yr, y), 0.0)], axis=1))
    dg = [jnp.where(same16, lab[i], 0.0) for i in ch]
    off = [lab[i] - dg[i] for i in ch]
    pinv = [eye + dg[i] for i in ch]
    pw = [_mm(dg[i], dg[i]) for i in ch]
    for _ in range(2):
        t = [_mm(pw[i], jnp.concatenate([pw[i], pinv[i]], axis=1)) for i in ch]
        pinv = [pinv[i] + t[i][:, n2:] for i in ch]
        pw = [t[i][:, :n2] for i in ch]
    pinv = [pinv[i] + _mm(pw[i], pinv[i]) for i in ch]
    f = [_mm(pinv[i], off[i]) for i in ch]
    t = [_mm(f[i], jnp.concatenate([f[i], pinv[i]], axis=1)) for i in ch]
    g = [pinv[i] + t[i][:, n2:] for i in ch]
    tinv = [g[i] + _mm(t[i][:, :n2], g[i]) for i in ch]
    lv = [_mm(lak[i], v2[i]) for i in ch]
    u2 = [_mm(tinv[i], big[i][:n2, n2:] + lv[i]) for i in ch]
    uv = [jnp.concatenate([u2[i], v2[i]], axis=0) for i in ch]
    y2 = [big[i][n2:, n2:] + _mm(mrbk[i], uv[i]) for i in ch]
    for i in ch:
        s_ref[key[i]] = s[i] * gtot[i] + _mm(uv[i].T, bk[i])
    npair = H_A // 2
    return [jnp.concatenate([y2[g * npair + p][:c] + y2[g * npair + p][c:] for p in range(npair)], axis=1)
            for g in range(len(dirs))]


def _head_sums(x):
    lane_a = lax.broadcasted_iota(jnp.int32, (x.shape[0], LANES), 1) < N_A
    outs = []
    for p in range(H_A // 2):
        xs = x[:, p * LANES:(p + 1) * LANES]
        sa = jnp.sum(jnp.where(lane_a, xs, 0.0), axis=-1, keepdims=True)
        sb = jnp.sum(jnp.where(lane_a, 0.0, xs), axis=-1, keepdims=True)
        outs.append(jnp.where(lane_a, sa, sb))
    return jnp.concatenate(outs, axis=1)


def _rwkv_prepare(k, lora, d, w):
    wlt_ref, wla_ref, w0_ref, a0_ref, kk_ref, ka_ref, rk_ref = w
    hs = slice(d * BRANCH_W, (d + 1) * BRANCH_W)
    lo_w, lo_a = lora[:, :LANES], lora[:, LANES:]
    pre = w0_ref[:, hs] + _mm(jnp.tanh(lo_w), wlt_ref[:LANES, hs])
    lw = -math.exp(-0.5) * jax.nn.sigmoid(pre)
    kkraw = k * kk_ref[...]
    kk = kkraw * lax.rsqrt(jnp.maximum(_head_sums(kkraw * kkraw), 1e-24))
    k_a = ka_ref[...]
    if d:
        a_b = jax.nn.sigmoid(a0_ref[:, hs] + _mm(lo_a, wla_ref[LANES:, hs]))
        return kk, lw, k * (1.0 + (a_b - 1.0) * k_a), kk * a_b
    a2 = jax.nn.sigmoid(a0_ref[...] + _mm(lo_a, wla_ref[LANES:, :]))
    ktf = k * (1.0 + (a2[:, :BRANCH_W] - 1.0) * k_a)
    ktb = k * (1.0 + (a2[:, BRANCH_W:] - 1.0) * k_a)
    return kk, lw, ktf, kk * a2[:, hs], (ktf + ktb) * rk_ref[...]


def _rwkv_kernel(*refs, has_init):
    if has_init:
        s0_ref, refs = refs[0], refs[1:]
    cur = (refs[0:5], refs[5:10])
    nxt = (refs[10:12], refs[12:14])
    w = refs[14:21]
    lng_ref, lnb_ref = refs[21:23]
    o_ref, sf_ref, sb_ref, s_scr, ysum_scr, kts_scr, p_scr = refs[23:30]
    c = pl.program_id(1)
    nc = pl.num_programs(1)

    @pl.when(c == 0)
    def _():
        if has_init:
            s_scr[...] = s0_ref[...]
        else:
            s_scr[...] = jnp.zeros_like(s_scr)

    @pl.when((c == 0) & (pl.program_id(0) == 0))
    def _():
        for j in range(SCAN_NB):
            for d in range(2):
                for i, a in enumerate(_rwkv_prepare(cur[d][3][j], cur[d][4][j], d, w)):
                    p_scr[j, d, i] = a

    dirs = []
    for j in range(SCAN_NB):
        for d in range(2):
            kk, lw, kt, bv = [p_scr[j, d, i] for i in range(4)]
            dirs.append((cur[d][0][j], kk, cur[d][1][j], lw, kt, bv, d == 1, j, d))
    kts = [p_scr[j, 0, 4] for j in range(SCAN_NB)]
    ys = _scan_chunks(dirs, s_scr)

    for j in range(SCAN_NB):
        for d in range(2):
            for i, a in enumerate(_rwkv_prepare(nxt[d][0][j], nxt[d][1][j], d, w)):
                p_scr[j, d, i] = a

    rows = (pl.multiple_of(c * SCAN_C, SCAN_C), pl.multiple_of((nc - 1 - c) * SCAN_C, SCAN_C))

    @pl.when(c < nc // 2)
    def _():
        for j in range(SCAN_NB):
            kts_scr[j, pl.ds(rows[0], SCAN_C), :] = kts[j]
            for d in range(2):
                ysum_scr[j, pl.ds(rows[d], SCAN_C), :] = ys[2 * j + d]

    @pl.when(c >= nc // 2)
    def _():
        for j in range(SCAN_NB):
            for d in range(2):
                r, v, g = cur[d][0][j], cur[d][1][j], cur[d][2][j]
                bonus = _head_sums(r * (kts_scr[j, pl.ds(rows[1], SCAN_C), :] if d else kts[j])) * v
                o = ysum_scr[j, pl.ds(rows[d], SCAN_C), :] + ys[2 * j + d]
                dlt = o - _head_sums(o) * (1.0 / N_A)
                var = _head_sums(dlt * dlt) * (1.0 / N_A)
                y = dlt * lax.rsqrt(var + RWKV_GN_EPS) * lng_ref[...] + lnb_ref[...] + bonus
                o_ref[j, pl.ds(rows[d], SCAN_C), :] = (y * _silu(g)).astype(BF16)

    @pl.when(c == nc - 1)
    def _():
        for j in range(SCAN_NB):
            for d, st_ref in enumerate((sf_ref, sb_ref)):
                for p in range(H_A // 2):
                    s = s_scr[j, d, p]
                    st_ref[j, 2 * p] = s[:N_A, :N_A]
                    st_ref[j, 2 * p + 1] = pltpu.roll(s, N_A, axis=1)[N_A:, :N_A]


def _rwkv(z, wp, s0_bd, nb, seq):
    n = z.shape[0]
    nc = seq // SCAN_C
    assert nc % 2 == 0 and nb % SCAN_NB == 0
    z3 = z.reshape(nb, seq, ZW)
    chunk = ((lambda c: c), (lambda c: nc - 1 - c))
    spec = lambda idx, col, w=BRANCH_W: pl.BlockSpec((SCAN_NB, SCAN_C, w), lambda b, c: (b, idx(c), col))
    in_specs = []
    for d in range(2):
        in_specs += [spec(chunk[d], COL_R), spec(chunk[d], COL_V), spec(chunk[d], COL_GA),
                     spec(chunk[d], COL_K), spec(chunk[d], COL_LORA, 256)]

    def spec_next(d, col, w=BRANCH_W):
        def index(b, c):
            wrap = c == nc - 1
            nb_i = jnp.minimum(b + wrap.astype(jnp.int32), nb // SCAN_NB - 1)
            return (nb_i, chunk[d](jnp.where(wrap, 0, c + 1)), col)
        return pl.BlockSpec((SCAN_NB, SCAN_C, w), index)

    for d in range(2):
        in_specs += [spec_next(d, COL_K), spec_next(d, COL_LORA, 256)]
    full = lambda a: pl.BlockSpec(a.shape, lambda b, c: (0,) * a.ndim)
    params = (wp["wlt"], wp["wla"], wp["w0"], wp["a0"], wp["k_k"], wp["k_a"], wp["r_k"], wp["ln_g"], wp["ln_b"])
    in_specs += [full(p) for p in params]
    args = [z3] * 14 + list(params)
    sspec = pl.BlockSpec((SCAN_NB, 2, H_A // 2, LANES, LANES), lambda b, c: (b, 0, 0, 0, 0))
    has_init = s0_bd is not None
    if has_init:
        in_specs = [sspec] + in_specs
        args = [s0_bd] + args
    fspec = pl.BlockSpec((SCAN_NB, H_A, N_A, N_A), lambda b, c: (b, 0, 0, 0))
    o_a, sf, sb = pl.pallas_call(
        functools.partial(_rwkv_kernel, has_init=has_init),
        grid=(nb // SCAN_NB, nc),
        in_specs=in_specs,
        out_specs=[pl.BlockSpec((SCAN_NB, seq, BRANCH_W), lambda b, c: (b, 0, 0)), fspec, fspec],
        out_shape=[jax.ShapeDtypeStruct((nb, seq, BRANCH_W), BF16),
                   jax.ShapeDtypeStruct((nb, H_A, N_A, N_A), F32),
                   jax.ShapeDtypeStruct((nb, H_A, N_A, N_A), F32)],
        scratch_shapes=[pltpu.VMEM((SCAN_NB, 2, H_A // 2, LANES, LANES), F32),
                        pltpu.VMEM((SCAN_NB, seq, BRANCH_W), F32),
                        pltpu.VMEM((SCAN_NB, seq, BRANCH_W), F32),
                        pltpu.VMEM((SCAN_NB, 2, 5, SCAN_C, BRANCH_W), F32)],
        compiler_params=_cparams(("arbitrary", "arbitrary")),
        name="rwkv",
    )(*args)
    return o_a.reshape(n, BRANCH_W), sf, sb


def _swap_pairs(x):
    w = x.shape[1]
    even = (lax.broadcasted_iota(jnp.int32, x.shape, 1) & 1) == 0
    return jnp.where(even, pltpu.roll(x, w - 1, axis=1), pltpu.roll(x, 1, axis=1))


def _mla_prep_kernel(*refs, rope):
    zb_ref, gq_ref, gkv_ref, wq_ref, wk_ref, wvt_ref, pk_ref = refs[:7]
    if rope:
        cq_ref, sq_ref, ck_ref, sk_ref = refs[7:11]
        refs = refs[11:]
    else:
        refs = refs[7:]
    q_o, k_o, vt_o, ckv_o, kr_o = refs
    zb = zb_ref[...]
    q = _mm(_rms(zb[:, :Q_LORA], gq_ref[...]), wq_ref[...])
    ckv = _rms(zb[:, Q_LORA:Q_LORA + KV_LORA], gkv_ref[...])
    kr = zb[:, Q_LORA + KV_LORA:]
    if rope:
        cq = jnp.concatenate([cq_ref[...]] * H_B, axis=1)
        sq = jnp.concatenate([sq_ref[...]] * H_B, axis=1)
        q = q * cq + _swap_pairs(q) * sq
        kr = kr * ck_ref[...] + _swap_pairs(kr) * sk_ref[...]
    q_o[...] = (q * Q_PRESCALE).astype(BF16)
    k_o[...] = (_mm(ckv, wk_ref[...]) + _mm(kr, pk_ref[...])).astype(BF16)
    vt_o[...] = _mm_nt(wvt_ref[...], ckv).astype(BF16)
    ckv_o[...] = ckv
    kr_o[...] = kr[:, :ROPE]


def _mla_prep(z, mp, rope_tabs, seq, tm=1024):
    n = z.shape[0]
    full = lambda a: pl.BlockSpec(a.shape, lambda i: (0,) * a.ndim)
    params = (mp["q_norm"], mp["kv_norm"], mp["wq"], mp["wk"], mp["wvt"], mp["pk"])
    in_specs = [pl.BlockSpec((tm, BRANCH_W), lambda i: (i, COL_ZB))] + [full(p) for p in params]
    args = [z, *params]
    rope = rope_tabs is not None
    if rope:
        per = seq // tm
        in_specs += [pl.BlockSpec((tm, LANES), lambda i: (i % per, 0))] * 4
        args += list(rope_tabs)
    rb = lambda w: pl.BlockSpec((tm, w), lambda i: (i, 0))
    return pl.pallas_call(
        functools.partial(_mla_prep_kernel, rope=rope),
        grid=(n // tm,),
        in_specs=in_specs,
        out_specs=[rb(H_B * LANES), rb(H_B * LANES), pl.BlockSpec((H_B * VDIM, tm), lambda i: (0, i)),
                   rb(KV_LORA), rb(ROPE)],
        out_shape=[jax.ShapeDtypeStruct((n, H_B * LANES), BF16),
                   jax.ShapeDtypeStruct((n, H_B * LANES), BF16),
                   jax.ShapeDtypeStruct((H_B * VDIM, n), BF16),
                   jax.ShapeDtypeStruct((n, KV_LORA), F32),
                   jax.ShapeDtypeStruct((n, ROPE), F32)],
        compiler_params=_cparams(("arbitrary",)),
        name="mla_prep",
    )(*args)


def _kv_expand_kernel(ckv_ref, kr_ref, wk_ref, wvt_ref, pk_ref, k_o, vt_o):
    ckv = ckv_ref[...]
    k_o[...] = (_mm(ckv, wk_ref[...]) + _mm(kr_ref[...], pk_ref[...])).astype(BF16)
    vt_o[...] = _mm_nt(wvt_ref[...], ckv).astype(BF16)


def _kv_expand(ckv, kr128, mp):
    m = ckv.shape[0]
    full = lambda a: pl.BlockSpec(a.shape, lambda i: (0,) * a.ndim)
    args = (ckv, kr128, mp["wk"], mp["wvt"], mp["pk"])
    return pl.pallas_call(
        _kv_expand_kernel,
        grid=(1,),
        in_specs=[full(a) for a in args],
        out_specs=[pl.BlockSpec((m, H_B * LANES), lambda i: (0, 0)),
                   pl.BlockSpec((H_B * VDIM, m), lambda i: (0, 0))],
        out_shape=[jax.ShapeDtypeStruct((m, H_B * LANES), BF16),
                   jax.ShapeDtypeStruct((H_B * VDIM, m), BF16)],
        compiler_params=_cparams(("arbitrary",)),
        name="mla_ctx_kv",
    )(*args)


def _attn_kernel(*refs, has_ctx, group):
    if has_ctx:
        q_ref, k_ref, vt_ref, kc_ref, vct_ref, g_ref, o_ref = refs
    else:
        q_ref, k_ref, vt_ref, g_ref, o_ref = refs
    nbs, seq = k_ref.shape[0], k_ref.shape[1]
    nt = (((1,), (1,)), ((), ()))
    groups = [(b, h0) for b in range(nbs) for h0 in range(0, H_B, group)]

    def scores(b, h0):
        out = []
        for h in range(h0, h0 + group):
            hs = slice(h * LANES, (h + 1) * LANES)
            s = lax.dot_general(k_ref[b, :, hs], q_ref[b, :, hs], nt, preferred_element_type=F32)
            sc = None
            if has_ctx:
                sc = lax.dot_general(kc_ref[:, hs], q_ref[b, :, hs], nt, preferred_element_type=F32)
            out.append((s, sc))
        return out

    pending = scores(*groups[0])
    outs = []
    for gi, (b, h0) in enumerate(groups):
        cur = pending
        if gi + 1 < len(groups):
            pending = scores(*groups[gi + 1])
        keys = slice(b * seq, (b + 1) * seq)
        m, e, ec, den = [], [], [], []
        for s, sc in cur:
            mi = jnp.max(s, axis=0, keepdims=True)
            if has_ctx:
                mi = jnp.maximum(mi, jnp.max(sc, axis=0, keepdims=True))
            m.append(mi)
        for (s, sc), mi in zip(cur, m):
            ei = jnp.exp2(s - mi)
            di = jnp.sum(ei, axis=0, keepdims=True)
            e.append(ei.astype(BF16))
            if has_ctx:
                eci = jnp.exp2(sc - mi)
                di = di + jnp.sum(eci, axis=0, keepdims=True)
                ec.append(eci.astype(BF16))
            den.append(di)
        for i in range(group):
            vh = slice((h0 + i) * VDIM, (h0 + i + 1) * VDIM)
            o = jnp.dot(vt_ref[vh, keys], e[i], preferred_element_type=F32)
            if has_ctx:
                o = o + jnp.dot(vct_ref[vh, :], ec[i], preferred_element_type=F32)
            outs.append(o / den[i])
        if h0 + group == H_B:
            o_ref[b] = (jnp.concatenate(outs, axis=0).T * _silu(g_ref[b])).astype(BF16)
            outs = []


def _attention(z, q, k, vt, kc, vct, nb, seq):
    n = z.shape[0]
    tq = min(seq, ATTN_TQ)
    nq = seq // tq
    has_ctx = kc is not None
    nbs = 1 if has_ctx else max(1, min(nb, ATTN_ROWS // seq))
    z3 = z.reshape(nb, seq, ZW)
    in_specs = [pl.BlockSpec((nbs, tq, H_B * LANES), lambda b, i: (b, i, 0)),
                pl.BlockSpec((nbs, seq, H_B * LANES), lambda b, i: (b, 0, 0)),
                pl.BlockSpec((H_B * VDIM, nbs * seq), lambda b, i: (0, b))]
    args = [q.reshape(nb, seq, H_B * LANES), k.reshape(nb, seq, H_B * LANES), vt]
    if has_ctx:
        past = kc.shape[0] // nb
        in_specs += [pl.BlockSpec((past, H_B * LANES), lambda b, i: (b, 0)),
                     pl.BlockSpec((H_B * VDIM, past), lambda b, i: (0, b))]
        args += [kc, vct]
    in_specs.append(pl.BlockSpec((nbs, tq, BRANCH_W), lambda b, i: (b, i, COL_GB)))
    args.append(z3)
    out = pl.pallas_call(
        functools.partial(_attn_kernel, has_ctx=has_ctx, group=ATTN_GROUP_LONG if has_ctx else ATTN_GROUP),
        grid=(nb // nbs, nq),
        in_specs=in_specs,
        out_specs=pl.BlockSpec((nbs, tq, BRANCH_W), lambda b, i: (b, i, 0)),
        out_shape=jax.ShapeDtypeStruct((nb, seq, BRANCH_W), BF16),
        compiler_params=_cparams(("arbitrary", "arbitrary")),
        name="mla_attention",
    )(*args)
    return out.reshape(n, BRANCH_W)


def _gmlp_kernel(u_ref, vc_ref, g_ref, lng_ref, lnb_ref, ws_ref, bs_ref, o_ref):
    tm = u_ref.shape[0]
    u = jax.nn.gelu(u_ref[...])
    x = jax.nn.gelu(vc_ref[...])
    mu = jnp.mean(x, axis=-1, keepdims=True)
    xc = x - mu
    var = jnp.mean(xc * xc, axis=-1, keepdims=True)
    vn = (xc * lax.rsqrt(var + 1e-5) * lng_ref[...] + lnb_ref[...]).astype(BF16)
    rows = []
    for c in range(tm // CHUNK):
        cols = []
        for g in range(G_C):
            blk = vn[c * CHUNK:(c + 1) * CHUNK, g * LANES:(g + 1) * LANES]
            cols.append(jnp.dot(ws_ref[g], blk, preferred_element_type=F32))
        rows.append(jnp.concatenate(cols, axis=1) + bs_ref[...])
    mixed = jnp.concatenate(rows, axis=0)
    o_ref[...] = (u * mixed * _silu(g_ref[...])).astype(BF16)


def _gmlp(z, gp, tm=512):
    n = z.shape[0]
    zb = lambda c: pl.BlockSpec((tm, BRANCH_W), lambda i, c=c: (i, c))
    full = lambda a: pl.BlockSpec(a.shape, lambda i: (0,) * a.ndim)
    params = (gp["ln_g"], gp["ln_b"], gp["w_s"], gp["b_s"])
    return pl.pallas_call(
        _gmlp_kernel,
        grid=(n // tm,),
        in_specs=[zb(COL_U), zb(COL_VC), zb(COL_GC)] + [full(p) for p in params],
        out_specs=pl.BlockSpec((tm, BRANCH_W), lambda i: (i, 0)),
        out_shape=jax.ShapeDtypeStruct((n, BRANCH_W), BF16),
        compiler_params=_cparams(("arbitrary",)),
        name="gmlp",
    )(z, z, z, *params)


def _fnet_kernel(f_ref, g_ref, cs_ref, dft_ref, o_ref, xcs_scr, *, seq):
    r = pl.program_id(1)
    nbs = f_ref.shape[0]

    @pl.when(r == 0)
    def _():
        for b in range(nbs):
            for g in range(G_D):
                xg = _mm(f_ref[b, :, g * LANES:(g + 1) * LANES], cs_ref[...])
                xcs_scr[b, 0:seq, g * LANES:(g + 1) * LANES] = xg[:, :LANES].astype(BF16)
                xcs_scr[b, seq:2 * seq, g * LANES:(g + 1) * LANES] = xg[:, LANES:].astype(BF16)

    for b in range(nbs):
        y = jnp.dot(dft_ref[...], xcs_scr[b], preferred_element_type=F32)
        o_ref[b] = (y * (1.0 / math.sqrt(seq * LANES)) * _silu(g_ref[b])).astype(BF16)


def _fnet(z, cs128, dft, nb, seq):
    n = z.shape[0]
    tr = min(seq, FNET_ROWS)
    nr = seq // tr
    nbs = max(1, min(nb, FRONT_TM // seq))
    z3 = z.reshape(nb, seq, ZW)
    out = pl.pallas_call(
        functools.partial(_fnet_kernel, seq=seq),
        grid=(nb // nbs, nr),
        in_specs=[pl.BlockSpec((nbs, seq, BRANCH_W), lambda b, r: (b, 0, COL_F)),
                  pl.BlockSpec((nbs, tr, BRANCH_W), lambda b, r: (b, r, COL_GD)),
                  pl.BlockSpec((LANES, 2 * LANES), lambda b, r: (0, 0)),
                  pl.BlockSpec((tr, 2 * seq), lambda b, r: (r, 0))],
        out_specs=pl.BlockSpec((nbs, tr, BRANCH_W), lambda b, r: (b, r, 0)),
        out_shape=jax.ShapeDtypeStruct((nb, seq, BRANCH_W), BF16),
        scratch_shapes=[pltpu.VMEM((nbs, 2 * seq, BRANCH_W), BF16)],
        compiler_params=_cparams(("arbitrary", "arbitrary")),
        name="fnet",
    )(z3, z3, cs128, dft)
    return out.reshape(n, BRANCH_W)


def _back_kernel(x_ref, h_ref, gate_ref, oa_ref, ob_ref, oc_ref, od_ref, wm_ref, bm_ref,
                 wb_ref, wo_ref, fg_ref, y_ref, acc_scr, *, final):
    nb = pl.program_id(1)
    tm = x_ref.shape[0]
    chunks = [pl.ds(r, BACK_ROWS) for r in range(0, tm, BACK_ROWS)]

    for i, o_ref in enumerate((oa_ref, ob_ref, oc_ref, od_ref)):
        @pl.when(nb == i)
        def _(o_ref=o_ref, i=i):
            for rows in chunks:
                gates = jax.nn.sigmoid(jnp.dot(h_ref[rows, :], wm_ref[...], preferred_element_type=F32)
                                       + bm_ref[...])
                upd = gates * jnp.dot(o_ref[rows, :], wb_ref[0], preferred_element_type=F32)
                if i == 0:
                    acc_scr[rows, :] = upd
                else:
                    acc_scr[rows, :] += upd

    @pl.when(nb == 3)
    def _():
        for rows in chunks:
            xn = x_ref[rows, :] + gate_ref[0] * _mm(acc_scr[rows, :], wo_ref[...])
            if final:
                xn = _rms(xn, fg_ref[...])
            y_ref[rows, :] = xn


def _back(x2d, h, gate, o_a, o_b, o_c, o_d, wts, l, final_g, final, tm=1024):
    n = x2d.shape[0]
    rows_per_gate = FRONT_TM // tm
    ob = pl.BlockSpec((tm, BRANCH_W), lambda i, j: (i, 0))
    return pl.pallas_call(
        functools.partial(_back_kernel, final=final),
        grid=(n // tm, 4),
        in_specs=[pl.BlockSpec((tm, D_MODEL), lambda i, j: (i, 0)),
                  pl.BlockSpec((tm, D_MODEL), lambda i, j: (i, 0)),
                  pl.BlockSpec((1, 1, D_MODEL), lambda i, j: (i // rows_per_gate, 0, 0)),
                  ob, ob, ob, ob,
                  pl.BlockSpec((None, D_MODEL, D_MODEL), lambda i, j: (l, 0, j)),
                  pl.BlockSpec((None, 1, D_MODEL), lambda i, j: (l, 0, j)),
                  pl.BlockSpec((None, 1, BRANCH_W, D_MODEL), lambda i, j: (l, j, 0, 0)),
                  pl.BlockSpec((None, D_MODEL, D_MODEL), lambda i, j: (l, 0, 0)),
                  pl.BlockSpec((1, D_MODEL), lambda i, j: (0, 0))],
        out_specs=pl.BlockSpec((tm, D_MODEL), lambda i, j: (i, 0)),
        out_shape=jax.ShapeDtypeStruct((n, D_MODEL), F32),
        scratch_shapes=[pltpu.VMEM((tm, D_MODEL), F32)],
        compiler_params=_cparams(("arbitrary", "arbitrary")),
        name="back",
    )(x2d, h, gate, o_a, o_b, o_c, o_d, wts["w_merge"], wts["b_merge"], wts["w_branch"],
      wts["w_out"], final_g)


def _pad_cols_kernel(w_ref, o_ref):
    rows = w_ref.shape[0]
    o_ref[:, :SHIFT_W] = w_ref[:, :SHIFT_W].astype(BF16)
    o_ref[:, SHIFT_W:Z_SHIFT_PAD] = jnp.zeros((rows, Z_SHIFT_PAD - SHIFT_W), BF16)
    o_ref[:, Z_SHIFT_PAD:Z_SHIFT_PAD + 928] = w_ref[:, SHIFT_W:2656].astype(BF16)
    o_ref[:, Z_SHIFT_PAD + 928:3072] = jnp.zeros((rows, 3072 - Z_SHIFT_PAD - 928), BF16)
    o_ref[:, 3072:] = w_ref[:, 2656:].astype(BF16)


def _pad_cols(w, l, tm=256):
    _, k, n_in = w.shape
    return pl.pallas_call(
        _pad_cols_kernel,
        grid=(k // tm,),
        in_specs=[pl.BlockSpec((tm, n_in), lambda i: (l * (k // tm) + i, 0))],
        out_specs=pl.BlockSpec((tm, ZW), lambda i: (i, 0)),
        out_shape=jax.ShapeDtypeStruct((k, ZW), BF16),
        compiler_params=_cparams(("arbitrary",)),
        name="pad_w_in",
    )(w.reshape(-1, n_in))


def _rope_place():
    p = np.zeros((LANES, H_B * LANES), np.float32)
    for h in range(H_B):
        for j in range(ROPE):
            p[j, h * LANES + NOPE + j] = 1.0
    return jnp.asarray(p, BF16)


def _pack_q(w):
    w = w.reshape(Q_LORA, H_B, NOPE + ROPE)
    return jnp.pad(w, ((0, 0), (0, 0), (0, LANES - NOPE - ROPE))).reshape(Q_LORA, H_B * LANES)


def _pack_kv(w):
    w = w.reshape(KV_LORA, H_B, NOPE + VDIM)
    wk = jnp.pad(w[..., :NOPE], ((0, 0), (0, 0), (0, LANES - NOPE))).reshape(KV_LORA, H_B * LANES)
    return wk, w[..., NOPE:].reshape(KV_LORA, H_B * VDIM)


def _rope_tables(n_tokens):
    rows = n_tokens // GRID_W
    row = jnp.repeat(jnp.arange(rows, dtype=F32), GRID_W)
    col = jnp.tile(jnp.arange(GRID_W, dtype=F32), rows)
    n_freq = ROPE // 4
    inv = ROPE_BASE ** (-jnp.arange(n_freq, dtype=F32) / n_freq)
    ang = jnp.concatenate([row[:, None] * inv, col[:, None] * inv], axis=-1)
    cos = jnp.repeat(jnp.cos(ang), 2, axis=-1)
    sin = jnp.repeat(jnp.sin(ang), 2, axis=-1) * jnp.tile(jnp.asarray([-1.0, 1.0], F32), ROPE // 2)
    ones = lambda k: jnp.ones((n_tokens, k), F32)
    zeros = lambda k: jnp.zeros((n_tokens, k), F32)
    cq = jnp.concatenate([ones(NOPE), cos, ones(LANES - NOPE - ROPE)], axis=-1)
    sq = jnp.concatenate([zeros(NOPE), sin, zeros(LANES - NOPE - ROPE)], axis=-1)
    ck = jnp.concatenate([cos, ones(LANES - ROPE)], axis=-1)
    sk = jnp.concatenate([sin, zeros(LANES - ROPE)], axis=-1)
    return cq, sq, ck, sk


def _dft_tables(seq):
    k = np.arange(LANES)
    a = 2.0 * np.pi * ((k[:, None] * k[None, :]) % LANES) / LANES
    cs128 = np.concatenate([np.cos(a), np.sin(a)], axis=1)
    t = np.arange(seq)
    b = 2.0 * np.pi * ((t[:, None] * t[None, :]) % seq) / seq
    dft = np.concatenate([np.cos(b), -np.sin(b)], axis=1)
    return (jnp.asarray(cs128.astype(np.float32)).astype(BF16),
            jnp.asarray(dft.astype(np.float32)).astype(BF16))


def _lora_weights(w_up, a_up):
    z = jnp.zeros((LORA, BRANCH_W), F32)
    pad = jnp.zeros((256 - 3 * LORA, 2 * BRANCH_W), F32)
    wlt = jnp.concatenate([jnp.concatenate([w_up[0], z], 1), jnp.concatenate([z, w_up[1]], 1),
                           jnp.concatenate([z, z], 1), pad], axis=0)
    wla = jnp.concatenate([jnp.concatenate([z, z], 1), jnp.concatenate([z, z], 1),
                           jnp.concatenate([a_up[0], a_up[1]], 1), pad], axis=0)
    return wlt.astype(BF16), wla.astype(BF16)


def _to_blockdiag(s):
    b = s.shape[0]
    s = s.reshape(b, H_A // 2, 2, N_A, N_A)
    z = jnp.zeros_like(s[:, :, 0])
    top = jnp.concatenate([s[:, :, 0], z], axis=-1)
    bot = jnp.concatenate([z, s[:, :, 1]], axis=-1)
    return jnp.concatenate([top, bot], axis=-2)


def _trunk_layer(x2d, ss, lw, wts, l, nb, seq, rope_tabs, ctx, final_g):
    z, h = _front(x2d, ss[:, :2], lw["norm_g"], lw["w_in"], lw["mu"], seq)
    s0 = None
    if ctx is not None:
        s0 = jnp.stack([_to_blockdiag(ctx[0]), _to_blockdiag(ctx[1])], axis=1)
    o_a, sf, sb = _rwkv(z, lw["rwkv"], s0, nb, seq)
    q, k, v, ckv, kr = _mla_prep(z, lw["mla"], rope_tabs, seq)
    kc = vc = None
    if ctx is not None:
        past = ctx[2].shape[1]
        kr_c = jnp.pad(ctx[3].reshape(nb * past, ROPE), ((0, 0), (0, LANES - ROPE)))
        kc, vc = _kv_expand(ctx[2].reshape(nb * past, KV_LORA), kr_c, lw["mla"])
    o_b = _attention(z, q, k, v, kc, vc, nb, seq)
    o_c = _gmlp(z, lw["gmlp"])
    o_d = _fnet(z, *lw["dft"][seq], nb, seq)
    x_new = _back(x2d, h, ss[:, 2:], o_a, o_b, o_c, o_d, wts, l, final_g, l == DEPTH - 1)
    return x_new, sf, sb, ckv, kr


def kernel(x_prompt, x_sample, state_rwkv_fwd, state_rwkv_bwd, cache_mla_ckv, cache_mla_krope, c, c_ctx, norm_g, w_ada, b_ada, w_in, shift_mu, rwkv_w0, rwkv_w_up, rwkv_a0, rwkv_a_up, rwkv_k_k, rwkv_k_a, rwkv_r_k, rwkv_ln_g, rwkv_ln_b, mla_q_norm, mla_w_q_up, mla_kv_norm, mla_w_kv_up, gmlp_ln_g, gmlp_ln_b, gmlp_w_s, gmlp_b_s, w_branch, w_merge, b_merge, w_out, final_norm_g):
    nb_c, seq_c, _ = x_prompt.shape
    nb_l, seq_l, _ = x_sample.shape
    assert (nb_c * seq_c) % FRONT_TM == 0 and FRONT_TM % seq_c == 0 and seq_l == FRONT_TM

    cond8 = jnp.concatenate([c_ctx[None], c, jnp.zeros((8 - 1 - nb_l, D_MODEL), F32)], axis=0)
    mod = _modulation(cond8, w_ada, b_ada).reshape(DEPTH, 8, 3, D_MODEL)
    n_ctx_tiles = nb_c * seq_c // FRONT_TM

    pk = _rope_place()
    rope_tabs = _rope_tables(seq_l)
    dft = {s: _dft_tables(s) for s in {seq_c, seq_l}}
    mu_p = jnp.pad(shift_mu, ((0, 0), (0, Z_SHIFT_PAD - SHIFT_W)))[:, None]
    final_g = final_norm_g[None]
    wts = {"w_merge": w_merge.astype(BF16), "b_merge": b_merge[:, None], "w_branch": w_branch.astype(BF16),
           "w_out": w_out.astype(BF16)}

    xc = x_prompt.reshape(nb_c * seq_c, D_MODEL)
    xl = x_sample.reshape(nb_l * seq_l, D_MODEL)
    sf_list, sb_list, ckv_list, kr_list = [], [], [], []
    for l in range(DEPTH):
        wlt, wla = _lora_weights(rwkv_w_up[l], rwkv_a_up[l])
        wk, wv = _pack_kv(mla_w_kv_up[l])
        lw = {
            "norm_g": norm_g[l][None], "w_in": _pad_cols(w_in, l), "mu": mu_p[l],
            "rwkv": {
                "wlt": wlt, "wla": wla, "w0": rwkv_w0[l].reshape(1, 2 * BRANCH_W),
                "a0": rwkv_a0[l].reshape(1, 2 * BRANCH_W), "k_k": rwkv_k_k[l][None],
                "k_a": rwkv_k_a[l][None], "r_k": rwkv_r_k[l].reshape(1, BRANCH_W),
                "ln_g": rwkv_ln_g[l][None], "ln_b": rwkv_ln_b[l][None],
            },
            "mla": {
                "q_norm": mla_q_norm[l][None], "kv_norm": mla_kv_norm[l][None],
                "wq": _pack_q(mla_w_q_up[l]).astype(BF16), "wk": wk.astype(BF16),
                "wvt": wv.T.astype(BF16), "pk": pk,
            },
            "gmlp": {
                "ln_g": gmlp_ln_g[l][None], "ln_b": gmlp_ln_b[l][None],
                "w_s": gmlp_w_s[l].astype(BF16),
                "b_s": jnp.repeat(gmlp_b_s[l].T, BRANCH_W // G_C, axis=1),
            },
            "dft": dft,
        }
        ss_c = jnp.broadcast_to(mod[l, 0][None], (n_ctx_tiles, 3, D_MODEL))
        ss_l = mod[l, 1:1 + nb_l]
        xc, sf, sb, ckv, kr = _trunk_layer(xc, ss_c, lw, wts, l, nb_c, seq_c, None, None, final_g)
        sf_list.append(sf)
        sb_list.append(sb)
        ckv_list.append(ckv.reshape(nb_c, seq_c, KV_LORA))
        kr_list.append(kr.reshape(nb_c, seq_c, ROPE))
        ctx = (state_rwkv_fwd[:, l], state_rwkv_bwd[:, l], cache_mla_ckv[:, l], cache_mla_krope[:, l])
        xl = _trunk_layer(xl, ss_l, lw, wts, l, nb_l, seq_l, rope_tabs, ctx, final_g)[0]

    return (xc.reshape(nb_c, seq_c, D_MODEL), xl.reshape(nb_l, seq_l, D_MODEL),
            jnp.stack(sf_list, axis=1), jnp.stack(sb_list, axis=1),
            jnp.stack(ckv_list, axis=1), jnp.stack(kr_list, axis=1))
```

```python
import functools
import math

import jax
import jax.numpy as jnp
import numpy as np
from jax import lax
from jax.experimental import pallas as pl
from jax.experimental.pallas import tpu as pltpu

F32 = jnp.float32
BF16 = jnp.bfloat16

D_MODEL = 1024
DEPTH = 2
GRID_W = 64
BRANCH_W = 512
H_A = 8
N_A = 64
LORA = 64
RWKV_GN_EPS = 64e-5
H_B = 8
NOPE = 64
ROPE = 32
VDIM = 64
Q_LORA = 256
KV_LORA = 128
ROPE_BASE = 10000.0
Q_PRESCALE = (NOPE + ROPE) ** -0.5 * math.log2(math.e)
G_C = 4
CHUNK = 128
G_D = 4
NORM_EPS = 1e-6
SHIFT_W = 1728

LANES = 128
VMEM_LIMIT = 52 * 1024 * 1024

ZW = 6144
Z_SHIFT_PAD = 2048
COL_R, COL_K, COL_V = 0, 1, 2
COL_LORA = 6
COL_GA, COL_ZB, COL_GB, COL_U, COL_VC, COL_GC, COL_F, COL_GD = 4, 5, 6, 7, 8, 9, 10, 11
FRONT_TM = 2048
FRONT_TN = 1024
FRONT_COLS = 256
N_SHIFT_TILES = Z_SHIFT_PAD // FRONT_TN
SCAN_C = 64
SCAN_NB = 2
ATTN_GROUP = 4
ATTN_GROUP_LONG = 2
ATTN_TQ = 512
FNET_ROWS = 1024
ATTN_ROWS = 1024
BACK_ROWS = 256


def _mm(a, b):
    return jnp.dot(a.astype(BF16), b.astype(BF16), preferred_element_type=F32)


def _mm_nt(a, b):
    return lax.dot_general(a.astype(BF16), b.astype(BF16), (((1,), (1,)), ((), ())),
                           preferred_element_type=F32)


def _mm_split_lhs01(m01, x):
    hi = x.astype(BF16)
    lo = (x - hi.astype(F32)).astype(BF16)
    d = functools.partial(jnp.dot, preferred_element_type=F32)
    return d(m01, hi) + d(m01, lo)


def _silu(x):
    return x * jax.nn.sigmoid(x)


def _rms(x, g):
    return x * lax.rsqrt(jnp.mean(x * x, axis=-1, keepdims=True) + NORM_EPS) * g


def _cparams(sem, vmem=VMEM_LIMIT):
    return pltpu.CompilerParams(dimension_semantics=sem, vmem_limit_bytes=vmem)


def _mod_kernel(c_ref, w_ref, b_ref, o_ref):
    o_ref[0] = _mm(_silu(c_ref[...]), w_ref[0]) + b_ref[0]


def _modulation(cond8, w_ada, b_ada):
    return pl.pallas_call(
        _mod_kernel,
        grid=(DEPTH, 3),
        in_specs=[
            pl.BlockSpec((8, D_MODEL), lambda l, j: (0, 0)),
            pl.BlockSpec((1, D_MODEL, D_MODEL), lambda l, j: (l, 0, j)),
            pl.BlockSpec((1, 1, D_MODEL), lambda l, j: (l, 0, j)),
        ],
        out_specs=pl.BlockSpec((1, 8, D_MODEL), lambda l, j: (l, 0, j)),
        out_shape=jax.ShapeDtypeStruct((DEPTH, 8, 3 * D_MODEL), F32),
        compiler_params=_cparams(("arbitrary", "arbitrary")),
        name="modulation",
    )(cond8, w_ada, b_ada.reshape(DEPTH, 1, 3 * D_MODEL))


def _front_kernel(x_ref, ss_ref, g_ref, w_ref, mu_ref, z_ref, h_ref, *, seq):
    j = pl.program_id(1)

    @pl.when(j == 0)
    def _():
        h = _rms(x_ref[...], g_ref[...]) * (1.0 + ss_ref[0, 1:2, :]) + ss_ref[0, 0:1, :]
        h_ref[...] = h.astype(BF16)

    cols = [slice(c, c + FRONT_COLS) for c in range(0, FRONT_TN, FRONT_COLS)]

    @pl.when(j < N_SHIFT_TILES)
    def _():
        first = lax.broadcasted_iota(jnp.int32, (8, FRONT_COLS), 0) == 0
        last = lax.broadcasted_iota(jnp.int32, (8, FRONT_COLS), 0) == 7
        for cs in cols:
            z = jnp.dot(h_ref[...], w_ref[:, cs], preferred_element_type=F32)
            hmu = 0.5 * mu_ref[:, cs]
            omu = 1.0 - mu_ref[:, cs]
            prev = pltpu.roll(z, 1, axis=0)
            nxt = pltpu.roll(z, FRONT_TM - 1, axis=0)
            z_ref[:, cs] = z * omu + (prev + nxt) * hmu
            for s0 in range(0, FRONT_TM, seq):
                a = slice(s0, s0 + 8)
                z_ref[a, cs] = z[a] * omu + (jnp.where(first, 0.0, prev[a]) + nxt[a]) * hmu
                b = slice(s0 + seq - 8, s0 + seq)
                z_ref[b, cs] = z[b] * omu + (prev[b] + jnp.where(last, 0.0, nxt[b])) * hmu

    @pl.when(j >= N_SHIFT_TILES)
    def _():
        for cs in cols:
            z_ref[:, cs] = jnp.dot(h_ref[...], w_ref[:, cs], preferred_element_type=F32)


def _front(x2d, ss, norm_g, w_in_p, mu_p, seq):
    n = x2d.shape[0]
    return pl.pallas_call(
        functools.partial(_front_kernel, seq=seq),
        grid=(n // FRONT_TM, ZW // FRONT_TN),
        in_specs=[
            pl.BlockSpec((FRONT_TM, D_MODEL), lambda i, j: (i, 0)),
            pl.BlockSpec((1, 2, D_MODEL), lambda i, j: (i, 0, 0)),
            pl.BlockSpec((1, D_MODEL), lambda i, j: (0, 0)),
            pl.BlockSpec((D_MODEL, FRONT_TN), lambda i, j: (0, j)),
            pl.BlockSpec((1, FRONT_TN), lambda i, j: (0, jnp.minimum(j, N_SHIFT_TILES - 1))),
        ],
        out_specs=[pl.BlockSpec((FRONT_TM, FRONT_TN), lambda i, j: (i, j)),
                   pl.BlockSpec((FRONT_TM, D_MODEL), lambda i, j: (i, 0))],
        out_shape=[jax.ShapeDtypeStruct((n, ZW), F32), jax.ShapeDtypeStruct((n, D_MODEL), BF16)],
        compiler_params=_cparams(("arbitrary", "arbitrary")),
        name="front",
    )(x2d, ss, norm_g, w_in_p, mu_p)


def _scan_chunks(dirs, s_ref):
    c = SCAN_C
    n2 = 2 * c
    ri = lax.broadcasted_iota(jnp.int32, (c, c), 0)
    ci = lax.broadcasted_iota(jnp.int32, (c, c), 1)
    row = lax.broadcasted_iota(jnp.int32, (n2, n2), 0)
    col = lax.broadcasted_iota(jnp.int32, (n2, n2), 1)
    same64 = (row >> 6) == (col >> 6)
    same16 = (row >> 4) == (col >> 4)
    tl = row & (c - 1)
    il = col & (c - 1)
    eye = (row == col).astype(F32)
    lane_a = lax.broadcasted_iota(jnp.int32, (c, LANES), 1) < N_A

    def stack2(x):
        return jnp.concatenate([jnp.where(lane_a, x, 0.0), jnp.where(lane_a, 0.0, x)], axis=0)

    lhs, rhs, v2, bk, gtot, strict, incl, key = [], [], [], [], [], [], [], []
    for r, kk, v, lw, kt, bv, rev, j, d in dirs:
        tri = ((ri <= ci) if rev else (ri >= ci)).astype(BF16)
        cs = _mm_split_lhs01(tri, lw)
        tot = cs[0:1] if rev else cs[c - 1:c]
        g_tot = jnp.exp(tot)
        a_t = -kk * jnp.exp(cs - lw)
        r_t = r * jnp.exp(cs)
        g_inv = jnp.exp(-cs)
        b_t = bv * g_inv
        k_t = kt * g_inv
        g_rem = jnp.exp(tot - cs)
        b_h = bv * g_rem
        k_h = kt * g_rem
        st = same64 & ((il > tl) if rev else (il < tl))
        inc = same64 & ((il >= tl) if rev else (il <= tl))
        for p in range(H_A // 2):
            sl = slice(p * LANES, (p + 1) * LANES)
            lhs.append(jnp.concatenate([stack2(a_t[:, sl]), stack2(r_t[:, sl])], axis=0))
            rhs.append(jnp.concatenate([b_t[:, sl], k_t[:, sl]], axis=0))
            v2.append(stack2(v[:, sl]))
            bk.append(jnp.concatenate([stack2(b_h[:, sl]), stack2(k_h[:, sl])], axis=0))
            gtot.append(g_tot[:, sl])
            strict.append(st)
            incl.append(inc)
            key.append((j, d, p))
    ch = range(len(key))

    s = [s_ref[key[i]] for i in ch]
    big = [_mm_nt(lhs[i], jnp.concatenate([rhs[i], s[i]], axis=0)) for i in ch]
    lab, lak, mrbk = [], [], []
    for i in ch:
        x = big[i][:n2, :n2]
        xr = pltpu.roll(x, c, axis=1)
        y = big[i][n2:, :n2]
        yr = pltpu.roll(y, c, axis=1)
        pick = lambda top, bot: jnp.concatenate([top[:c], bot[c:]], axis=0)
        lab.append(jnp.where(strict[i], pick(x, xr), 0.0))
        lak.append(jnp.where(strict[i], pick(xr, x), 0.0))
        mrbk.append(jnp.concatenate([jnp.where(incl[i], pick(y, yr), 0.0),
                                     jnp.where(incl[i], pick(yr, y), 0.0)], axis=1))
    dg = [jnp.where(same16, lab[i], 0.0) for i in ch]
    off = [lab[i] - dg[i] for i in ch]
    pinv = [eye + dg[i] for i in ch]
    pw = [_mm(dg[i], dg[i]) for i in ch]
    for _ in range(2):
        t = [_mm(pw[i], jnp.concatenate([pw[i], pinv[i]], axis=1)) for i in ch]
        pinv = [pinv[i] + t[i][:, n2:] for i in ch]
        pw = [t[i][:, :n2] for i in ch]
    pinv = [pinv[i] + _mm(pw[i], pinv[i]) for i in ch]
    f = [_mm(pinv[i], off[i]) for i in ch]
    t = [_mm(f[i], jnp.concatenate([f[i], pinv[i]], axis=1)) for i in ch]
    g = [pinv[i] + t[i][:, n2:] for i in ch]
    tinv = [g[i] + _mm(t[i][:, :n2], g[i]) for i in ch]
    lv = [_mm(lak[i], v2[i]) for i in ch]
    u2 = [_mm(tinv[i], big[i][:n2, n2:] + lv[i]) for i in ch]
    uv = [jnp.concatenate([u2[i], v2[i]], axis=0) for i in ch]
    y2 = [big[i][n2:, n2:] + _mm(mrbk[i], uv[i]) for i in ch]
    for i in ch:
        s_ref[key[i]] = s[i] * gtot[i] + _mm(uv[i].T, bk[i])
    npair = H_A // 2
    return [jnp.concatenate([y2[g * npair + p][:c] + y2[g * npair + p][c:] for p in range(npair)], axis=1)
            for g in range(len(dirs))]


def _head_sums(x):
    lane_a = lax.broadcasted_iota(jnp.int32, (x.shape[0], LANES), 1) < N_A
    outs = []
    for p in range(H_A // 2):
        xs = x[:, p * LANES:(p + 1) * LANES]
        sa = jnp.sum(jnp.where(lane_a, xs, 0.0), axis=-1, keepdims=True)
        sb = jnp.sum(jnp.where(lane_a, 0.0, xs), axis=-1, keepdims=True)
        outs.append(jnp.where(lane_a, sa, sb))
    return jnp.concatenate(outs, axis=1)


def _rwkv_prepare(k, lora, d, w):
    wlt_ref, wla_ref, w0_ref, a0_ref, kk_ref, ka_ref, rk_ref = w
    hs = slice(d * BRANCH_W, (d + 1) * BRANCH_W)
    lo_w, lo_a = lora[:, :LANES], lora[:, LANES:]
    pre = w0_ref[:, hs] + _mm(jnp.tanh(lo_w), wlt_ref[:LANES, hs])
    lw = -math.exp(-0.5) * jax.nn.sigmoid(pre)
    kkraw = k * kk_ref[...]
    kk = kkraw * lax.rsqrt(jnp.maximum(_head_sums(kkraw * kkraw), 1e-24))
    k_a = ka_ref[...]
    if d:
        a_b = jax.nn.sigmoid(a0_ref[:, hs] + _mm(lo_a, wla_ref[LANES:, hs]))
        return kk, lw, k * (1.0 + (a_b - 1.0) * k_a), kk * a_b
    a2 = jax.nn.sigmoid(a0_ref[...] + _mm(lo_a, wla_ref[LANES:, :]))
    ktf = k * (1.0 + (a2[:, :BRANCH_W] - 1.0) * k_a)
    ktb = k * (1.0 + (a2[:, BRANCH_W:] - 1.0) * k_a)
    return kk, lw, ktf, kk * a2[:, hs], (ktf + ktb) * rk_ref[...]


def _rwkv_kernel(*refs, has_init):
    if has_init:
        s0_ref, refs = refs[0], refs[1:]
    cur = (refs[0:5], refs[5:10])
    nxt = (refs[10:12], refs[12:14])
    w = refs[14:21]
    lng_ref, lnb_ref = refs[21:23]
    o_ref, sf_ref, sb_ref, s_scr, ysum_scr, kts_scr, p_scr = refs[23:30]
    c = pl.program_id(1)
    nc = pl.num_programs(1)

    @pl.when(c == 0)
    def _():
        if has_init:
            s_scr[...] = s0_ref[...]
        else:
            s_scr[...] = jnp.zeros_like(s_scr)

    @pl.when((c == 0) & (pl.program_id(0) == 0))
    def _():
        for j in range(SCAN_NB):
            for d in range(2):
                for i, a in enumerate(_rwkv_prepare(cur[d][3][j], cur[d][4][j], d, w)):
                    p_scr[j, d, i] = a

    dirs = []
    for j in range(SCAN_NB):
        for d in range(2):
            kk, lw, kt, bv = [p_scr[j, d, i] for i in range(4)]
            dirs.append((cur[d][0][j], kk, cur[d][1][j], lw, kt, bv, d == 1, j, d))
    kts = [p_scr[j, 0, 4] for j in range(SCAN_NB)]
    ys = _scan_chunks(dirs, s_scr)

    for j in range(SCAN_NB):
        for d in range(2):
            for i, a in enumerate(_rwkv_prepare(nxt[d][0][j], nxt[d][1][j], d, w)):
                p_scr[j, d, i] = a

    rows = (pl.multiple_of(c * SCAN_C, SCAN_C), pl.multiple_of((nc - 1 - c) * SCAN_C, SCAN_C))

    @pl.when(c < nc // 2)
    def _():
        for j in range(SCAN_NB):
            kts_scr[j, pl.ds(rows[0], SCAN_C), :] = kts[j]
            for d in range(2):
                ysum_scr[j, pl.ds(rows[d], SCAN_C), :] = ys[2 * j + d]

    @pl.when(c >= nc // 2)
    def _():
        for j in range(SCAN_NB):
            for d in range(2):
                r, v, g = cur[d][0][j], cur[d][1][j], cur[d][2][j]
                bonus = _head_sums(r * (kts_scr[j, pl.ds(rows[1], SCAN_C), :] if d else kts[j])) * v
                o = ysum_scr[j, pl.ds(rows[d], SCAN_C), :] + ys[2 * j + d]
                dlt = o - _head_sums(o) * (1.0 / N_A)
                var = _head_sums(dlt * dlt) * (1.0 / N_A)
                y = dlt * lax.rsqrt(var + RWKV_GN_EPS) * lng_ref[...] + lnb_ref[...] + bonus
                o_ref[j, pl.ds(rows[d], SCAN_C), :] = (y * _silu(g)).astype(BF16)

    @pl.when(c == nc - 1)
    def _():
        for j in range(SCAN_NB):
            for d, st_ref in enumerate((sf_ref, sb_ref)):
                for p in range(H_A // 2):
                    s = s_scr[j, d, p]
                    st_ref[j, 2 * p] = s[:N_A, :N_A]
                    st_ref[j, 2 * p + 1] = pltpu.roll(s, N_A, axis=1)[N_A:, :N_A]


def _rwkv(z, wp, s0_bd, nb, seq):
    n = z.shape[0]
    nc = seq // SCAN_C
    assert nc % 2 == 0 and nb % SCAN_NB == 0
    z3 = z.reshape(nb, seq, ZW)
    chunk = ((lambda c: c), (lambda c: nc - 1 - c))
    spec = lambda idx, col, w=BRANCH_W: pl.BlockSpec((SCAN_NB, SCAN_C, w), lambda b, c: (b, idx(c), col))
    in_specs = []
    for d in range(2):
        in_specs += [spec(chunk[d], COL_R), spec(chunk[d], COL_V), spec(chunk[d], COL_GA),
                     spec(chunk[d], COL_K), spec(chunk[d], COL_LORA, 256)]

    def spec_next(d, col, w=BRANCH_W):
        def index(b, c):
            wrap = c == nc - 1
            nb_i = jnp.minimum(b + wrap.astype(jnp.int32), nb // SCAN_NB - 1)
            return (nb_i, chunk[d](jnp.where(wrap, 0, c + 1)), col)
        return pl.BlockSpec((SCAN_NB, SCAN_C, w), index)

    for d in range(2):
        in_specs += [spec_next(d, COL_K), spec_next(d, COL_LORA, 256)]
    full = lambda a: pl.BlockSpec(a.shape, lambda b, c: (0,) * a.ndim)
    params = (wp["wlt"], wp["wla"], wp["w0"], wp["a0"], wp["k_k"], wp["k_a"], wp["r_k"], wp["ln_g"], wp["ln_b"])
    in_specs += [full(p) for p in params]
    args = [z3] * 14 + list(params)
    sspec = pl.BlockSpec((SCAN_NB, 2, H_A // 2, LANES, LANES), lambda b, c: (b, 0, 0, 0, 0))
    has_init = s0_bd is not None
    if has_init:
        in_specs = [sspec] + in_specs
        args = [s0_bd] + args
    fspec = pl.BlockSpec((SCAN_NB, H_A, N_A, N_A), lambda b, c: (b, 0, 0, 0))
    o_a, sf, sb = pl.pallas_call(
        functools.partial(_rwkv_kernel, has_init=has_init),
        grid=(nb // SCAN_NB, nc),
        in_specs=in_specs,
        out_specs=[pl.BlockSpec((SCAN_NB, seq, BRANCH_W), lambda b, c: (b, 0, 0)), fspec, fspec],
        out_shape=[jax.ShapeDtypeStruct((nb, seq, BRANCH_W), BF16),
                   jax.ShapeDtypeStruct((nb, H_A, N_A, N_A), F32),
                   jax.ShapeDtypeStruct((nb, H_A, N_A, N_A), F32)],
        scratch_shapes=[pltpu.VMEM((SCAN_NB, 2, H_A // 2, LANES, LANES), F32),
                        pltpu.VMEM((SCAN_NB, seq, BRANCH_W), F32),
                        pltpu.VMEM((SCAN_NB, seq, BRANCH_W), F32),
                        pltpu.VMEM((SCAN_NB, 2, 5, SCAN_C, BRANCH_W), F32)],
        compiler_params=_cparams(("arbitrary", "arbitrary")),
        name="rwkv",
    )(*args)
    return o_a.reshape(n, BRANCH_W), sf, sb


def _swap_pairs(x):
    w = x.shape[1]
    even = (lax.broadcasted_iota(jnp.int32, x.shape, 1) & 1) == 0
    return jnp.where(even, pltpu.roll(x, w - 1, axis=1), pltpu.roll(x, 1, axis=1))


def _place_rope_key(kr):
    return jnp.concatenate([pltpu.roll(kr, NOPE, axis=1)] * H_B, axis=1)


def _mla_prep_kernel(*refs, rope):
    zb_ref, gq_ref, gkv_ref, wq_ref, wk_ref, wvt_ref = refs[:6]
    if rope:
        cq_ref, sq_ref, ck_ref, sk_ref = refs[6:10]
        refs = refs[10:]
    else:
        refs = refs[6:]
    q_o, k_o, vt_o, ckv_o, kr_o = refs
    zb = zb_ref[...]
    q = _mm(_rms(zb[:, :Q_LORA], gq_ref[...]), wq_ref[...])
    ckv = _rms(zb[:, Q_LORA:Q_LORA + KV_LORA], gkv_ref[...])
    kr = zb[:, Q_LORA + KV_LORA:]
    if rope:
        cq = jnp.concatenate([cq_ref[...]] * H_B, axis=1)
        sq = jnp.concatenate([sq_ref[...]] * H_B, axis=1)
        q = q * cq + _swap_pairs(q) * sq
        kr = kr * ck_ref[...] + _swap_pairs(kr) * sk_ref[...]
    q_o[...] = (q * Q_PRESCALE).astype(BF16)
    k_o[...] = (_mm(ckv, wk_ref[...]) + _place_rope_key(kr)).astype(BF16)
    vt_o[...] = _mm_nt(wvt_ref[...], ckv).astype(BF16)
    ckv_o[...] = ckv
    kr_o[...] = kr[:, :ROPE]


def _mla_prep(z, mp, rope_tabs, seq, tm=1024):
    n = z.shape[0]
    full = lambda a: pl.BlockSpec(a.shape, lambda i: (0,) * a.ndim)
    params = (mp["q_norm"], mp["kv_norm"], mp["wq"], mp["wk"], mp["wvt"])
    in_specs = [pl.BlockSpec((tm, BRANCH_W), lambda i: (i, COL_ZB))] + [full(p) for p in params]
    args = [z, *params]
    rope = rope_tabs is not None
    if rope:
        per = seq // tm
        in_specs += [pl.BlockSpec((tm, LANES), lambda i: (i % per, 0))] * 4
        args += list(rope_tabs)
    rb = lambda w: pl.BlockSpec((tm, w), lambda i: (i, 0))
    return pl.pallas_call(
        functools.partial(_mla_prep_kernel, rope=rope),
        grid=(n // tm,),
        in_specs=in_specs,
        out_specs=[rb(H_B * LANES), rb(H_B * LANES), pl.BlockSpec((H_B * VDIM, tm), lambda i: (0, i)),
                   rb(KV_LORA), rb(ROPE)],
        out_shape=[jax.ShapeDtypeStruct((n, H_B * LANES), BF16),
                   jax.ShapeDtypeStruct((n, H_B * LANES), BF16),
                   jax.ShapeDtypeStruct((H_B * VDIM, n), BF16),
                   jax.ShapeDtypeStruct((n, KV_LORA), F32),
                   jax.ShapeDtypeStruct((n, ROPE), F32)],
        compiler_params=_cparams(("arbitrary",)),
        name="mla_prep",
    )(*args)


def _kv_expand_kernel(ckv_ref, kr_ref, wk_ref, wvt_ref, k_o, vt_o):
    ckv = ckv_ref[...]
    k_o[...] = (_mm(ckv, wk_ref[...]) + _place_rope_key(kr_ref[...])).astype(BF16)
    vt_o[...] = _mm_nt(wvt_ref[...], ckv).astype(BF16)


def _kv_expand(ckv, kr128, mp):
    m = ckv.shape[0]
    full = lambda a: pl.BlockSpec(a.shape, lambda i: (0,) * a.ndim)
    args = (ckv, kr128, mp["wk"], mp["wvt"])
    return pl.pallas_call(
        _kv_expand_kernel,
        grid=(1,),
        in_specs=[full(a) for a in args],
        out_specs=[pl.BlockSpec((m, H_B * LANES), lambda i: (0, 0)),
                   pl.BlockSpec((H_B * VDIM, m), lambda i: (0, 0))],
        out_shape=[jax.ShapeDtypeStruct((m, H_B * LANES), BF16),
                   jax.ShapeDtypeStruct((H_B * VDIM, m), BF16)],
        compiler_params=_cparams(("arbitrary",)),
        name="mla_ctx_kv",
    )(*args)


def _attn_kernel(*refs, has_ctx, group):
    if has_ctx:
        q_ref, k_ref, vt_ref, kc_ref, vct_ref, g_ref, o_ref = refs
    else:
        q_ref, k_ref, vt_ref, g_ref, o_ref = refs
    nbs, seq = k_ref.shape[0], k_ref.shape[1]
    nt = (((1,), (1,)), ((), ()))
    groups = [(b, h0) for b in range(nbs) for h0 in range(0, H_B, group)]

    def scores(b, h0):
        out = []
        for h in range(h0, h0 + group):
            hs = slice(h * LANES, (h + 1) * LANES)
            s = lax.dot_general(k_ref[b, :, hs], q_ref[b, :, hs], nt, preferred_element_type=F32)
            sc = None
            if has_ctx:
                sc = lax.dot_general(kc_ref[:, hs], q_ref[b, :, hs], nt, preferred_element_type=F32)
            out.append((s, sc))
        return out

    pending = scores(*groups[0])
    outs = []
    for gi, (b, h0) in enumerate(groups):
        cur = pending
        if gi + 1 < len(groups):
            pending = scores(*groups[gi + 1])
        keys = slice(b * seq, (b + 1) * seq)
        m, e, ec, den = [], [], [], []
        for s, sc in cur:
            mi = jnp.max(s, axis=0, keepdims=True)
            if has_ctx:
                mi = jnp.maximum(mi, jnp.max(sc, axis=0, keepdims=True))
            m.append(mi)
        for (s, sc), mi in zip(cur, m):
            ei = jnp.exp2(s - mi)
            di = jnp.sum(ei, axis=0, keepdims=True)
            e.append(ei.astype(BF16))
            if has_ctx:
                eci = jnp.exp2(sc - mi)
                di = di + jnp.sum(eci, axis=0, keepdims=True)
                ec.append(eci.astype(BF16))
            den.append(di)
        for i in range(group):
            vh = slice((h0 + i) * VDIM, (h0 + i + 1) * VDIM)
            o = jnp.dot(vt_ref[vh, keys], e[i], preferred_element_type=F32)
            if has_ctx:
                o = o + jnp.dot(vct_ref[vh, :], ec[i], preferred_element_type=F32)
            outs.append(o / den[i])
        if h0 + group == H_B:
            o_ref[b] = (jnp.concatenate(outs, axis=0).T * _silu(g_ref[b])).astype(BF16)
            outs = []


def _attention(z, q, k, vt, kc, vct, nb, seq):
    n = z.shape[0]
    tq = min(seq, ATTN_TQ)
    nq = seq // tq
    has_ctx = kc is not None
    nbs = 1 if has_ctx else max(1, min(nb, ATTN_ROWS // seq))
    z3 = z.reshape(nb, seq, ZW)
    in_specs = [pl.BlockSpec((nbs, tq, H_B * LANES), lambda b, i: (b, i, 0)),
                pl.BlockSpec((nbs, seq, H_B * LANES), lambda b, i: (b, 0, 0)),
                pl.BlockSpec((H_B * VDIM, nbs * seq), lambda b, i: (0, b))]
    args = [q.reshape(nb, seq, H_B * LANES), k.reshape(nb, seq, H_B * LANES), vt]
    if has_ctx:
        past = kc.shape[0] // nb
        in_specs += [pl.BlockSpec((past, H_B * LANES), lambda b, i: (b, 0)),
                     pl.BlockSpec((H_B * VDIM, past), lambda b, i: (0, b))]
        args += [kc, vct]
    in_specs.append(pl.BlockSpec((nbs, tq, BRANCH_W), lambda b, i: (b, i, COL_GB)))
    args.append(z3)
    out = pl.pallas_call(
        functools.partial(_attn_kernel, has_ctx=has_ctx, group=ATTN_GROUP_LONG if has_ctx else ATTN_GROUP),
        grid=(nb // nbs, nq),
        in_specs=in_specs,
        out_specs=pl.BlockSpec((nbs, tq, BRANCH_W), lambda b, i: (b, i, 0)),
        out_shape=jax.ShapeDtypeStruct((nb, seq, BRANCH_W), BF16),
        compiler_params=_cparams(("arbitrary", "arbitrary")),
        name="mla_attention",
    )(*args)
    return out.reshape(n, BRANCH_W)


def _gmlp_kernel(u_ref, vc_ref, g_ref, lng_ref, lnb_ref, ws_ref, bs_ref, o_ref):
    tm = u_ref.shape[0]
    u = jax.nn.gelu(u_ref[...])
    x = jax.nn.gelu(vc_ref[...])
    mu = jnp.mean(x, axis=-1, keepdims=True)
    xc = x - mu
    var = jnp.mean(xc * xc, axis=-1, keepdims=True)
    vn = (xc * lax.rsqrt(var + 1e-5) * lng_ref[...] + lnb_ref[...]).astype(BF16)
    rows = []
    for c in range(tm // CHUNK):
        cols = []
        for g in range(G_C):
            blk = vn[c * CHUNK:(c + 1) * CHUNK, g * LANES:(g + 1) * LANES]
            cols.append(jnp.dot(ws_ref[g], blk, preferred_element_type=F32))
        rows.append(jnp.concatenate(cols, axis=1) + bs_ref[...])
    mixed = jnp.concatenate(rows, axis=0)
    o_ref[...] = (u * mixed * _silu(g_ref[...])).astype(BF16)


def _gmlp(z, gp, tm=1024):
    n = z.shape[0]
    zb = lambda c: pl.BlockSpec((tm, BRANCH_W), lambda i, c=c: (i, c))
    full = lambda a: pl.BlockSpec(a.shape, lambda i: (0,) * a.ndim)
    params = (gp["ln_g"], gp["ln_b"], gp["w_s"], gp["b_s"])
    return pl.pallas_call(
        _gmlp_kernel,
        grid=(n // tm,),
        in_specs=[zb(COL_U), zb(COL_VC), zb(COL_GC)] + [full(p) for p in params],
        out_specs=pl.BlockSpec((tm, BRANCH_W), lambda i: (i, 0)),
        out_shape=jax.ShapeDtypeStruct((n, BRANCH_W), BF16),
        compiler_params=_cparams(("arbitrary",)),
        name="gmlp",
    )(z, z, z, *params)


def _fnet_kernel(f_ref, g_ref, cs_ref, dft_ref, o_ref, xcs_scr, *, seq):
    r = pl.program_id(1)
    nbs = f_ref.shape[0]

    @pl.when(r == 0)
    def _():
        for b in range(nbs):
            for g in range(G_D):
                xg = _mm(f_ref[b, :, g * LANES:(g + 1) * LANES], cs_ref[...])
                xcs_scr[b, 0:seq, g * LANES:(g + 1) * LANES] = xg[:, :LANES].astype(BF16)
                xcs_scr[b, seq:2 * seq, g * LANES:(g + 1) * LANES] = xg[:, LANES:].astype(BF16)

    for b in range(nbs):
        y = jnp.dot(dft_ref[...], xcs_scr[b], preferred_element_type=F32)
        o_ref[b] = (y * (1.0 / math.sqrt(seq * LANES)) * _silu(g_ref[b])).astype(BF16)


def _fnet(z, cs128, dft, nb, seq):
    n = z.shape[0]
    tr = min(seq, FNET_ROWS)
    nr = seq // tr
    nbs = max(1, min(nb, FRONT_TM // seq))
    z3 = z.reshape(nb, seq, ZW)
    out = pl.pallas_call(
        functools.partial(_fnet_kernel, seq=seq),
        grid=(nb // nbs, nr),
        in_specs=[pl.BlockSpec((nbs, seq, BRANCH_W), lambda b, r: (b, 0, COL_F)),
                  pl.BlockSpec((nbs, tr, BRANCH_W), lambda b, r: (b, r, COL_GD)),
                  pl.BlockSpec((LANES, 2 * LANES), lambda b, r: (0, 0)),
                  pl.BlockSpec((tr, 2 * seq), lambda b, r: (r, 0))],
        out_specs=pl.BlockSpec((nbs, tr, BRANCH_W), lambda b, r: (b, r, 0)),
        out_shape=jax.ShapeDtypeStruct((nb, seq, BRANCH_W), BF16),
        scratch_shapes=[pltpu.VMEM((nbs, 2 * seq, BRANCH_W), BF16)],
        compiler_params=_cparams(("arbitrary", "arbitrary")),
        name="fnet",
    )(z3, z3, cs128, dft)
    return out.reshape(n, BRANCH_W)


def _back_kernel(x_ref, h_ref, gate_ref, oa_ref, ob_ref, oc_ref, od_ref, wm_ref, bm_ref,
                 wb_ref, wo_ref, fg_ref, y_ref, acc_scr, *, final):
    nb = pl.program_id(1)
    tm = x_ref.shape[0]
    chunks = [pl.ds(r, BACK_ROWS) for r in range(0, tm, BACK_ROWS)]

    for i, o_ref in enumerate((oa_ref, ob_ref, oc_ref, od_ref)):
        @pl.when(nb == i)
        def _(o_ref=o_ref, i=i):
            for rows in chunks:
                gates = jax.nn.sigmoid(jnp.dot(h_ref[rows, :], wm_ref[...], preferred_element_type=F32)
                                       + bm_ref[...])
                upd = gates * jnp.dot(o_ref[rows, :], wb_ref[0], preferred_element_type=F32)
                if i == 0:
                    acc_scr[rows, :] = upd
                else:
                    acc_scr[rows, :] += upd

    @pl.when(nb == 3)
    def _():
        for rows in chunks:
            xn = x_ref[rows, :] + gate_ref[0] * _mm(acc_scr[rows, :], wo_ref[...])
            if final:
                xn = _rms(xn, fg_ref[...])
            y_ref[rows, :] = xn


def _back(x2d, h, gate, o_a, o_b, o_c, o_d, wts, l, final_g, final, tm=1024):
    n = x2d.shape[0]
    rows_per_gate = FRONT_TM // tm
    ob = pl.BlockSpec((tm, BRANCH_W), lambda i, j: (i, 0))
    return pl.pallas_call(
        functools.partial(_back_kernel, final=final),
        grid=(n // tm, 4),
        in_specs=[pl.BlockSpec((tm, D_MODEL), lambda i, j: (i, 0)),
                  pl.BlockSpec((tm, D_MODEL), lambda i, j: (i, 0)),
                  pl.BlockSpec((1, 1, D_MODEL), lambda i, j: (i // rows_per_gate, 0, 0)),
                  ob, ob, ob, ob,
                  pl.BlockSpec((None, D_MODEL, D_MODEL), lambda i, j: (l, 0, j)),
                  pl.BlockSpec((None, 1, D_MODEL), lambda i, j: (l, 0, j)),
                  pl.BlockSpec((None, 1, BRANCH_W, D_MODEL), lambda i, j: (l, j, 0, 0)),
                  pl.BlockSpec((None, D_MODEL, D_MODEL), lambda i, j: (l, 0, 0)),
                  pl.BlockSpec((1, D_MODEL), lambda i, j: (0, 0))],
        out_specs=pl.BlockSpec((tm, D_MODEL), lambda i, j: (i, 0)),
        out_shape=jax.ShapeDtypeStruct((n, D_MODEL), F32),
        scratch_shapes=[pltpu.VMEM((tm, D_MODEL), F32)],
        compiler_params=_cparams(("arbitrary", "arbitrary")),
        name="back",
    )(x2d, h, gate, o_a, o_b, o_c, o_d, wts["w_merge"], wts["b_merge"], wts["w_branch"],
      wts["w_out"], final_g)


def _pad_cols_kernel(w_ref, o_ref):
    rows = w_ref.shape[0]
    o_ref[:, :SHIFT_W] = w_ref[:, :SHIFT_W].astype(BF16)
    o_ref[:, SHIFT_W:Z_SHIFT_PAD] = jnp.zeros((rows, Z_SHIFT_PAD - SHIFT_W), BF16)
    o_ref[:, Z_SHIFT_PAD:Z_SHIFT_PAD + 928] = w_ref[:, SHIFT_W:2656].astype(BF16)
    o_ref[:, Z_SHIFT_PAD + 928:3072] = jnp.zeros((rows, 3072 - Z_SHIFT_PAD - 928), BF16)
    o_ref[:, 3072:] = w_ref[:, 2656:].astype(BF16)


def _pad_cols(w, l, tm=256):
    _, k, n_in = w.shape
    return pl.pallas_call(
        _pad_cols_kernel,
        grid=(k // tm,),
        in_specs=[pl.BlockSpec((tm, n_in), lambda i: (l * (k // tm) + i, 0))],
        out_specs=pl.BlockSpec((tm, ZW), lambda i: (i, 0)),
        out_shape=jax.ShapeDtypeStruct((k, ZW), BF16),
        compiler_params=_cparams(("arbitrary",)),
        name="pad_w_in",
    )(w.reshape(-1, n_in))


def _pack_q(w):
    w = w.reshape(Q_LORA, H_B, NOPE + ROPE)
    return jnp.pad(w, ((0, 0), (0, 0), (0, LANES - NOPE - ROPE))).reshape(Q_LORA, H_B * LANES)


def _pack_kv(w):
    w = w.reshape(KV_LORA, H_B, NOPE + VDIM)
    wk = jnp.pad(w[..., :NOPE], ((0, 0), (0, 0), (0, LANES - NOPE))).reshape(KV_LORA, H_B * LANES)
    return wk, w[..., NOPE:].reshape(KV_LORA, H_B * VDIM)


def _rope_tables(n_tokens):
    rows = n_tokens // GRID_W
    row = jnp.repeat(jnp.arange(rows, dtype=F32), GRID_W)
    col = jnp.tile(jnp.arange(GRID_W, dtype=F32), rows)
    n_freq = ROPE // 4
    inv = ROPE_BASE ** (-jnp.arange(n_freq, dtype=F32) / n_freq)
    ang = jnp.concatenate([row[:, None] * inv, col[:, None] * inv], axis=-1)
    cos = jnp.repeat(jnp.cos(ang), 2, axis=-1)
    sin = jnp.repeat(jnp.sin(ang), 2, axis=-1) * jnp.tile(jnp.asarray([-1.0, 1.0], F32), ROPE // 2)
    ones = lambda k: jnp.ones((n_tokens, k), F32)
    zeros = lambda k: jnp.zeros((n_tokens, k), F32)
    cq = jnp.concatenate([ones(NOPE), cos, ones(LANES - NOPE - ROPE)], axis=-1)
    sq = jnp.concatenate([zeros(NOPE), sin, zeros(LANES - NOPE - ROPE)], axis=-1)
    ck = jnp.concatenate([cos, ones(LANES - ROPE)], axis=-1)
    sk = jnp.concatenate([sin, zeros(LANES - ROPE)], axis=-1)
    return cq, sq, ck, sk


def _dft_tables(seq):
    k = np.arange(LANES)
    a = 2.0 * np.pi * ((k[:, None] * k[None, :]) % LANES) / LANES
    cs128 = np.concatenate([np.cos(a), np.sin(a)], axis=1)
    t = np.arange(seq)
    b = 2.0 * np.pi * ((t[:, None] * t[None, :]) % seq) / seq
    dft = np.concatenate([np.cos(b), -np.sin(b)], axis=1)
    return (jnp.asarray(cs128.astype(np.float32)).astype(BF16),
            jnp.asarray(dft.astype(np.float32)).astype(BF16))


def _lora_weights(w_up, a_up):
    z = jnp.zeros((LORA, BRANCH_W), F32)
    pad = jnp.zeros((256 - 3 * LORA, 2 * BRANCH_W), F32)
    wlt = jnp.concatenate([jnp.concatenate([w_up[0], z], 1), jnp.concatenate([z, w_up[1]], 1),
                           jnp.concatenate([z, z], 1), pad], axis=0)
    wla = jnp.concatenate([jnp.concatenate([z, z], 1), jnp.concatenate([z, z], 1),
                           jnp.concatenate([a_up[0], a_up[1]], 1), pad], axis=0)
    return wlt.astype(BF16), wla.astype(BF16)


def _to_blockdiag(s):
    b = s.shape[0]
    s = s.reshape(b, H_A // 2, 2, N_A, N_A)
    z = jnp.zeros_like(s[:, :, 0])
    top = jnp.concatenate([s[:, :, 0], z], axis=-1)
    bot = jnp.concatenate([z, s[:, :, 1]], axis=-1)
    return jnp.concatenate([top, bot], axis=-2)


def _trunk_layer(x2d, ss, lw, wts, l, nb, seq, rope_tabs, ctx, final_g):
    z, h = _front(x2d, ss[:, :2], lw["norm_g"], lw["w_in"], lw["mu"], seq)
    s0 = None
    if ctx is not None:
        s0 = jnp.stack([_to_blockdiag(ctx[0]), _to_blockdiag(ctx[1])], axis=1)
    o_a, sf, sb = _rwkv(z, lw["rwkv"], s0, nb, seq)
    q, k, v, ckv, kr = _mla_prep(z, lw["mla"], rope_tabs, seq)
    kc = vc = None
    if ctx is not None:
        past = ctx[2].shape[1]
        kr_c = jnp.pad(ctx[3].reshape(nb * past, ROPE), ((0, 0), (0, LANES - ROPE)))
        kc, vc = _kv_expand(ctx[2].reshape(nb * past, KV_LORA), kr_c, lw["mla"])
    o_b = _attention(z, q, k, v, kc, vc, nb, seq)
    o_c = _gmlp(z, lw["gmlp"])
    o_d = _fnet(z, *lw["dft"][seq], nb, seq)
    x_new = _back(x2d, h, ss[:, 2:], o_a, o_b, o_c, o_d, wts, l, final_g, l == DEPTH - 1)
    return x_new, sf, sb, ckv, kr


def kernel(x_prompt, x_sample, state_rwkv_fwd, state_rwkv_bwd, cache_mla_ckv, cache_mla_krope, c, c_ctx, norm_g, w_ada, b_ada, w_in, shift_mu, rwkv_w0, rwkv_w_up, rwkv_a0, rwkv_a_up, rwkv_k_k, rwkv_k_a, rwkv_r_k, rwkv_ln_g, rwkv_ln_b, mla_q_norm, mla_w_q_up, mla_kv_norm, mla_w_kv_up, gmlp_ln_g, gmlp_ln_b, gmlp_w_s, gmlp_b_s, w_branch, w_merge, b_merge, w_out, final_norm_g):
    nb_c, seq_c, _ = x_prompt.shape
    nb_l, seq_l, _ = x_sample.shape
    assert (nb_c * seq_c) % FRONT_TM == 0 and FRONT_TM % seq_c == 0 and seq_l == FRONT_TM

    cond8 = jnp.concatenate([c_ctx[None], c, jnp.zeros((8 - 1 - nb_l, D_MODEL), F32)], axis=0)
    mod = _modulation(cond8, w_ada, b_ada).reshape(DEPTH, 8, 3, D_MODEL)
    n_ctx_tiles = nb_c * seq_c // FRONT_TM

    rope_tabs = _rope_tables(seq_l)
    dft = {s: _dft_tables(s) for s in {seq_c, seq_l}}
    mu_p = jnp.pad(shift_mu, ((0, 0), (0, Z_SHIFT_PAD - SHIFT_W)))[:, None]
    final_g = final_norm_g[None]
    wts = {"w_merge": w_merge.astype(BF16), "b_merge": b_merge[:, None], "w_branch": w_branch.astype(BF16),
           "w_out": w_out.astype(BF16)}

    xc = x_prompt.reshape(nb_c * seq_c, D_MODEL)
    xl = x_sample.reshape(nb_l * seq_l, D_MODEL)
    sf_list, sb_list, ckv_list, kr_list = [], [], [], []
    for l in range(DEPTH):
        wlt, wla = _lora_weights(rwkv_w_up[l], rwkv_a_up[l])
        wk, wv = _pack_kv(mla_w_kv_up[l])
        lw = {
            "norm_g": norm_g[l][None], "w_in": _pad_cols(w_in, l), "mu": mu_p[l],
            "rwkv": {
                "wlt": wlt, "wla": wla, "w0": rwkv_w0[l].reshape(1, 2 * BRANCH_W),
                "a0": rwkv_a0[l].reshape(1, 2 * BRANCH_W), "k_k": rwkv_k_k[l][None],
                "k_a": rwkv_k_a[l][None], "r_k": rwkv_r_k[l].reshape(1, BRANCH_W),
                "ln_g": rwkv_ln_g[l][None], "ln_b": rwkv_ln_b[l][None],
            },
            "mla": {
                "q_norm": mla_q_norm[l][None], "kv_norm": mla_kv_norm[l][None],
                "wq": _pack_q(mla_w_q_up[l]).astype(BF16), "wk": wk.astype(BF16),
                "wvt": wv.T.astype(BF16),
            },
            "gmlp": {
                "ln_g": gmlp_ln_g[l][None], "ln_b": gmlp_ln_b[l][None],
                "w_s": gmlp_w_s[l].astype(BF16),
                "b_s": jnp.repeat(gmlp_b_s[l].T, BRANCH_W // G_C, axis=1),
            },
            "dft": dft,
        }
        ss_c = jnp.broadcast_to(mod[l, 0][None], (n_ctx_tiles, 3, D_MODEL))
        ss_l = mod[l, 1:1 + nb_l]
        xc, sf, sb, ckv, kr = _trunk_layer(xc, ss_c, lw, wts, l, nb_c, seq_c, None, None, final_g)
        sf_list.append(sf)
        sb_list.append(sb)
        ckv_list.append(ckv.reshape(nb_c, seq_c, KV_LORA))
        kr_list.append(kr.reshape(nb_c, seq_c, ROPE))
        ctx = (state_rwkv_fwd[:, l], state_rwkv_bwd[:, l], cache_mla_ckv[:, l], cache_mla_krope[:, l])
        xl = _trunk_layer(xl, ss_l, lw, wts, l, nb_l, seq_l, rope_tabs, ctx, final_g)[0]

    return (xc.reshape(nb_c, seq_c, D_MODEL), xl.reshape(nb_l, seq_l, D_MODEL),
            jnp.stack(sf_list, axis=1), jnp.stack(sb_list, axis=1),
            jnp.stack(ckv_list, axis=1), jnp.stack(kr_list, axis=1))
```

```python
import functools
import math

import jax
import jax.numpy as jnp
import numpy as np
from jax import lax
from jax.experimental import pallas as pl
from jax.experimental.pallas import tpu as pltpu

F32 = jnp.float32
BF16 = jnp.bfloat16

D_MODEL = 1024
DEPTH = 2
GRID_W = 64
BRANCH_W = 512
H_A = 8
N_A = 64
LORA = 64
RWKV_GN_EPS = 64e-5
H_B = 8
NOPE = 64
ROPE = 32
VDIM = 64
Q_LORA = 256
KV_LORA = 128
ROPE_BASE = 10000.0
Q_PRESCALE = (NOPE + ROPE) ** -0.5 * math.log2(math.e)
G_C = 4
CHUNK = 128
G_D = 4
NORM_EPS = 1e-6
SHIFT_W = 1728

LANES = 128
VMEM_LIMIT = 52 * 1024 * 1024

ZW = 6144
Z_SHIFT_PAD = 2048
COL_R, COL_K, COL_V = 0, 1, 2
COL_LORA = 6
COL_GA, COL_ZB, COL_GB, COL_U, COL_VC, COL_GC, COL_F, COL_GD = 4, 5, 6, 7, 8, 9, 10, 11
FRONT_TM = 2048
FRONT_TN = 1024
FRONT_COLS = 256
N_SHIFT_TILES = Z_SHIFT_PAD // FRONT_TN
SCAN_C = 64
SCAN_NB = 2
ATTN_GROUP = 4
ATTN_GROUP_LONG = 2
ATTN_TQ = 512
FNET_ROWS = 1024
ATTN_ROWS = 1024
BACK_ROWS = 256


def _mm(a, b):
    return jnp.dot(a.astype(BF16), b.astype(BF16), preferred_element_type=F32)


def _mm_nt(a, b):
    return lax.dot_general(a.astype(BF16), b.astype(BF16), (((1,), (1,)), ((), ())),
                           preferred_element_type=F32)


def _mm_split_lhs01(m01, x):
    hi = x.astype(BF16)
    lo = (x - hi.astype(F32)).astype(BF16)
    d = functools.partial(jnp.dot, preferred_element_type=F32)
    return d(m01, hi) + d(m01, lo)


def _silu(x):
    return x * jax.nn.sigmoid(x)


def _rms(x, g):
    return x * lax.rsqrt(jnp.mean(x * x, axis=-1, keepdims=True) + NORM_EPS) * g


def _cparams(sem, vmem=VMEM_LIMIT):
    return pltpu.CompilerParams(dimension_semantics=sem, vmem_limit_bytes=vmem)


def _mod_kernel(c_ref, w_ref, b_ref, o_ref):
    o_ref[0] = _mm(_silu(c_ref[...]), w_ref[0]) + b_ref[0]


def _modulation(cond8, w_ada, b_ada):
    return pl.pallas_call(
        _mod_kernel,
        grid=(DEPTH, 3),
        in_specs=[
            pl.BlockSpec((8, D_MODEL), lambda l, j: (0, 0)),
            pl.BlockSpec((1, D_MODEL, D_MODEL), lambda l, j: (l, 0, j)),
            pl.BlockSpec((1, 1, D_MODEL), lambda l, j: (l, 0, j)),
        ],
        out_specs=pl.BlockSpec((1, 8, D_MODEL), lambda l, j: (l, 0, j)),
        out_shape=jax.ShapeDtypeStruct((DEPTH, 8, 3 * D_MODEL), F32),
        compiler_params=_cparams(("arbitrary", "arbitrary")),
        name="modulation",
    )(cond8, w_ada, b_ada.reshape(DEPTH, 1, 3 * D_MODEL))


def _front_kernel(x_ref, ss_ref, g_ref, w_ref, mu_ref, z_ref, h_ref, *, seq):
    j = pl.program_id(1)

    @pl.when(j == 0)
    def _():
        h = _rms(x_ref[...], g_ref[...]) * (1.0 + ss_ref[0, 1:2, :]) + ss_ref[0, 0:1, :]
        h_ref[...] = h.astype(BF16)

    cols = [slice(c, c + FRONT_COLS) for c in range(0, FRONT_TN, FRONT_COLS)]

    @pl.when(j < N_SHIFT_TILES)
    def _():
        first = lax.broadcasted_iota(jnp.int32, (8, FRONT_COLS), 0) == 0
        last = lax.broadcasted_iota(jnp.int32, (8, FRONT_COLS), 0) == 7
        for cs in cols:
            z = jnp.dot(h_ref[...], w_ref[:, cs], preferred_element_type=F32)
            hmu = 0.5 * mu_ref[:, cs]
            omu = 1.0 - mu_ref[:, cs]
            prev = pltpu.roll(z, 1, axis=0)
            nxt = pltpu.roll(z, FRONT_TM - 1, axis=0)
            z_ref[:, cs] = z * omu + (prev + nxt) * hmu
            for s0 in range(0, FRONT_TM, seq):
                a = slice(s0, s0 + 8)
                z_ref[a, cs] = z[a] * omu + (jnp.where(first, 0.0, prev[a]) + nxt[a]) * hmu
                b = slice(s0 + seq - 8, s0 + seq)
                z_ref[b, cs] = z[b] * omu + (prev[b] + jnp.where(last, 0.0, nxt[b])) * hmu

    @pl.when(j >= N_SHIFT_TILES)
    def _():
        for cs in cols:
            z_ref[:, cs] = jnp.dot(h_ref[...], w_ref[:, cs], preferred_element_type=F32)


def _front(x2d, ss, norm_g, w_in_p, mu_p, seq):
    n = x2d.shape[0]
    return pl.pallas_call(
        functools.partial(_front_kernel, seq=seq),
        grid=(n // FRONT_TM, ZW // FRONT_TN),
        in_specs=[
            pl.BlockSpec((FRONT_TM, D_MODEL), lambda i, j: (i, 0)),
            pl.BlockSpec((1, 2, D_MODEL), lambda i, j: (i, 0, 0)),
            pl.BlockSpec((1, D_MODEL), lambda i, j: (0, 0)),
            pl.BlockSpec((D_MODEL, FRONT_TN), lambda i, j: (0, j)),
            pl.BlockSpec((1, FRONT_TN), lambda i, j: (0, jnp.minimum(j, N_SHIFT_TILES - 1))),
        ],
        out_specs=[pl.BlockSpec((FRONT_TM, FRONT_TN), lambda i, j: (i, j)),
                   pl.BlockSpec((FRONT_TM, D_MODEL), lambda i, j: (i, 0))],
        out_shape=[jax.ShapeDtypeStruct((n, ZW), F32), jax.ShapeDtypeStruct((n, D_MODEL), BF16)],
        compiler_params=_cparams(("arbitrary", "arbitrary")),
        name="front",
    )(x2d, ss, norm_g, w_in_p, mu_p)


def _scan_chunks(dirs, s_ref):
    c = SCAN_C
    n2 = 2 * c
    ri = lax.broadcasted_iota(jnp.int32, (c, c), 0)
    ci = lax.broadcasted_iota(jnp.int32, (c, c), 1)
    row = lax.broadcasted_iota(jnp.int32, (n2, n2), 0)
    col = lax.broadcasted_iota(jnp.int32, (n2, n2), 1)
    same64 = (row >> 6) == (col >> 6)
    same16 = (row >> 4) == (col >> 4)
    tl = row & (c - 1)
    il = col & (c - 1)
    eye = (row == col).astype(F32)
    lane_a = lax.broadcasted_iota(jnp.int32, (c, LANES), 1) < N_A

    def stack2(x):
        return jnp.concatenate([jnp.where(lane_a, x, 0.0), jnp.where(lane_a, 0.0, x)], axis=0)

    lhs, rhs, v2, bk, gtot, strict, incl, key = [], [], [], [], [], [], [], []
    for r, kk, v, lw, kt, bv, rev, j, d in dirs:
        tri = ((ri <= ci) if rev else (ri >= ci)).astype(BF16)
        cs = _mm_split_lhs01(tri, lw)
        tot = cs[0:1] if rev else cs[c - 1:c]
        g_tot = jnp.exp(tot)
        a_t = -kk * jnp.exp(cs - lw)
        r_t = r * jnp.exp(cs)
        g_inv = jnp.exp(-cs)
        b_t = bv * g_inv
        k_t = kt * g_inv
        g_rem = jnp.exp(tot - cs)
        b_h = bv * g_rem
        k_h = kt * g_rem
        st = same64 & ((il > tl) if rev else (il < tl))
        inc = same64 & ((il >= tl) if rev else (il <= tl))
        for p in range(H_A // 2):
            sl = slice(p * LANES, (p + 1) * LANES)
            lhs.append(jnp.concatenate([stack2(a_t[:, sl]), stack2(r_t[:, sl])], axis=0))
            rhs.append(jnp.concatenate([b_t[:, sl], k_t[:, sl]], axis=0))
            v2.append(stack2(v[:, sl]))
            bk.append(jnp.concatenate([stack2(b_h[:, sl]), stack2(k_h[:, sl])], axis=0))
            gtot.append(g_tot[:, sl])
            strict.append(st)
            incl.append(inc)
            key.append((j, d, p))
    ch = range(len(key))

    s = [s_ref[key[i]] for i in ch]
    big = [_mm_nt(lhs[i], jnp.concatenate([rhs[i], s[i]], axis=0)) for i in ch]
    lab, lak, mrbk = [], [], []
    for i in ch:
        x = big[i][:n2, :n2]
        xr = pltpu.roll(x, c, axis=1)
        y = big[i][n2:, :n2]
        yr = pltpu.roll(y, c, axis=1)
        pick = lambda top, bot: jnp.concatenate([top[:c], bot[c:]], axis=0)
        lab.append(jnp.where(strict[i], pick(x, xr), 0.0))
        lak.append(jnp.where(strict[i], pick(xr, x), 0.0))
        mrbk.append(jnp.concatenate([jnp.where(incl[i], pick(y, yr), 0.0),
                                     jnp.where(incl[i], pick(yr, y), 0.0)], axis=1))
    dg = [jnp.where(same16, lab[i], 0.0) for i in ch]
    off = [lab[i] - dg[i] for i in ch]
    pinv = [eye + dg[i] for i in ch]
    pw = [_mm(dg[i], dg[i]) for i in ch]
    for _ in range(2):
        t = [_mm(pw[i], jnp.concatenate([pw[i], pinv[i]], axis=1)) for i in ch]
        pinv = [pinv[i] + t[i][:, n2:] for i in ch]
        pw = [t[i][:, :n2] for i in ch]
    pinv = [pinv[i] + _mm(pw[i], pinv[i]) for i in ch]
    f = [_mm(pinv[i], off[i]) for i in ch]
    t = [_mm(f[i], jnp.concatenate([f[i], pinv[i]], axis=1)) for i in ch]
    g = [pinv[i] + t[i][:, n2:] for i in ch]
    tinv = [g[i] + _mm(t[i][:, :n2], g[i]) for i in ch]
    lv = [_mm(lak[i], v2[i]) for i in ch]
    u2 = [_mm(tinv[i], big[i][:n2, n2:] + lv[i]) for i in ch]
    uv = [jnp.concatenate([u2[i], v2[i]], axis=0) for i in ch]
    y2 = [big[i][n2:, n2:] + _mm(mrbk[i], uv[i]) for i in ch]
    for i in ch:
        s_ref[key[i]] = s[i] * gtot[i] + _mm(uv[i].T, bk[i])
    npair = H_A // 2
    return [jnp.concatenate([y2[g * npair + p][:c] + y2[g * npair + p][c:] for p in range(npair)], axis=1)
            for g in range(len(dirs))]


def _head_sums(x):
    lane_a = lax.broadcasted_iota(jnp.int32, (x.shape[0], LANES), 1) < N_A
    outs = []
    for p in range(H_A // 2):
        xs = x[:, p * LANES:(p + 1) * LANES]
        sa = jnp.sum(jnp.where(lane_a, xs, 0.0), axis=-1, keepdims=True)
        sb = jnp.sum(jnp.where(lane_a, 0.0, xs), axis=-1, keepdims=True)
        outs.append(jnp.where(lane_a, sa, sb))
    return jnp.concatenate(outs, axis=1)


def _rwkv_prepare(k, lora, d, w):
    wlt_ref, wla_ref, w0_ref, a0_ref, kk_ref, ka_ref, rk_ref = w
    hs = slice(d * BRANCH_W, (d + 1) * BRANCH_W)
    lo_w, lo_a = lora[:, :LANES], lora[:, LANES:]
    pre = w0_ref[:, hs] + _mm(jnp.tanh(lo_w), wlt_ref[:LANES, hs])
    lw = -math.exp(-0.5) * jax.nn.sigmoid(pre)
    kkraw = k * kk_ref[...]
    kk = kkraw * lax.rsqrt(jnp.maximum(_head_sums(kkraw * kkraw), 1e-24))
    k_a = ka_ref[...]
    if d:
        a_b = jax.nn.sigmoid(a0_ref[:, hs] + _mm(lo_a, wla_ref[LANES:, hs]))
        return kk, lw, k * (1.0 + (a_b - 1.0) * k_a), kk * a_b
    a2 = jax.nn.sigmoid(a0_ref[...] + _mm(lo_a, wla_ref[LANES:, :]))
    ktf = k * (1.0 + (a2[:, :BRANCH_W] - 1.0) * k_a)
    ktb = k * (1.0 + (a2[:, BRANCH_W:] - 1.0) * k_a)
    return kk, lw, ktf, kk * a2[:, hs], (ktf + ktb) * rk_ref[...]


def _rwkv_kernel(*refs, has_init, n_prev):
    if has_init:
        s0_ref, refs = refs[0], refs[1:]
    prev, refs = refs[:2 * n_prev], refs[2 * n_prev:]
    cur = (refs[0:5], refs[5:10])
    nxt = (refs[10:12], refs[12:14])
    w = refs[14:21]
    lng_ref, lnb_ref = refs[21:23]
    o_ref, sf_ref, sb_ref, s_scr, ysum_scr, kts_scr, p_scr = refs[23:30]
    c = pl.program_id(1)
    nc = pl.num_programs(1)

    @pl.when(c == 0)
    def _():
        if has_init:
            s_scr[...] = s0_ref[...]
        else:
            s_scr[...] = jnp.zeros_like(s_scr)

    @pl.when((c == 0) & (pl.program_id(0) == 0))
    def _():
        for j in range(SCAN_NB):
            for d in range(2):
                for i, a in enumerate(_rwkv_prepare(cur[d][3][j], cur[d][4][j], d, w)):
                    p_scr[j, d, i] = a

    dirs = []
    for j in range(SCAN_NB):
        for d in range(2):
            kk, lw, kt, bv = [p_scr[j, d, i] for i in range(4)]
            dirs.append((cur[d][0][j], kk, cur[d][1][j], lw, kt, bv, d == 1, j, d))
    kts = [p_scr[j, 0, 4] for j in range(SCAN_NB)]
    ys = _scan_chunks(dirs, s_scr)

    for j in range(SCAN_NB):
        for d in range(2):
            for i, a in enumerate(_rwkv_prepare(nxt[d][0][j], nxt[d][1][j], d, w)):
                p_scr[j, d, i] = a

    rows = (pl.multiple_of(c * SCAN_C, SCAN_C), pl.multiple_of((nc - 1 - c) * SCAN_C, SCAN_C))

    @pl.when(c < nc // 2)
    def _():
        for j in range(SCAN_NB):
            kts_scr[j, pl.ds(rows[0], SCAN_C), :] = kts[j]
            for d in range(2):
                ysum_scr[j, pl.ds(rows[d], SCAN_C), :] = ys[2 * j + d]

    @pl.when(c >= nc // 2)
    def _():
        for j in range(SCAN_NB):
            for d in range(2):
                r, v, g = cur[d][0][j], cur[d][1][j], cur[d][2][j]
                bonus = _head_sums(r * (kts_scr[j, pl.ds(rows[1], SCAN_C), :] if d else kts[j])) * v
                o = ysum_scr[j, pl.ds(rows[d], SCAN_C), :] + ys[2 * j + d]
                dlt = o - _head_sums(o) * (1.0 / N_A)
                var = _head_sums(dlt * dlt) * (1.0 / N_A)
                y = dlt * lax.rsqrt(var + RWKV_GN_EPS) * lng_ref[...] + lnb_ref[...] + bonus
                o_ref[j, pl.ds(rows[d], SCAN_C), :] = (y * _silu(g)).astype(BF16)

    @pl.when(c == nc - 1)
    def _():
        for j in range(SCAN_NB):
            for d, st_ref in enumerate((sf_ref, sb_ref)):
                for i in range(n_prev):
                    st_ref[j, i] = prev[d * n_prev + i][j]
                for p in range(H_A // 2):
                    s = s_scr[j, d, p]
                    st_ref[j, n_prev, 2 * p] = s[:N_A, :N_A]
                    st_ref[j, n_prev, 2 * p + 1] = pltpu.roll(s, N_A, axis=1)[N_A:, :N_A]


def _rwkv(z, wp, s0_bd, nb, seq, prev_f=(), prev_b=()):
    n = z.shape[0]
    nc = seq // SCAN_C
    assert nc % 2 == 0 and nb % SCAN_NB == 0
    z3 = z.reshape(nb, seq, ZW)
    chunk = ((lambda c: c), (lambda c: nc - 1 - c))
    spec = lambda idx, col, w=BRANCH_W: pl.BlockSpec((SCAN_NB, SCAN_C, w), lambda b, c: (b, idx(c), col))
    in_specs = []
    for d in range(2):
        in_specs += [spec(chunk[d], COL_R), spec(chunk[d], COL_V), spec(chunk[d], COL_GA),
                     spec(chunk[d], COL_K), spec(chunk[d], COL_LORA, 256)]

    def spec_next(d, col, w=BRANCH_W):
        def index(b, c):
            wrap = c == nc - 1
            nb_i = jnp.minimum(b + wrap.astype(jnp.int32), nb // SCAN_NB - 1)
            return (nb_i, chunk[d](jnp.where(wrap, 0, c + 1)), col)
        return pl.BlockSpec((SCAN_NB, SCAN_C, w), index)

    for d in range(2):
        in_specs += [spec_next(d, COL_K), spec_next(d, COL_LORA, 256)]
    full = lambda a: pl.BlockSpec(a.shape, lambda b, c: (0,) * a.ndim)
    params = (wp["wlt"], wp["wla"], wp["w0"], wp["a0"], wp["k_k"], wp["k_a"], wp["r_k"], wp["ln_g"], wp["ln_b"])
    in_specs += [full(p) for p in params]
    args = [z3] * 14 + list(params)
    sspec = pl.BlockSpec((SCAN_NB, 2, H_A // 2, LANES, LANES), lambda b, c: (b, 0, 0, 0, 0))
    has_init = s0_bd is not None
    if has_init:
        in_specs = [sspec] + in_specs
        args = [s0_bd] + args
    n_prev = len(prev_f)
    pspec = pl.BlockSpec((SCAN_NB, H_A, N_A, N_A), lambda b, c: (b, 0, 0, 0))
    at = 1 if has_init else 0
    in_specs[at:at] = [pspec] * (2 * n_prev)
    args[at:at] = list(prev_f) + list(prev_b)
    fspec = pl.BlockSpec((SCAN_NB, n_prev + 1, H_A, N_A, N_A), lambda b, c: (b, 0, 0, 0, 0))
    o_a, sf, sb = pl.pallas_call(
        functools.partial(_rwkv_kernel, has_init=has_init, n_prev=n_prev),
        grid=(nb // SCAN_NB, nc),
        in_specs=in_specs,
        out_specs=[pl.BlockSpec((SCAN_NB, seq, BRANCH_W), lambda b, c: (b, 0, 0)), fspec, fspec],
        out_shape=[jax.ShapeDtypeStruct((nb, seq, BRANCH_W), BF16),
                   jax.ShapeDtypeStruct((nb, n_prev + 1, H_A, N_A, N_A), F32),
                   jax.ShapeDtypeStruct((nb, n_prev + 1, H_A, N_A, N_A), F32)],
        scratch_shapes=[pltpu.VMEM((SCAN_NB, 2, H_A // 2, LANES, LANES), F32),
                        pltpu.VMEM((SCAN_NB, seq, BRANCH_W), F32),
                        pltpu.VMEM((SCAN_NB, seq, BRANCH_W), F32),
                        pltpu.VMEM((SCAN_NB, 2, 5, SCAN_C, BRANCH_W), F32)],
        compiler_params=_cparams(("arbitrary", "arbitrary")),
        name="rwkv",
    )(*args)
    return o_a.reshape(n, BRANCH_W), sf, sb


def _swap_pairs(x):
    w = x.shape[1]
    even = (lax.broadcasted_iota(jnp.int32, x.shape, 1) & 1) == 0
    return jnp.where(even, pltpu.roll(x, w - 1, axis=1), pltpu.roll(x, 1, axis=1))


def _place_rope_key(kr):
    return jnp.concatenate([pltpu.roll(kr, NOPE, axis=1)] * H_B, axis=1)


def _mla_prep_kernel(*refs, rope):
    zb_ref, gq_ref, gkv_ref, wq_ref, wk_ref, wvt_ref = refs[:6]
    if rope:
        cq_ref, sq_ref, ck_ref, sk_ref = refs[6:10]
        refs = refs[10:]
    else:
        refs = refs[6:]
    q_o, k_o, vt_o, ckv_o, kr_o = refs
    zb = zb_ref[...]
    q = _mm(_rms(zb[:, :Q_LORA], gq_ref[...]), wq_ref[...])
    ckv = _rms(zb[:, Q_LORA:Q_LORA + KV_LORA], gkv_ref[...])
    kr = zb[:, Q_LORA + KV_LORA:]
    if rope:
        cq = jnp.concatenate([cq_ref[...]] * H_B, axis=1)
        sq = jnp.concatenate([sq_ref[...]] * H_B, axis=1)
        q = q * cq + _swap_pairs(q) * sq
        kr = kr * ck_ref[...] + _swap_pairs(kr) * sk_ref[...]
    q_o[...] = (q * Q_PRESCALE).astype(BF16)
    k_o[...] = (_mm(ckv, wk_ref[...]) + _place_rope_key(kr)).astype(BF16)
    vt_o[...] = _mm_nt(wvt_ref[...], ckv).astype(BF16)
    ckv_o[...] = ckv
    kr_o[...] = kr[:, :ROPE]


def _mla_prep(z, mp, rope_tabs, seq, tm=1024):
    n = z.shape[0]
    full = lambda a: pl.BlockSpec(a.shape, lambda i: (0,) * a.ndim)
    params = (mp["q_norm"], mp["kv_norm"], mp["wq"], mp["wk"], mp["wvt"])
    in_specs = [pl.BlockSpec((tm, BRANCH_W), lambda i: (i, COL_ZB))] + [full(p) for p in params]
    args = [z, *params]
    rope = rope_tabs is not None
    if rope:
        per = seq // tm
        in_specs += [pl.BlockSpec((tm, LANES), lambda i: (i % per, 0))] * 4
        args += list(rope_tabs)
    rb = lambda w: pl.BlockSpec((tm, w), lambda i: (i, 0))
    return pl.pallas_call(
        functools.partial(_mla_prep_kernel, rope=rope),
        grid=(n // tm,),
        in_specs=in_specs,
        out_specs=[rb(H_B * LANES), rb(H_B * LANES), pl.BlockSpec((H_B * VDIM, tm), lambda i: (0, i)),
                   rb(KV_LORA), rb(ROPE)],
        out_shape=[jax.ShapeDtypeStruct((n, H_B * LANES), BF16),
                   jax.ShapeDtypeStruct((n, H_B * LANES), BF16),
                   jax.ShapeDtypeStruct((H_B * VDIM, n), BF16),
                   jax.ShapeDtypeStruct((n, KV_LORA), F32),
                   jax.ShapeDtypeStruct((n, ROPE), F32)],
        compiler_params=_cparams(("arbitrary",)),
        name="mla_prep",
    )(*args)


def _kv_expand_kernel(ckv_ref, kr_ref, wk_ref, wvt_ref, k_o, vt_o):
    ckv = ckv_ref[...]
    k_o[...] = (_mm(ckv, wk_ref[...]) + _place_rope_key(kr_ref[...])).astype(BF16)
    vt_o[...] = _mm_nt(wvt_ref[...], ckv).astype(BF16)


def _kv_expand(ckv, kr128, mp):
    m = ckv.shape[0]
    full = lambda a: pl.BlockSpec(a.shape, lambda i: (0,) * a.ndim)
    args = (ckv, kr128, mp["wk"], mp["wvt"])
    return pl.pallas_call(
        _kv_expand_kernel,
        grid=(1,),
        in_specs=[full(a) for a in args],
        out_specs=[pl.BlockSpec((m, H_B * LANES), lambda i: (0, 0)),
                   pl.BlockSpec((H_B * VDIM, m), lambda i: (0, 0))],
        out_shape=[jax.ShapeDtypeStruct((m, H_B * LANES), BF16),
                   jax.ShapeDtypeStruct((H_B * VDIM, m), BF16)],
        compiler_params=_cparams(("arbitrary",)),
        name="mla_ctx_kv",
    )(*args)


def _attn_kernel(*refs, has_ctx, group):
    if has_ctx:
        q_ref, k_ref, vt_ref, kc_ref, vct_ref, g_ref, o_ref = refs
    else:
        q_ref, k_ref, vt_ref, g_ref, o_ref = refs
    nbs, seq = k_ref.shape[0], k_ref.shape[1]
    nt = (((1,), (1,)), ((), ()))
    groups = [(b, h0) for b in range(nbs) for h0 in range(0, H_B, group)]

    def scores(b, h0):
        out = []
        for h in range(h0, h0 + group):
            hs = slice(h * LANES, (h + 1) * LANES)
            s = lax.dot_general(k_ref[b, :, hs], q_ref[b, :, hs], nt, preferred_element_type=F32)
            sc = None
            if has_ctx:
                sc = lax.dot_general(kc_ref[:, hs], q_ref[b, :, hs], nt, preferred_element_type=F32)
            out.append((s, sc))
        return out

    pending = scores(*groups[0])
    outs = []
    for gi, (b, h0) in enumerate(groups):
        cur = pending
        if gi + 1 < len(groups):
            pending = scores(*groups[gi + 1])
        keys = slice(b * seq, (b + 1) * seq)
        m, e, ec, den = [], [], [], []
        for s, sc in cur:
            mi = jnp.max(s, axis=0, keepdims=True)
            if has_ctx:
                mi = jnp.maximum(mi, jnp.max(sc, axis=0, keepdims=True))
            m.append(mi)
        for (s, sc), mi in zip(cur, m):
            ei = jnp.exp2(s - mi)
            di = jnp.sum(ei, axis=0, keepdims=True)
            e.append(ei.astype(BF16))
            if has_ctx:
                eci = jnp.exp2(sc - mi)
                di = di + jnp.sum(eci, axis=0, keepdims=True)
                ec.append(eci.astype(BF16))
            den.append(di)
        for i in range(group):
            vh = slice((h0 + i) * VDIM, (h0 + i + 1) * VDIM)
            o = jnp.dot(vt_ref[vh, keys], e[i], preferred_element_type=F32)
            if has_ctx:
                o = o + jnp.dot(vct_ref[vh, :], ec[i], preferred_element_type=F32)
            outs.append(o / den[i])
        if h0 + group == H_B:
            o_ref[b] = (jnp.concatenate(outs, axis=0).T * _silu(g_ref[b])).astype(BF16)
            outs = []


def _attention(z, q, k, vt, kc, vct, nb, seq):
    n = z.shape[0]
    tq = min(seq, ATTN_TQ)
    nq = seq // tq
    has_ctx = kc is not None
    nbs = 1 if has_ctx else max(1, min(nb, ATTN_ROWS // seq))
    z3 = z.reshape(nb, seq, ZW)
    in_specs = [pl.BlockSpec((nbs, tq, H_B * LANES), lambda b, i: (b, i, 0)),
                pl.BlockSpec((nbs, seq, H_B * LANES), lambda b, i: (b, 0, 0)),
                pl.BlockSpec((H_B * VDIM, nbs * seq), lambda b, i: (0, b))]
    args = [q.reshape(nb, seq, H_B * LANES), k.reshape(nb, seq, H_B * LANES), vt]
    if has_ctx:
        past = kc.shape[0] // nb
        in_specs += [pl.BlockSpec((past, H_B * LANES), lambda b, i: (b, 0)),
                     pl.BlockSpec((H_B * VDIM, past), lambda b, i: (0, b))]
        args += [kc, vct]
    in_specs.append(pl.BlockSpec((nbs, tq, BRANCH_W), lambda b, i: (b, i, COL_GB)))
    args.append(z3)
    out = pl.pallas_call(
        functools.partial(_attn_kernel, has_ctx=has_ctx, group=ATTN_GROUP_LONG if has_ctx else ATTN_GROUP),
        grid=(nb // nbs, nq),
        in_specs=in_specs,
        out_specs=pl.BlockSpec((nbs, tq, BRANCH_W), lambda b, i: (b, i, 0)),
        out_shape=jax.ShapeDtypeStruct((nb, seq, BRANCH_W), BF16),
        compiler_params=_cparams(("arbitrary", "arbitrary")),
        name="mla_attention",
    )(*args)
    return out.reshape(n, BRANCH_W)


def _gmlp_kernel(u_ref, vc_ref, g_ref, lng_ref, lnb_ref, ws_ref, bs_ref, o_ref):
    tm = u_ref.shape[0]
    u = jax.nn.gelu(u_ref[...])
    x = jax.nn.gelu(vc_ref[...])
    mu = jnp.mean(x, axis=-1, keepdims=True)
    xc = x - mu
    var = jnp.mean(xc * xc, axis=-1, keepdims=True)
    vn = (xc * lax.rsqrt(var + 1e-5) * lng_ref[...] + lnb_ref[...]).astype(BF16)
    rows = []
    for c in range(tm // CHUNK):
        cols = []
        for g in range(G_C):
            blk = vn[c * CHUNK:(c + 1) * CHUNK, g * LANES:(g + 1) * LANES]
            cols.append(jnp.dot(ws_ref[g], blk, preferred_element_type=F32))
        rows.append(jnp.concatenate(cols, axis=1) + bs_ref[...])
    mixed = jnp.concatenate(rows, axis=0)
    o_ref[...] = (u * mixed * _silu(g_ref[...])).astype(BF16)


def _gmlp(z, gp, tm=1024):
    n = z.shape[0]
    zb = lambda c: pl.BlockSpec((tm, BRANCH_W), lambda i, c=c: (i, c))
    full = lambda a: pl.BlockSpec(a.shape, lambda i: (0,) * a.ndim)
    params = (gp["ln_g"], gp["ln_b"], gp["w_s"], gp["b_s"])
    return pl.pallas_call(
        _gmlp_kernel,
        grid=(n // tm,),
        in_specs=[zb(COL_U), zb(COL_VC), zb(COL_GC)] + [full(p) for p in params],
        out_specs=pl.BlockSpec((tm, BRANCH_W), lambda i: (i, 0)),
        out_shape=jax.ShapeDtypeStruct((n, BRANCH_W), BF16),
        compiler_params=_cparams(("arbitrary",)),
        name="gmlp",
    )(z, z, z, *params)


def _fnet_kernel(f_ref, g_ref, cs_ref, dft_ref, o_ref, xcs_scr, *, seq):
    r = pl.program_id(1)
    nbs = f_ref.shape[0]

    @pl.when(r == 0)
    def _():
        for b in range(nbs):
            for g in range(G_D):
                xg = _mm(f_ref[b, :, g * LANES:(g + 1) * LANES], cs_ref[...])
                xcs_scr[b, 0:seq, g * LANES:(g + 1) * LANES] = xg[:, :LANES].astype(BF16)
                xcs_scr[b, seq:2 * seq, g * LANES:(g + 1) * LANES] = xg[:, LANES:].astype(BF16)

    for b in range(nbs):
        y = jnp.dot(dft_ref[...], xcs_scr[b], preferred_element_type=F32)
        o_ref[b] = (y * (1.0 / math.sqrt(seq * LANES)) * _silu(g_ref[b])).astype(BF16)


def _fnet(z, cs128, dft, nb, seq):
    n = z.shape[0]
    tr = min(seq, FNET_ROWS)
    nr = seq // tr
    nbs = max(1, min(nb, FRONT_TM // seq))
    z3 = z.reshape(nb, seq, ZW)
    out = pl.pallas_call(
        functools.partial(_fnet_kernel, seq=seq),
        grid=(nb // nbs, nr),
        in_specs=[pl.BlockSpec((nbs, seq, BRANCH_W), lambda b, r: (b, 0, COL_F)),
                  pl.BlockSpec((nbs, tr, BRANCH_W), lambda b, r: (b, r, COL_GD)),
                  pl.BlockSpec((LANES, 2 * LANES), lambda b, r: (0, 0)),
                  pl.BlockSpec((tr, 2 * seq), lambda b, r: (r, 0))],
        out_specs=pl.BlockSpec((nbs, tr, BRANCH_W), lambda b, r: (b, r, 0)),
        out_shape=jax.ShapeDtypeStruct((nb, seq, BRANCH_W), BF16),
        scratch_shapes=[pltpu.VMEM((nbs, 2 * seq, BRANCH_W), BF16)],
        compiler_params=_cparams(("arbitrary", "arbitrary")),
        name="fnet",
    )(z3, z3, cs128, dft)
    return out.reshape(n, BRANCH_W)


def _back_kernel(x_ref, h_ref, gate_ref, oa_ref, ob_ref, oc_ref, od_ref, wm_ref, bm_ref,
                 wb_ref, wo_ref, fg_ref, y_ref, acc_scr, *, final):
    nb = pl.program_id(1)
    tm = x_ref.shape[0]
    chunks = [pl.ds(r, BACK_ROWS) for r in range(0, tm, BACK_ROWS)]

    for i, o_ref in enumerate((oa_ref, ob_ref, oc_ref, od_ref)):
        @pl.when(nb == i)
        def _(o_ref=o_ref, i=i):
            for rows in chunks:
                gates = jax.nn.sigmoid(jnp.dot(h_ref[rows, :], wm_ref[...], preferred_element_type=F32)
                                       + bm_ref[...])
                upd = gates * jnp.dot(o_ref[rows, :], wb_ref[0], preferred_element_type=F32)
                if i == 0:
                    acc_scr[rows, :] = upd
                else:
                    acc_scr[rows, :] += upd

    @pl.when(nb == 3)
    def _():
        for rows in chunks:
            xn = x_ref[rows, :] + gate_ref[0] * _mm(acc_scr[rows, :], wo_ref[...])
            if final:
                xn = _rms(xn, fg_ref[...])
            y_ref[rows, :] = xn


def _back(x2d, h, gate, o_a, o_b, o_c, o_d, wts, l, final_g, final, tm=1024):
    n = x2d.shape[0]
    rows_per_gate = FRONT_TM // tm
    ob = pl.BlockSpec((tm, BRANCH_W), lambda i, j: (i, 0))
    return pl.pallas_call(
        functools.partial(_back_kernel, final=final),
        grid=(n // tm, 4),
        in_specs=[pl.BlockSpec((tm, D_MODEL), lambda i, j: (i, 0)),
                  pl.BlockSpec((tm, D_MODEL), lambda i, j: (i, 0)),
                  pl.BlockSpec((1, 1, D_MODEL), lambda i, j: (i // rows_per_gate, 0, 0)),
                  ob, ob, ob, ob,
                  pl.BlockSpec((None, D_MODEL, D_MODEL), lambda i, j: (l, 0, j)),
                  pl.BlockSpec((None, 1, D_MODEL), lambda i, j: (l, 0, j)),
                  pl.BlockSpec((None, 1, BRANCH_W, D_MODEL), lambda i, j: (l, j, 0, 0)),
                  pl.BlockSpec((None, D_MODEL, D_MODEL), lambda i, j: (l, 0, 0)),
                  pl.BlockSpec((1, D_MODEL), lambda i, j: (0, 0))],
        out_specs=pl.BlockSpec((tm, D_MODEL), lambda i, j: (i, 0)),
        out_shape=jax.ShapeDtypeStruct((n, D_MODEL), F32),
        scratch_shapes=[pltpu.VMEM((tm, D_MODEL), F32)],
        compiler_params=_cparams(("arbitrary", "arbitrary")),
        name="back",
    )(x2d, h, gate, o_a, o_b, o_c, o_d, wts["w_merge"], wts["b_merge"], wts["w_branch"],
      wts["w_out"], final_g)


def _pad_cols_kernel(w_ref, o_ref):
    rows = w_ref.shape[0]
    o_ref[:, :SHIFT_W] = w_ref[:, :SHIFT_W].astype(BF16)
    o_ref[:, SHIFT_W:Z_SHIFT_PAD] = jnp.zeros((rows, Z_SHIFT_PAD - SHIFT_W), BF16)
    o_ref[:, Z_SHIFT_PAD:Z_SHIFT_PAD + 928] = w_ref[:, SHIFT_W:2656].astype(BF16)
    o_ref[:, Z_SHIFT_PAD + 928:3072] = jnp.zeros((rows, 3072 - Z_SHIFT_PAD - 928), BF16)
    o_ref[:, 3072:] = w_ref[:, 2656:].astype(BF16)


def _pad_cols(w, l, tm=256):
    _, k, n_in = w.shape
    return pl.pallas_call(
        _pad_cols_kernel,
        grid=(k // tm,),
        in_specs=[pl.BlockSpec((tm, n_in), lambda i: (l * (k // tm) + i, 0))],
        out_specs=pl.BlockSpec((tm, ZW), lambda i: (i, 0)),
        out_shape=jax.ShapeDtypeStruct((k, ZW), BF16),
        compiler_params=_cparams(("arbitrary",)),
        name="pad_w_in",
    )(w.reshape(-1, n_in))


def _pack_q(w):
    w = w.reshape(Q_LORA, H_B, NOPE + ROPE)
    return jnp.pad(w, ((0, 0), (0, 0), (0, LANES - NOPE - ROPE))).reshape(Q_LORA, H_B * LANES)


def _pack_kv(w):
    w = w.reshape(KV_LORA, H_B, NOPE + VDIM)
    wk = jnp.pad(w[..., :NOPE], ((0, 0), (0, 0), (0, LANES - NOPE))).reshape(KV_LORA, H_B * LANES)
    return wk, w[..., NOPE:].reshape(KV_LORA, H_B * VDIM)


def _rope_tables(n_tokens):
    rows = n_tokens // GRID_W
    row = jnp.repeat(jnp.arange(rows, dtype=F32), GRID_W)
    col = jnp.tile(jnp.arange(GRID_W, dtype=F32), rows)
    n_freq = ROPE // 4
    inv = ROPE_BASE ** (-jnp.arange(n_freq, dtype=F32) / n_freq)
    ang = jnp.concatenate([row[:, None] * inv, col[:, None] * inv], axis=-1)
    cos = jnp.repeat(jnp.cos(ang), 2, axis=-1)
    sin = jnp.repeat(jnp.sin(ang), 2, axis=-1) * jnp.tile(jnp.asarray([-1.0, 1.0], F32), ROPE // 2)
    ones = lambda k: jnp.ones((n_tokens, k), F32)
    zeros = lambda k: jnp.zeros((n_tokens, k), F32)
    cq = jnp.concatenate([ones(NOPE), cos, ones(LANES - NOPE - ROPE)], axis=-1)
    sq = jnp.concatenate([zeros(NOPE), sin, zeros(LANES - NOPE - ROPE)], axis=-1)
    ck = jnp.concatenate([cos, ones(LANES - ROPE)], axis=-1)
    sk = jnp.concatenate([sin, zeros(LANES - ROPE)], axis=-1)
    return cq, sq, ck, sk


def _dft_tables(seq):
    k = np.arange(LANES)
    a = 2.0 * np.pi * ((k[:, None] * k[None, :]) % LANES) / LANES
    cs128 = np.concatenate([np.cos(a), np.sin(a)], axis=1)
    t = np.arange(seq)
    b = 2.0 * np.pi * ((t[:, None] * t[None, :]) % seq) / seq
    dft = np.concatenate([np.cos(b), -np.sin(b)], axis=1)
    return (jnp.asarray(cs128.astype(np.float32)).astype(BF16),
            jnp.asarray(dft.astype(np.float32)).astype(BF16))


def _lora_weights(w_up, a_up):
    z = jnp.zeros((LORA, BRANCH_W), F32)
    pad = jnp.zeros((256 - 3 * LORA, 2 * BRANCH_W), F32)
    wlt = jnp.concatenate([jnp.concatenate([w_up[0], z], 1), jnp.concatenate([z, w_up[1]], 1),
                           jnp.concatenate([z, z], 1), pad], axis=0)
    wla = jnp.concatenate([jnp.concatenate([z, z], 1), jnp.concatenate([z, z], 1),
                           jnp.concatenate([a_up[0], a_up[1]], 1), pad], axis=0)
    return wlt.astype(BF16), wla.astype(BF16)


def _to_blockdiag(s):
    b = s.shape[0]
    s = s.reshape(b, H_A // 2, 2, N_A, N_A)
    z = jnp.zeros_like(s[:, :, 0])
    top = jnp.concatenate([s[:, :, 0], z], axis=-1)
    bot = jnp.concatenate([z, s[:, :, 1]], axis=-1)
    return jnp.concatenate([top, bot], axis=-2)


def _trunk_layer(x2d, ss, lw, wts, l, nb, seq, rope_tabs, ctx, final_g, prev_f=(), prev_b=()):
    z, h = _front(x2d, ss[:, :2], lw["norm_g"], lw["w_in"], lw["mu"], seq)
    s0 = None
    if ctx is not None:
        s0 = jnp.stack([_to_blockdiag(ctx[0]), _to_blockdiag(ctx[1])], axis=1)
    o_a, sf, sb = _rwkv(z, lw["rwkv"], s0, nb, seq, prev_f, prev_b)
    q, k, v, ckv, kr = _mla_prep(z, lw["mla"], rope_tabs, seq)
    kc = vc = None
    if ctx is not None:
        past = ctx[2].shape[1]
        kr_c = jnp.pad(ctx[3].reshape(nb * past, ROPE), ((0, 0), (0, LANES - ROPE)))
        kc, vc = _kv_expand(ctx[2].reshape(nb * past, KV_LORA), kr_c, lw["mla"])
    o_b = _attention(z, q, k, v, kc, vc, nb, seq)
    o_c = _gmlp(z, lw["gmlp"])
    o_d = _fnet(z, *lw["dft"][seq], nb, seq)
    x_new = _back(x2d, h, ss[:, 2:], o_a, o_b, o_c, o_d, wts, l, final_g, l == DEPTH - 1)
    return x_new, sf, sb, ckv, kr


def kernel(x_prompt, x_sample, state_rwkv_fwd, state_rwkv_bwd, cache_mla_ckv, cache_mla_krope, c, c_ctx, norm_g, w_ada, b_ada, w_in, shift_mu, rwkv_w0, rwkv_w_up, rwkv_a0, rwkv_a_up, rwkv_k_k, rwkv_k_a, rwkv_r_k, rwkv_ln_g, rwkv_ln_b, mla_q_norm, mla_w_q_up, mla_kv_norm, mla_w_kv_up, gmlp_ln_g, gmlp_ln_b, gmlp_w_s, gmlp_b_s, w_branch, w_merge, b_merge, w_out, final_norm_g):
    nb_c, seq_c, _ = x_prompt.shape
    nb_l, seq_l, _ = x_sample.shape
    assert (nb_c * seq_c) % FRONT_TM == 0 and FRONT_TM % seq_c == 0 and seq_l == FRONT_TM

    cond8 = jnp.concatenate([c_ctx[None], c, jnp.zeros((8 - 1 - nb_l, D_MODEL), F32)], axis=0)
    mod = _modulation(cond8, w_ada, b_ada).reshape(DEPTH, 8, 3, D_MODEL)
    n_ctx_tiles = nb_c * seq_c // FRONT_TM

    rope_tabs = _rope_tables(seq_l)
    dft = {s: _dft_tables(s) for s in {seq_c, seq_l}}
    mu_p = jnp.pad(shift_mu, ((0, 0), (0, Z_SHIFT_PAD - SHIFT_W)))[:, None]
    final_g = final_norm_g[None]
    wts = {"w_merge": w_merge.astype(BF16), "b_merge": b_merge[:, None], "w_branch": w_branch.astype(BF16),
           "w_out": w_out.astype(BF16)}

    xc = x_prompt.reshape(nb_c * seq_c, D_MODEL)
    xl = x_sample.reshape(nb_l * seq_l, D_MODEL)
    sf = sb = None
    ckv_list, kr_list = [], []
    for l in range(DEPTH):
        wlt, wla = _lora_weights(rwkv_w_up[l], rwkv_a_up[l])
        wk, wv = _pack_kv(mla_w_kv_up[l])
        lw = {
            "norm_g": norm_g[l][None], "w_in": _pad_cols(w_in, l), "mu": mu_p[l],
            "rwkv": {
                "wlt": wlt, "wla": wla, "w0": rwkv_w0[l].reshape(1, 2 * BRANCH_W),
                "a0": rwkv_a0[l].reshape(1, 2 * BRANCH_W), "k_k": rwkv_k_k[l][None],
                "k_a": rwkv_k_a[l][None], "r_k": rwkv_r_k[l].reshape(1, BRANCH_W),
                "ln_g": rwkv_ln_g[l][None], "ln_b": rwkv_ln_b[l][None],
            },
            "mla": {
                "q_norm": mla_q_norm[l][None], "kv_norm": mla_kv_norm[l][None],
                "wq": _pack_q(mla_w_q_up[l]).astype(BF16), "wk": wk.astype(BF16),
                "wvt": wv.T.astype(BF16),
            },
            "gmlp": {
                "ln_g": gmlp_ln_g[l][None], "ln_b": gmlp_ln_b[l][None],
                "w_s": gmlp_w_s[l].astype(BF16),
                "b_s": jnp.repeat(gmlp_b_s[l].T, BRANCH_W // G_C, axis=1),
            },
            "dft": dft,
        }
        ss_c = jnp.broadcast_to(mod[l, 0][None], (n_ctx_tiles, 3, D_MODEL))
        ss_l = mod[l, 1:1 + nb_l]
        xc, sf, sb, ckv, kr = _trunk_layer(xc, ss_c, lw, wts, l, nb_c, seq_c, None, None, final_g,
                                           [sf[:, i] for i in range(l)] if l else (),
                                           [sb[:, i] for i in range(l)] if l else ())
        ckv_list.append(ckv.reshape(nb_c, seq_c, KV_LORA))
        kr_list.append(kr.reshape(nb_c, seq_c, ROPE))
        ctx = (state_rwkv_fwd[:, l], state_rwkv_bwd[:, l], cache_mla_ckv[:, l], cache_mla_krope[:, l])
        xl = _trunk_layer(xl, ss_l, lw, wts, l, nb_l, seq_l, rope_tabs, ctx, final_g)[0]

    return (xc.reshape(nb_c, seq_c, D_MODEL), xl.reshape(nb_l, seq_l, D_MODEL),
            sf, sb,
            jnp.stack(ckv_list, axis=1), jnp.stack(kr_list, axis=1))
```

```python
import functools
import math

import jax
import jax.numpy as jnp
import numpy as np
from jax import lax
from jax.experimental import pallas as pl
from jax.experimental.pallas import tpu as pltpu

F32 = jnp.float32
BF16 = jnp.bfloat16

D_MODEL = 1024
DEPTH = 2
GRID_W = 64
BRANCH_W = 512
H_A = 8
N_A = 64
LORA = 64
RWKV_GN_EPS = 64e-5
H_B = 8
NOPE = 64
ROPE = 32
VDIM = 64
Q_LORA = 256
KV_LORA = 128
ROPE_BASE = 10000.0
Q_PRESCALE = (NOPE + ROPE) ** -0.5 * math.log2(math.e)
G_C = 4
CHUNK = 128
G_D = 4
NORM_EPS = 1e-6
SHIFT_W = 1728

LANES = 128
VMEM_LIMIT = 52 * 1024 * 1024
FRONT_VMEM_LIMIT = 58 * 1024 * 1024

ZW = 6144
Z_SHIFT_PAD = 2048
COL_R, COL_K, COL_V = 0, 1, 2
COL_LORA = 6
COL_GA, COL_ZB = 4, 5
Z_HI = 3072
COL_GB, COL_U, COL_VC, COL_GC, COL_F, COL_GD = 0, 1, 2, 3, 4, 5
FRONT_TM = 2048
FRONT_TN = 1024
FRONT_COLS = 256
N_SHIFT_TILES = Z_SHIFT_PAD // FRONT_TN
SCAN_C = 64
SCAN_NB = 2
ATTN_GROUP = 4
ATTN_GROUP_LONG = 2
ATTN_TQ = 512
FNET_ROWS = 1024
ATTN_ROWS = 1024
BACK_ROWS = 256


def _mm(a, b):
    return jnp.dot(a.astype(BF16), b.astype(BF16), preferred_element_type=F32)


def _mm_nt(a, b):
    return lax.dot_general(a.astype(BF16), b.astype(BF16), (((1,), (1,)), ((), ())),
                           preferred_element_type=F32)


def _mm_split_lhs01(m01, x):
    hi = x.astype(BF16)
    lo = (x - hi.astype(F32)).astype(BF16)
    d = functools.partial(jnp.dot, preferred_element_type=F32)
    return d(m01, hi) + d(m01, lo)


def _silu(x):
    return x * jax.nn.sigmoid(x)


def _rms(x, g):
    return x * lax.rsqrt(jnp.mean(x * x, axis=-1, keepdims=True) + NORM_EPS) * g


def _cparams(sem, vmem=VMEM_LIMIT):
    return pltpu.CompilerParams(dimension_semantics=sem, vmem_limit_bytes=vmem)


def _mod_kernel(c_ref, w_ref, b_ref, o_ref):
    o_ref[0] = _mm(_silu(c_ref[...]), w_ref[0]) + b_ref[0]


def _modulation(cond8, w_ada, b_ada):
    return pl.pallas_call(
        _mod_kernel,
        grid=(DEPTH, 3),
        in_specs=[
            pl.BlockSpec((8, D_MODEL), lambda l, j: (0, 0)),
            pl.BlockSpec((1, D_MODEL, D_MODEL), lambda l, j: (l, 0, j)),
            pl.BlockSpec((1, 1, D_MODEL), lambda l, j: (l, 0, j)),
        ],
        out_specs=pl.BlockSpec((1, 8, D_MODEL), lambda l, j: (l, 0, j)),
        out_shape=jax.ShapeDtypeStruct((DEPTH, 8, 3 * D_MODEL), F32),
        compiler_params=_cparams(("arbitrary", "arbitrary")),
        name="modulation",
    )(cond8, w_ada, b_ada.reshape(DEPTH, 1, 3 * D_MODEL))


def _front_kernel(x_ref, ss_ref, g_ref, w_ref, mu_ref, z_ref, zl_ref, h_ref, *, seq):
    j = pl.program_id(1)

    @pl.when(j == 0)
    def _():
        h = _rms(x_ref[...], g_ref[...]) * (1.0 + ss_ref[0, 1:2, :]) + ss_ref[0, 0:1, :]
        h_ref[...] = h.astype(BF16)

    cols = [slice(c, c + FRONT_COLS) for c in range(0, FRONT_TN, FRONT_COLS)]

    @pl.when(j < N_SHIFT_TILES)
    def _():
        first = lax.broadcasted_iota(jnp.int32, (8, FRONT_COLS), 0) == 0
        last = lax.broadcasted_iota(jnp.int32, (8, FRONT_COLS), 0) == 7
        for cs in cols:
            z = jnp.dot(h_ref[...], w_ref[:, cs], preferred_element_type=F32)
            hmu = 0.5 * mu_ref[:, cs]
            omu = 1.0 - mu_ref[:, cs]
            prev = pltpu.roll(z, 1, axis=0)
            nxt = pltpu.roll(z, FRONT_TM - 1, axis=0)
            z_ref[:, cs] = z * omu + (prev + nxt) * hmu
            for s0 in range(0, FRONT_TM, seq):
                a = slice(s0, s0 + 8)
                z_ref[a, cs] = z[a] * omu + (jnp.where(first, 0.0, prev[a]) + nxt[a]) * hmu
                b = slice(s0 + seq - 8, s0 + seq)
                z_ref[b, cs] = z[b] * omu + (prev[b] + jnp.where(last, 0.0, nxt[b])) * hmu

    @pl.when((j >= N_SHIFT_TILES) & (j < Z_HI // FRONT_TN))
    def _():
        for cs in cols:
            z_ref[:, cs] = jnp.dot(h_ref[...], w_ref[:, cs], preferred_element_type=F32)

    @pl.when(j >= Z_HI // FRONT_TN)
    def _():
        for cs in cols:
            zl_ref[:, cs] = jnp.dot(h_ref[...], w_ref[:, cs], preferred_element_type=F32).astype(BF16)


def _front(x2d, ss, norm_g, w_in_p, mu_p, seq):
    n = x2d.shape[0]
    n_hi = Z_HI // FRONT_TN
    return pl.pallas_call(
        functools.partial(_front_kernel, seq=seq),
        grid=(n // FRONT_TM, ZW // FRONT_TN),
        in_specs=[
            pl.BlockSpec((FRONT_TM, D_MODEL), lambda i, j: (i, 0)),
            pl.BlockSpec((1, 2, D_MODEL), lambda i, j: (i, 0, 0)),
            pl.BlockSpec((1, D_MODEL), lambda i, j: (0, 0)),
            pl.BlockSpec((D_MODEL, FRONT_TN), lambda i, j: (0, j)),
            pl.BlockSpec((1, FRONT_TN), lambda i, j: (0, jnp.minimum(j, N_SHIFT_TILES - 1))),
        ],
        out_specs=[pl.BlockSpec((FRONT_TM, FRONT_TN), lambda i, j: (i, jnp.minimum(j, n_hi - 1))),
                   pl.BlockSpec((FRONT_TM, FRONT_TN), lambda i, j: (i, jnp.maximum(j - n_hi, 0))),
                   pl.BlockSpec((FRONT_TM, D_MODEL), lambda i, j: (i, 0))],
        out_shape=[jax.ShapeDtypeStruct((n, Z_HI), F32), jax.ShapeDtypeStruct((n, ZW - Z_HI), BF16),
                   jax.ShapeDtypeStruct((n, D_MODEL), BF16)],
        compiler_params=_cparams(("arbitrary", "arbitrary"), FRONT_VMEM_LIMIT),
        name="front",
    )(x2d, ss, norm_g, w_in_p, mu_p)


def _scan_chunks(dirs, s_ref):
    c = SCAN_C
    n2 = 2 * c
    ri = lax.broadcasted_iota(jnp.int32, (c, c), 0)
    ci = lax.broadcasted_iota(jnp.int32, (c, c), 1)
    row = lax.broadcasted_iota(jnp.int32, (n2, n2), 0)
    col = lax.broadcasted_iota(jnp.int32, (n2, n2), 1)
    same64 = (row >> 6) == (col >> 6)
    same16 = (row >> 4) == (col >> 4)
    tl = row & (c - 1)
    il = col & (c - 1)
    eye = (row == col).astype(F32)
    lane_a = lax.broadcasted_iota(jnp.int32, (c, LANES), 1) < N_A

    def stack2(x):
        return jnp.concatenate([jnp.where(lane_a, x, 0.0), jnp.where(lane_a, 0.0, x)], axis=0)

    lhs, rhs, v2, bk, gtot, strict, incl, key = [], [], [], [], [], [], [], []
    for r, kk, v, lw, kt, bv, rev, j, d in dirs:
        tri = ((ri <= ci) if rev else (ri >= ci)).astype(BF16)
        cs = _mm_split_lhs01(tri, lw)
        tot = cs[0:1] if rev else cs[c - 1:c]
        g_tot = jnp.exp(tot)
        a_t = -kk * jnp.exp(cs - lw)
        r_t = r * jnp.exp(cs)
        g_inv = jnp.exp(-cs)
        b_t = bv * g_inv
        k_t = kt * g_inv
        g_rem = jnp.exp(tot - cs)
        b_h = bv * g_rem
        k_h = kt * g_rem
        st = same64 & ((il > tl) if rev else (il < tl))
        inc = same64 & ((il >= tl) if rev else (il <= tl))
        for p in range(H_A // 2):
            sl = slice(p * LANES, (p + 1) * LANES)
            lhs.append(jnp.concatenate([stack2(a_t[:, sl]), stack2(r_t[:, sl])], axis=0))
            rhs.append(jnp.concatenate([b_t[:, sl], k_t[:, sl]], axis=0))
            v2.append(stack2(v[:, sl]))
            bk.append(jnp.concatenate([stack2(b_h[:, sl]), stack2(k_h[:, sl])], axis=0))
            gtot.append(g_tot[:, sl])
            strict.append(st)
            incl.append(inc)
            key.append((j, d, p))
    ch = range(len(key))

    s = [s_ref[key[i]] for i in ch]
    big = [_mm_nt(lhs[i], jnp.concatenate([rhs[i], s[i]], axis=0)) for i in ch]
    lab, lak, mrbk = [], [], []
    for i in ch:
        x = big[i][:n2, :n2]
        xr = pltpu.roll(x, c, axis=1)
        y = big[i][n2:, :n2]
        yr = pltpu.roll(y, c, axis=1)
        pick = lambda top, bot: jnp.concatenate([top[:c], bot[c:]], axis=0)
        lab.append(jnp.where(strict[i], pick(x, xr), 0.0))
        lak.append(jnp.where(strict[i], pick(xr, x), 0.0))
        mrbk.append(jnp.concatenate([jnp.where(incl[i], pick(y, yr), 0.0),
                                     jnp.where(incl[i], pick(yr, y), 0.0)], axis=1))
    dg = [jnp.where(same16, lab[i], 0.0) for i in ch]
    off = [lab[i] - dg[i] for i in ch]
    pinv = [eye + dg[i] for i in ch]
    pw = [_mm(dg[i], dg[i]) for i in ch]
    for _ in range(2):
        t = [_mm(pw[i], jnp.concatenate([pw[i], pinv[i]], axis=1)) for i in ch]
        pinv = [pinv[i] + t[i][:, n2:] for i in ch]
        pw = [t[i][:, :n2] for i in ch]
    pinv = [pinv[i] + _mm(pw[i], pinv[i]) for i in ch]
    f = [_mm(pinv[i], off[i]) for i in ch]
    t = [_mm(f[i], jnp.concatenate([f[i], pinv[i]], axis=1)) for i in ch]
    g = [pinv[i] + t[i][:, n2:] for i in ch]
    tinv = [g[i] + _mm(t[i][:, :n2], g[i]) for i in ch]
    lv = [_mm(lak[i], v2[i]) for i in ch]
    u2 = [_mm(tinv[i], big[i][:n2, n2:] + lv[i]) for i in ch]
    uv = [jnp.concatenate([u2[i], v2[i]], axis=0) for i in ch]
    y2 = [big[i][n2:, n2:] + _mm(mrbk[i], uv[i]) for i in ch]
    for i in ch:
        s_ref[key[i]] = s[i] * gtot[i] + _mm(uv[i].T, bk[i])
    npair = H_A // 2
    return [jnp.concatenate([y2[g * npair + p][:c] + y2[g * npair + p][c:] for p in range(npair)], axis=1)
            for g in range(len(dirs))]


def _head_sums(x):
    lane_a = lax.broadcasted_iota(jnp.int32, (x.shape[0], LANES), 1) < N_A
    outs = []
    for p in range(H_A // 2):
        xs = x[:, p * LANES:(p + 1) * LANES]
        sa = jnp.sum(jnp.where(lane_a, xs, 0.0), axis=-1, keepdims=True)
        sb = jnp.sum(jnp.where(lane_a, 0.0, xs), axis=-1, keepdims=True)
        outs.append(jnp.where(lane_a, sa, sb))
    return jnp.concatenate(outs, axis=1)


def _rwkv_prepare(k, lora, d, w):
    wlt_ref, wla_ref, w0_ref, a0_ref, kk_ref, ka_ref, rk_ref = w
    hs = slice(d * BRANCH_W, (d + 1) * BRANCH_W)
    lo_w, lo_a = lora[:, :LANES], lora[:, LANES:]
    pre = w0_ref[:, hs] + _mm(jnp.tanh(lo_w), wlt_ref[:LANES, hs])
    lw = -math.exp(-0.5) * jax.nn.sigmoid(pre)
    kkraw = k * kk_ref[...]
    kk = kkraw * lax.rsqrt(jnp.maximum(_head_sums(kkraw * kkraw), 1e-24))
    k_a = ka_ref[...]
    if d:
        a_b = jax.nn.sigmoid(a0_ref[:, hs] + _mm(lo_a, wla_ref[LANES:, hs]))
        return kk, lw, k * (1.0 + (a_b - 1.0) * k_a), kk * a_b
    a2 = jax.nn.sigmoid(a0_ref[...] + _mm(lo_a, wla_ref[LANES:, :]))
    ktf = k * (1.0 + (a2[:, :BRANCH_W] - 1.0) * k_a)
    ktb = k * (1.0 + (a2[:, BRANCH_W:] - 1.0) * k_a)
    return kk, lw, ktf, kk * a2[:, hs], (ktf + ktb) * rk_ref[...]


def _rwkv_kernel(*refs, has_init, n_prev):
    if has_init:
        s0_ref, refs = refs[0], refs[1:]
    prev, refs = refs[:2 * n_prev], refs[2 * n_prev:]
    cur = (refs[0:5], refs[5:10])
    nxt = (refs[10:12], refs[12:14])
    w = refs[14:21]
    lng_ref, lnb_ref = refs[21:23]
    o_ref, sf_ref, sb_ref, s_scr, ysum_scr, kts_scr, p_scr = refs[23:30]
    c = pl.program_id(1)
    nc = pl.num_programs(1)

    @pl.when(c == 0)
    def _():
        if has_init:
            s_scr[...] = s0_ref[...]
        else:
            s_scr[...] = jnp.zeros_like(s_scr)

    @pl.when((c == 0) & (pl.program_id(0) == 0))
    def _():
        for j in range(SCAN_NB):
            for d in range(2):
                for i, a in enumerate(_rwkv_prepare(cur[d][3][j], cur[d][4][j], d, w)):
                    p_scr[j, d, i] = a

    dirs = []
    for j in range(SCAN_NB):
        for d in range(2):
            kk, lw, kt, bv = [p_scr[j, d, i] for i in range(4)]
            dirs.append((cur[d][0][j], kk, cur[d][1][j], lw, kt, bv, d == 1, j, d))
    kts = [p_scr[j, 0, 4] for j in range(SCAN_NB)]
    ys = _scan_chunks(dirs, s_scr)

    for j in range(SCAN_NB):
        for d in range(2):
            for i, a in enumerate(_rwkv_prepare(nxt[d][0][j], nxt[d][1][j], d, w)):
                p_scr[j, d, i] = a

    rows = (pl.multiple_of(c * SCAN_C, SCAN_C), pl.multiple_of((nc - 1 - c) * SCAN_C, SCAN_C))

    @pl.when(c < nc // 2)
    def _():
        for j in range(SCAN_NB):
            kts_scr[j, pl.ds(rows[0], SCAN_C), :] = kts[j]
            for d in range(2):
                ysum_scr[j, pl.ds(rows[d], SCAN_C), :] = ys[2 * j + d]

    @pl.when(c >= nc // 2)
    def _():
        for j in range(SCAN_NB):
            for d in range(2):
                r, v, g = cur[d][0][j], cur[d][1][j], cur[d][2][j]
                bonus = _head_sums(r * (kts_scr[j, pl.ds(rows[1], SCAN_C), :] if d else kts[j])) * v
                o = ysum_scr[j, pl.ds(rows[d], SCAN_C), :] + ys[2 * j + d]
                dlt = o - _head_sums(o) * (1.0 / N_A)
                var = _head_sums(dlt * dlt) * (1.0 / N_A)
                y = dlt * lax.rsqrt(var + RWKV_GN_EPS) * lng_ref[...] + lnb_ref[...] + bonus
                o_ref[j, pl.ds(rows[d], SCAN_C), :] = (y * _silu(g)).astype(BF16)

    @pl.when(c == nc - 1)
    def _():
        for j in range(SCAN_NB):
            for d, st_ref in enumerate((sf_ref, sb_ref)):
                for i in range(n_prev):
                    st_ref[j, i] = prev[d * n_prev + i][j]
                for p in range(H_A // 2):
                    s = s_scr[j, d, p]
                    st_ref[j, n_prev, 2 * p] = s[:N_A, :N_A]
                    st_ref[j, n_prev, 2 * p + 1] = pltpu.roll(s, N_A, axis=1)[N_A:, :N_A]


def _rwkv(z, wp, s0_bd, nb, seq, prev_f=(), prev_b=()):
    n = z.shape[0]
    nc = seq // SCAN_C
    assert nc % 2 == 0 and nb % SCAN_NB == 0
    z3 = z.reshape(nb, seq, z.shape[-1])
    chunk = ((lambda c: c), (lambda c: nc - 1 - c))
    spec = lambda idx, col, w=BRANCH_W: pl.BlockSpec((SCAN_NB, SCAN_C, w), lambda b, c: (b, idx(c), col))
    in_specs = []
    for d in range(2):
        in_specs += [spec(chunk[d], COL_R), spec(chunk[d], COL_V), spec(chunk[d], COL_GA),
                     spec(chunk[d], COL_K), spec(chunk[d], COL_LORA, 256)]

    def spec_next(d, col, w=BRANCH_W):
        def index(b, c):
            wrap = c == nc - 1
            nb_i = jnp.minimum(b + wrap.astype(jnp.int32), nb // SCAN_NB - 1)
            return (nb_i, chunk[d](jnp.where(wrap, 0, c + 1)), col)
        return pl.BlockSpec((SCAN_NB, SCAN_C, w), index)

    for d in range(2):
        in_specs += [spec_next(d, COL_K), spec_next(d, COL_LORA, 256)]
    full = lambda a: pl.BlockSpec(a.shape, lambda b, c: (0,) * a.ndim)
    params = (wp["wlt"], wp["wla"], wp["w0"], wp["a0"], wp["k_k"], wp["k_a"], wp["r_k"], wp["ln_g"], wp["ln_b"])
    in_specs += [full(p) for p in params]
    args = [z3] * 14 + list(params)
    sspec = pl.BlockSpec((SCAN_NB, 2, H_A // 2, LANES, LANES), lambda b, c: (b, 0, 0, 0, 0))
    has_init = s0_bd is not None
    if has_init:
        in_specs = [sspec] + in_specs
        args = [s0_bd] + args
    n_prev = len(prev_f)
    pspec = pl.BlockSpec((SCAN_NB, H_A, N_A, N_A), lambda b, c: (b, 0, 0, 0))
    at = 1 if has_init else 0
    in_specs[at:at] = [pspec] * (2 * n_prev)
    args[at:at] = list(prev_f) + list(prev_b)
    fspec = pl.BlockSpec((SCAN_NB, n_prev + 1, H_A, N_A, N_A), lambda b, c: (b, 0, 0, 0, 0))
    o_a, sf, sb = pl.pallas_call(
        functools.partial(_rwkv_kernel, has_init=has_init, n_prev=n_prev),
        grid=(nb // SCAN_NB, nc),
        in_specs=in_specs,
        out_specs=[pl.BlockSpec((SCAN_NB, seq, BRANCH_W), lambda b, c: (b, 0, 0)), fspec, fspec],
        out_shape=[jax.ShapeDtypeStruct((nb, seq, BRANCH_W), BF16),
                   jax.ShapeDtypeStruct((nb, n_prev + 1, H_A, N_A, N_A), F32),
                   jax.ShapeDtypeStruct((nb, n_prev + 1, H_A, N_A, N_A), F32)],
        scratch_shapes=[pltpu.VMEM((SCAN_NB, 2, H_A // 2, LANES, LANES), F32),
                        pltpu.VMEM((SCAN_NB, seq, BRANCH_W), F32),
                        pltpu.VMEM((SCAN_NB, seq, BRANCH_W), F32),
                        pltpu.VMEM((SCAN_NB, 2, 5, SCAN_C, BRANCH_W), F32)],
        compiler_params=_cparams(("arbitrary", "arbitrary")),
        name="rwkv",
    )(*args)
    return o_a.reshape(n, BRANCH_W), sf, sb


def _swap_pairs(x):
    w = x.shape[1]
    even = (lax.broadcasted_iota(jnp.int32, x.shape, 1) & 1) == 0
    return jnp.where(even, pltpu.roll(x, w - 1, axis=1), pltpu.roll(x, 1, axis=1))


def _place_rope_key(kr):
    return jnp.concatenate([pltpu.roll(kr, NOPE, axis=1)] * H_B, axis=1)


def _mla_prep_kernel(*refs, rope):
    zb_ref, gq_ref, gkv_ref, wq_ref, wk_ref, wvt_ref = refs[:6]
    if rope:
        cq_ref, sq_ref, ck_ref, sk_ref = refs[6:10]
        refs = refs[10:]
    else:
        refs = refs[6:]
    q_o, k_o, vt_o, ckv_o, kr_o = refs
    zb = zb_ref[...]
    q = _mm(_rms(zb[:, :Q_LORA], gq_ref[...]), wq_ref[...])
    ckv = _rms(zb[:, Q_LORA:Q_LORA + KV_LORA], gkv_ref[...])
    kr = zb[:, Q_LORA + KV_LORA:]
    if rope:
        cq = jnp.concatenate([cq_ref[...]] * H_B, axis=1)
        sq = jnp.concatenate([sq_ref[...]] * H_B, axis=1)
        q = q * cq + _swap_pairs(q) * sq
        kr = kr * ck_ref[...] + _swap_pairs(kr) * sk_ref[...]
    q_o[...] = (q * Q_PRESCALE).astype(BF16)
    k_o[...] = (_mm(ckv, wk_ref[...]) + _place_rope_key(kr)).astype(BF16)
    vt_o[...] = _mm_nt(wvt_ref[...], ckv).astype(BF16)
    ckv_o[...] = ckv
    kr_o[...] = kr[:, :ROPE]


def _mla_prep(z, mp, rope_tabs, seq, tm=1024):
    n = z.shape[0]
    full = lambda a: pl.BlockSpec(a.shape, lambda i: (0,) * a.ndim)
    params = (mp["q_norm"], mp["kv_norm"], mp["wq"], mp["wk"], mp["wvt"])
    in_specs = [pl.BlockSpec((tm, BRANCH_W), lambda i: (i, COL_ZB))] + [full(p) for p in params]
    args = [z, *params]
    rope = rope_tabs is not None
    if rope:
        per = seq // tm
        in_specs += [pl.BlockSpec((tm, LANES), lambda i: (i % per, 0))] * 4
        args += list(rope_tabs)
    rb = lambda w: pl.BlockSpec((tm, w), lambda i: (i, 0))
    return pl.pallas_call(
        functools.partial(_mla_prep_kernel, rope=rope),
        grid=(n // tm,),
        in_specs=in_specs,
        out_specs=[rb(H_B * LANES), rb(H_B * LANES), pl.BlockSpec((H_B * VDIM, tm), lambda i: (0, i)),
                   rb(KV_LORA), rb(ROPE)],
        out_shape=[jax.ShapeDtypeStruct((n, H_B * LANES), BF16),
                   jax.ShapeDtypeStruct((n, H_B * LANES), BF16),
                   jax.ShapeDtypeStruct((H_B * VDIM, n), BF16),
                   jax.ShapeDtypeStruct((n, KV_LORA), F32),
                   jax.ShapeDtypeStruct((n, ROPE), F32)],
        compiler_params=_cparams(("arbitrary",)),
        name="mla_prep",
    )(*args)


def _kv_expand_kernel(ckv_ref, kr_ref, wk_ref, wvt_ref, k_o, vt_o):
    ckv = ckv_ref[...]
    k_o[...] = (_mm(ckv, wk_ref[...]) + _place_rope_key(kr_ref[...])).astype(BF16)
    vt_o[...] = _mm_nt(wvt_ref[...], ckv).astype(BF16)


def _kv_expand(ckv, kr128, mp):
    m = ckv.shape[0]
    full = lambda a: pl.BlockSpec(a.shape, lambda i: (0,) * a.ndim)
    args = (ckv, kr128, mp["wk"], mp["wvt"])
    return pl.pallas_call(
        _kv_expand_kernel,
        grid=(1,),
        in_specs=[full(a) for a in args],
        out_specs=[pl.BlockSpec((m, H_B * LANES), lambda i: (0, 0)),
                   pl.BlockSpec((H_B * VDIM, m), lambda i: (0, 0))],
        out_shape=[jax.ShapeDtypeStruct((m, H_B * LANES), BF16),
                   jax.ShapeDtypeStruct((H_B * VDIM, m), BF16)],
        compiler_params=_cparams(("arbitrary",)),
        name="mla_ctx_kv",
    )(*args)


def _attn_kernel(*refs, has_ctx, group):
    if has_ctx:
        q_ref, k_ref, vt_ref, kc_ref, vct_ref, g_ref, o_ref = refs
    else:
        q_ref, k_ref, vt_ref, g_ref, o_ref = refs
    nbs, seq = k_ref.shape[0], k_ref.shape[1]
    nt = (((1,), (1,)), ((), ()))
    groups = [(b, h0) for b in range(nbs) for h0 in range(0, H_B, group)]

    def scores(b, h0):
        out = []
        for h in range(h0, h0 + group):
            hs = slice(h * LANES, (h + 1) * LANES)
            s = lax.dot_general(k_ref[b, :, hs], q_ref[b, :, hs], nt, preferred_element_type=F32)
            sc = None
            if has_ctx:
                sc = lax.dot_general(kc_ref[:, hs], q_ref[b, :, hs], nt, preferred_element_type=F32)
            out.append((s, sc))
        return out

    pending = scores(*groups[0])
    outs = []
    for gi, (b, h0) in enumerate(groups):
        cur = pending
        if gi + 1 < len(groups):
            pending = scores(*groups[gi + 1])
        keys = slice(b * seq, (b + 1) * seq)
        m, e, ec, den = [], [], [], []
        for s, sc in cur:
            mi = jnp.max(s, axis=0, keepdims=True)
            if has_ctx:
                mi = jnp.maximum(mi, jnp.max(sc, axis=0, keepdims=True))
            m.append(mi)
        for (s, sc), mi in zip(cur, m):
            ei = jnp.exp2(s - mi)
            di = jnp.sum(ei, axis=0, keepdims=True)
            e.append(ei.astype(BF16))
            if has_ctx:
                eci = jnp.exp2(sc - mi)
                di = di + jnp.sum(eci, axis=0, keepdims=True)
                ec.append(eci.astype(BF16))
            den.append(di)
        for i in range(group):
            vh = slice((h0 + i) * VDIM, (h0 + i + 1) * VDIM)
            o = jnp.dot(vt_ref[vh, keys], e[i], preferred_element_type=F32)
            if has_ctx:
                o = o + jnp.dot(vct_ref[vh, :], ec[i], preferred_element_type=F32)
            outs.append(o / den[i])
        if h0 + group == H_B:
            o_ref[b] = (jnp.concatenate(outs, axis=0).T * _silu(g_ref[b].astype(F32))).astype(BF16)
            outs = []


def _attention(z, q, k, vt, kc, vct, nb, seq):
    n = z.shape[0]
    tq = min(seq, ATTN_TQ)
    nq = seq // tq
    has_ctx = kc is not None
    nbs = 1 if has_ctx else max(1, min(nb, ATTN_ROWS // seq))
    z3 = z.reshape(nb, seq, z.shape[-1])
    in_specs = [pl.BlockSpec((nbs, tq, H_B * LANES), lambda b, i: (b, i, 0)),
                pl.BlockSpec((nbs, seq, H_B * LANES), lambda b, i: (b, 0, 0)),
                pl.BlockSpec((H_B * VDIM, nbs * seq), lambda b, i: (0, b))]
    args = [q.reshape(nb, seq, H_B * LANES), k.reshape(nb, seq, H_B * LANES), vt]
    if has_ctx:
        past = kc.shape[0] // nb
        in_specs += [pl.BlockSpec((past, H_B * LANES), lambda b, i: (b, 0)),
                     pl.BlockSpec((H_B * VDIM, past), lambda b, i: (0, b))]
        args += [kc, vct]
    in_specs.append(pl.BlockSpec((nbs, tq, BRANCH_W), lambda b, i: (b, i, COL_GB)))
    args.append(z3)
    out = pl.pallas_call(
        functools.partial(_attn_kernel, has_ctx=has_ctx, group=ATTN_GROUP_LONG if has_ctx else ATTN_GROUP),
        grid=(nb // nbs, nq),
        in_specs=in_specs,
        out_specs=pl.BlockSpec((nbs, tq, BRANCH_W), lambda b, i: (b, i, 0)),
        out_shape=jax.ShapeDtypeStruct((nb, seq, BRANCH_W), BF16),
        compiler_params=_cparams(("arbitrary", "arbitrary")),
        name="mla_attention",
    )(*args)
    return out.reshape(n, BRANCH_W)


def _gmlp_kernel(u_ref, vc_ref, g_ref, lng_ref, lnb_ref, ws_ref, bs_ref, o_ref):
    tm = u_ref.shape[0]
    u = jax.nn.gelu(u_ref[...].astype(F32))
    x = jax.nn.gelu(vc_ref[...].astype(F32))
    mu = jnp.mean(x, axis=-1, keepdims=True)
    xc = x - mu
    var = jnp.mean(xc * xc, axis=-1, keepdims=True)
    vn = (xc * lax.rsqrt(var + 1e-5) * lng_ref[...] + lnb_ref[...]).astype(BF16)
    rows = []
    for c in range(tm // CHUNK):
        cols = []
        for g in range(G_C):
            blk = vn[c * CHUNK:(c + 1) * CHUNK, g * LANES:(g + 1) * LANES]
            cols.append(jnp.dot(ws_ref[g], blk, preferred_element_type=F32))
        rows.append(jnp.concatenate(cols, axis=1) + bs_ref[...])
    mixed = jnp.concatenate(rows, axis=0)
    o_ref[...] = (u * mixed * _silu(g_ref[...].astype(F32))).astype(BF16)


def _gmlp(z, gp, tm=1024):
    n = z.shape[0]
    zb = lambda c: pl.BlockSpec((tm, BRANCH_W), lambda i, c=c: (i, c))
    full = lambda a: pl.BlockSpec(a.shape, lambda i: (0,) * a.ndim)
    params = (gp["ln_g"], gp["ln_b"], gp["w_s"], gp["b_s"])
    return pl.pallas_call(
        _gmlp_kernel,
        grid=(n // tm,),
        in_specs=[zb(COL_U), zb(COL_VC), zb(COL_GC)] + [full(p) for p in params],
        out_specs=pl.BlockSpec((tm, BRANCH_W), lambda i: (i, 0)),
        out_shape=jax.ShapeDtypeStruct((n, BRANCH_W), BF16),
        compiler_params=_cparams(("arbitrary",)),
        name="gmlp",
    )(z, z, z, *params)


def _fnet_kernel(f_ref, g_ref, cs_ref, dft_ref, o_ref, xcs_scr, *, seq):
    r = pl.program_id(1)
    nbs = f_ref.shape[0]

    @pl.when(r == 0)
    def _():
        for b in range(nbs):
            for g in range(G_D):
                xg = _mm(f_ref[b, :, g * LANES:(g + 1) * LANES], cs_ref[...])
                xcs_scr[b, 0:seq, g * LANES:(g + 1) * LANES] = xg[:, :LANES].astype(BF16)
                xcs_scr[b, seq:2 * seq, g * LANES:(g + 1) * LANES] = xg[:, LANES:].astype(BF16)

    for b in range(nbs):
        y = jnp.dot(dft_ref[...], xcs_scr[b], preferred_element_type=F32)
        o_ref[b] = (y * (1.0 / math.sqrt(seq * LANES)) * _silu(g_ref[b].astype(F32))).astype(BF16)


def _fnet(z, cs128, dft, nb, seq):
    n = z.shape[0]
    tr = min(seq, FNET_ROWS)
    nr = seq // tr
    nbs = max(1, min(nb, FRONT_TM // seq))
    z3 = z.reshape(nb, seq, z.shape[-1])
    out = pl.pallas_call(
        functools.partial(_fnet_kernel, seq=seq),
        grid=(nb // nbs, nr),
        in_specs=[pl.BlockSpec((nbs, seq, BRANCH_W), lambda b, r: (b, 0, COL_F)),
                  pl.BlockSpec((nbs, tr, BRANCH_W), lambda b, r: (b, r, COL_GD)),
                  pl.BlockSpec((LANES, 2 * LANES), lambda b, r: (0, 0)),
                  pl.BlockSpec((tr, 2 * seq), lambda b, r: (r, 0))],
        out_specs=pl.BlockSpec((nbs, tr, BRANCH_W), lambda b, r: (b, r, 0)),
        out_shape=jax.ShapeDtypeStruct((nb, seq, BRANCH_W), BF16),
        scratch_shapes=[pltpu.VMEM((nbs, 2 * seq, BRANCH_W), BF16)],
        compiler_params=_cparams(("arbitrary", "arbitrary")),
        name="fnet",
    )(z3, z3, cs128, dft)
    return out.reshape(n, BRANCH_W)


def _back_kernel(x_ref, h_ref, gate_ref, oa_ref, ob_ref, oc_ref, od_ref, wm_ref, bm_ref,
                 wb_ref, wo_ref, fg_ref, y_ref, acc_scr, *, final):
    nb = pl.program_id(1)
    tm = x_ref.shape[0]
    chunks = [pl.ds(r, BACK_ROWS) for r in range(0, tm, BACK_ROWS)]

    for i, o_ref in enumerate((oa_ref, ob_ref, oc_ref, od_ref)):
        @pl.when(nb == i)
        def _(o_ref=o_ref, i=i):
            for rows in chunks:
                gates = jax.nn.sigmoid(jnp.dot(h_ref[rows, :], wm_ref[...], preferred_element_type=F32)
                                       + bm_ref[...])
                upd = gates * jnp.dot(o_ref[rows, :], wb_ref[0], preferred_element_type=F32)
                if i == 0:
                    acc_scr[rows, :] = upd
                else:
                    acc_scr[rows, :] += upd

    @pl.when(nb == 3)
    def _():
        for rows in chunks:
            xn = x_ref[rows, :] + gate_ref[0] * _mm(acc_scr[rows, :], wo_ref[...])
            if final:
                xn = _rms(xn, fg_ref[...])
            y_ref[rows, :] = xn


def _back(x2d, h, gate, o_a, o_b, o_c, o_d, wts, l, final_g, final, tm=1024):
    n = x2d.shape[0]
    rows_per_gate = FRONT_TM // tm
    ob = pl.BlockSpec((tm, BRANCH_W), lambda i, j: (i, 0))
    return pl.pallas_call(
        functools.partial(_back_kernel, final=final),
        grid=(n // tm, 4),
        in_specs=[pl.BlockSpec((tm, D_MODEL), lambda i, j: (i, 0)),
                  pl.BlockSpec((tm, D_MODEL), lambda i, j: (i, 0)),
                  pl.BlockSpec((1, 1, D_MODEL), lambda i, j: (i // rows_per_gate, 0, 0)),
                  ob, ob, ob, ob,
                  pl.BlockSpec((None, D_MODEL, D_MODEL), lambda i, j: (l, 0, j)),
                  pl.BlockSpec((None, 1, D_MODEL), lambda i, j: (l, 0, j)),
                  pl.BlockSpec((None, 1, BRANCH_W, D_MODEL), lambda i, j: (l, j, 0, 0)),
                  pl.BlockSpec((None, D_MODEL, D_MODEL), lambda i, j: (l, 0, 0)),
                  pl.BlockSpec((1, D_MODEL), lambda i, j: (0, 0))],
        out_specs=pl.BlockSpec((tm, D_MODEL), lambda i, j: (i, 0)),
        out_shape=jax.ShapeDtypeStruct((n, D_MODEL), F32),
        scratch_shapes=[pltpu.VMEM((tm, D_MODEL), F32)],
        compiler_params=_cparams(("arbitrary", "arbitrary")),
        name="back",
    )(x2d, h, gate, o_a, o_b, o_c, o_d, wts["w_merge"], wts["b_merge"], wts["w_branch"],
      wts["w_out"], final_g)


def _pad_cols_kernel(w_ref, o_ref):
    rows = w_ref.shape[0]
    o_ref[:, :SHIFT_W] = w_ref[:, :SHIFT_W].astype(BF16)
    o_ref[:, SHIFT_W:Z_SHIFT_PAD] = jnp.zeros((rows, Z_SHIFT_PAD - SHIFT_W), BF16)
    o_ref[:, Z_SHIFT_PAD:Z_SHIFT_PAD + 928] = w_ref[:, SHIFT_W:2656].astype(BF16)
    o_ref[:, Z_SHIFT_PAD + 928:3072] = jnp.zeros((rows, 3072 - Z_SHIFT_PAD - 928), BF16)
    o_ref[:, 3072:] = w_ref[:, 2656:].astype(BF16)


def _pad_cols(w, l, tm=256):
    _, k, n_in = w.shape
    return pl.pallas_call(
        _pad_cols_kernel,
        grid=(k // tm,),
        in_specs=[pl.BlockSpec((tm, n_in), lambda i: (l * (k // tm) + i, 0))],
        out_specs=pl.BlockSpec((tm, ZW), lambda i: (i, 0)),
        out_shape=jax.ShapeDtypeStruct((k, ZW), BF16),
        compiler_params=_cparams(("arbitrary",)),
        name="pad_w_in",
    )(w.reshape(-1, n_in))


def _pack_q(w):
    w = w.reshape(Q_LORA, H_B, NOPE + ROPE)
    return jnp.pad(w, ((0, 0), (0, 0), (0, LANES - NOPE - ROPE))).reshape(Q_LORA, H_B * LANES)


def _pack_kv(w):
    w = w.reshape(KV_LORA, H_B, NOPE + VDIM)
    wk = jnp.pad(w[..., :NOPE], ((0, 0), (0, 0), (0, LANES - NOPE))).reshape(KV_LORA, H_B * LANES)
    return wk, w[..., NOPE:].reshape(KV_LORA, H_B * VDIM)


def _rope_tables(n_tokens):
    rows = n_tokens // GRID_W
    row = jnp.repeat(jnp.arange(rows, dtype=F32), GRID_W)
    col = jnp.tile(jnp.arange(GRID_W, dtype=F32), rows)
    n_freq = ROPE // 4
    inv = ROPE_BASE ** (-jnp.arange(n_freq, dtype=F32) / n_freq)
    ang = jnp.concatenate([row[:, None] * inv, col[:, None] * inv], axis=-1)
    cos = jnp.repeat(jnp.cos(ang), 2, axis=-1)
    sin = jnp.repeat(jnp.sin(ang), 2, axis=-1) * jnp.tile(jnp.asarray([-1.0, 1.0], F32), ROPE // 2)
    ones = lambda k: jnp.ones((n_tokens, k), F32)
    zeros = lambda k: jnp.zeros((n_tokens, k), F32)
    cq = jnp.concatenate([ones(NOPE), cos, ones(LANES - NOPE - ROPE)], axis=-1)
    sq = jnp.concatenate([zeros(NOPE), sin, zeros(LANES - NOPE - ROPE)], axis=-1)
    ck = jnp.concatenate([cos, ones(LANES - ROPE)], axis=-1)
    sk = jnp.concatenate([sin, zeros(LANES - ROPE)], axis=-1)
    return cq, sq, ck, sk


def _dft_tables(seq):
    k = np.arange(LANES)
    a = 2.0 * np.pi * ((k[:, None] * k[None, :]) % LANES) / LANES
    cs128 = np.concatenate([np.cos(a), np.sin(a)], axis=1)
    t = np.arange(seq)
    b = 2.0 * np.pi * ((t[:, None] * t[None, :]) % seq) / seq
    dft = np.concatenate([np.cos(b), -np.sin(b)], axis=1)
    return (jnp.asarray(cs128.astype(np.float32)).astype(BF16),
            jnp.asarray(dft.astype(np.float32)).astype(BF16))


def _lora_weights(w_up, a_up):
    z = jnp.zeros((LORA, BRANCH_W), F32)
    pad = jnp.zeros((256 - 3 * LORA, 2 * BRANCH_W), F32)
    wlt = jnp.concatenate([jnp.concatenate([w_up[0], z], 1), jnp.concatenate([z, w_up[1]], 1),
                           jnp.concatenate([z, z], 1), pad], axis=0)
    wla = jnp.concatenate([jnp.concatenate([z, z], 1), jnp.concatenate([z, z], 1),
                           jnp.concatenate([a_up[0], a_up[1]], 1), pad], axis=0)
    return wlt.astype(BF16), wla.astype(BF16)


def _to_blockdiag(s):
    b = s.shape[0]
    s = s.reshape(b, H_A // 2, 2, N_A, N_A)
    z = jnp.zeros_like(s[:, :, 0])
    top = jnp.concatenate([s[:, :, 0], z], axis=-1)
    bot = jnp.concatenate([z, s[:, :, 1]], axis=-1)
    return jnp.concatenate([top, bot], axis=-2)


def _trunk_layer(x2d, ss, lw, wts, l, nb, seq, rope_tabs, ctx, final_g, prev_f=(), prev_b=()):
    z, zl, h = _front(x2d, ss[:, :2], lw["norm_g"], lw["w_in"], lw["mu"], seq)
    s0 = None
    if ctx is not None:
        s0 = jnp.stack([_to_blockdiag(ctx[0]), _to_blockdiag(ctx[1])], axis=1)
    o_a, sf, sb = _rwkv(z, lw["rwkv"], s0, nb, seq, prev_f, prev_b)
    q, k, v, ckv, kr = _mla_prep(z, lw["mla"], rope_tabs, seq)
    kc = vc = None
    if ctx is not None:
        past = ctx[2].shape[1]
        kr_c = jnp.pad(ctx[3].reshape(nb * past, ROPE), ((0, 0), (0, LANES - ROPE)))
        kc, vc = _kv_expand(ctx[2].reshape(nb * past, KV_LORA), kr_c, lw["mla"])
    o_b = _attention(zl, q, k, v, kc, vc, nb, seq)
    o_c = _gmlp(zl, lw["gmlp"])
    o_d = _fnet(zl, *lw["dft"][seq], nb, seq)
    x_new = _back(x2d, h, ss[:, 2:], o_a, o_b, o_c, o_d, wts, l, final_g, l == DEPTH - 1)
    return x_new, sf, sb, ckv, kr


def kernel(x_prompt, x_sample, state_rwkv_fwd, state_rwkv_bwd, cache_mla_ckv, cache_mla_krope, c, c_ctx, norm_g, w_ada, b_ada, w_in, shift_mu, rwkv_w0, rwkv_w_up, rwkv_a0, rwkv_a_up, rwkv_k_k, rwkv_k_a, rwkv_r_k, rwkv_ln_g, rwkv_ln_b, mla_q_norm, mla_w_q_up, mla_kv_norm, mla_w_kv_up, gmlp_ln_g, gmlp_ln_b, gmlp_w_s, gmlp_b_s, w_branch, w_merge, b_merge, w_out, final_norm_g):
    nb_c, seq_c, _ = x_prompt.shape
    nb_l, seq_l, _ = x_sample.shape
    assert (nb_c * seq_c) % FRONT_TM == 0 and FRONT_TM % seq_c == 0 and seq_l == FRONT_TM

    cond8 = jnp.concatenate([c_ctx[None], c, jnp.zeros((8 - 1 - nb_l, D_MODEL), F32)], axis=0)
    mod = _modulation(cond8, w_ada, b_ada).reshape(DEPTH, 8, 3, D_MODEL)
    n_ctx_tiles = nb_c * seq_c // FRONT_TM

    rope_tabs = _rope_tables(seq_l)
    dft = {s: _dft_tables(s) for s in {seq_c, seq_l}}
    mu_p = jnp.pad(shift_mu, ((0, 0), (0, Z_SHIFT_PAD - SHIFT_W)))[:, None]
    final_g = final_norm_g[None]
    wts = {"w_merge": w_merge.astype(BF16), "b_merge": b_merge[:, None], "w_branch": w_branch.astype(BF16),
           "w_out": w_out.astype(BF16)}

    xc = x_prompt.reshape(nb_c * seq_c, D_MODEL)
    xl = x_sample.reshape(nb_l * seq_l, D_MODEL)
    sf = sb = None
    ckv_list, kr_list = [], []
    for l in range(DEPTH):
        wlt, wla = _lora_weights(rwkv_w_up[l], rwkv_a_up[l])
        wk, wv = _pack_kv(mla_w_kv_up[l])
        lw = {
            "norm_g": norm_g[l][None], "w_in": _pad_cols(w_in, l), "mu": mu_p[l],
            "rwkv": {
                "wlt": wlt, "wla": wla, "w0": rwkv_w0[l].reshape(1, 2 * BRANCH_W),
                "a0": rwkv_a0[l].reshape(1, 2 * BRANCH_W), "k_k": rwkv_k_k[l][None],
                "k_a": rwkv_k_a[l][None], "r_k": rwkv_r_k[l].reshape(1, BRANCH_W),
                "ln_g": rwkv_ln_g[l][None], "ln_b": rwkv_ln_b[l][None],
            },
            "mla": {
                "q_norm": mla_q_norm[l][None], "kv_norm": mla_kv_norm[l][None],
                "wq": _pack_q(mla_w_q_up[l]).astype(BF16), "wk": wk.astype(BF16),
                "wvt": wv.T.astype(BF16),
            },
            "gmlp": {
                "ln_g": gmlp_ln_g[l][None], "ln_b": gmlp_ln_b[l][None],
                "w_s": gmlp_w_s[l].astype(BF16),
                "b_s": jnp.repeat(gmlp_b_s[l].T, BRANCH_W // G_C, axis=1),
            },
            "dft": dft,
        }
        ss_c = jnp.broadcast_to(mod[l, 0][None], (n_ctx_tiles, 3, D_MODEL))
        ss_l = mod[l, 1:1 + nb_l]
        xc, sf, sb, ckv, kr = _trunk_layer(xc, ss_c, lw, wts, l, nb_c, seq_c, None, None, final_g,
                                           [sf[:, i] for i in range(l)] if l else (),
                                           [sb[:, i] for i in range(l)] if l else ())
        ckv_list.append(ckv.reshape(nb_c, seq_c, KV_LORA))
        kr_list.append(kr.reshape(nb_c, seq_c, ROPE))
        ctx = (state_rwkv_fwd[:, l], state_rwkv_bwd[:, l], cache_mla_ckv[:, l], cache_mla_krope[:, l])
        xl = _trunk_layer(xl, ss_l, lw, wts, l, nb_l, seq_l, rope_tabs, ctx, final_g)[0]

    return (xc.reshape(nb_c, seq_c, D_MODEL), xl.reshape(nb_l, seq_l, D_MODEL),
            sf, sb,
            jnp.stack(ckv_list, axis=1), jnp.stack(kr_list, axis=1))
```
